```python
import jax, jax.numpy as jnp
from jax import lax
import numpy as np

D_MODEL = 1024
BATCH = 8
SEQ = 2048
DEPTH = 4
DEC_BATCH = 128
DEC_SEQ = 1
PAST_LEN = 16384
PAGE_SIZE = 128

N_META = 16
N_MIXERS = 3
N_RET_LAYERS = (DEPTH + N_MIXERS - 1) // N_MIXERS
RET_HEADS = 4
RET_DK = D_MODEL // RET_HEADS
RET_DV = 2 * D_MODEL // RET_HEADS
RET_CHUNK = 128
ROPE_BASE = 10000.0
RWKV_HEAD = 64
RWKV_HEADS = D_MODEL // RWKV_HEAD
RWKV_DECAY_LORA = 64
RWKV_A_LORA = 64
RWKV_GATE_LORA = 160
CONV_WIDTH = 3
D_FF = 4 * D_MODEL
EPS = 1e-6
RWKV_GN_EPS = 64e-5

kernel_name = 'hybrid_retention_rwkv7_shortconv_step'


def rmsnorm(x, g):
    xf = x.astype(jnp.float32)
    y = xf * lax.rsqrt(jnp.mean(xf * xf, axis=-1, keepdims=True) + EPS)
    return (y * g.astype(jnp.float32)).astype(x.dtype)


def rotate_pairs(x, pos):
    inv = 1.0 / (ROPE_BASE ** jnp.linspace(0.0, 1.0, RET_DK // 2, dtype=jnp.float32))
    ang = pos.astype(jnp.float32)[:, None] * inv[None, :]
    cos = jnp.cos(ang)[None, :, None, :]
    sin = jnp.sin(ang)[None, :, None, :]
    xf = x.astype(jnp.float32).reshape(x.shape[:-1] + (RET_DK // 2, 2))
    x1, x2 = xf[..., 0], xf[..., 1]
    return jnp.stack([x1 * cos - x2 * sin, x1 * sin + x2 * cos], axis=-1).reshape(x.shape)


def retention_chunk(S, q, k, v, log_gamma):
    L = q.shape[1]
    idx = jnp.arange(L, dtype=jnp.float32)
    diff = idx[:, None] - idx[None, :]
    mask = jnp.where(diff >= 0, jnp.exp(jnp.maximum(diff, 0.0)[None] * log_gamma[:, None, None]), 0.0)
    scores = jnp.einsum('blhd,bmhd->bhlm', q, k) * mask[None]
    inner = jnp.einsum('bhlm,bmhe->blhe', scores, v)
    q_decay = jnp.exp((idx + 1.0)[:, None] * log_gamma[None, :])
    cross = jnp.einsum('blhd,bhde->blhe', q * q_decay[None, :, :, None], S)
    k_decay = jnp.exp((L - 1.0 - idx)[:, None] * log_gamma[None, :])
    S_new = S * jnp.exp(L * log_gamma)[None, :, None, None] + jnp.einsum('blhd,blhe->bhde', k * k_decay[None, :, :, None], v)
    return S_new, inner + cross


def retention_blocks(S, q, k, v, log_gamma):
    B, T = q.shape[0], q.shape[1]
    n_full = T // RET_CHUNK
    outs = []
    if n_full > 0:
        def to_blocks(a):
            return a[:, :n_full * RET_CHUNK].reshape((B, n_full, RET_CHUNK) + a.shape[2:]).swapaxes(0, 1)

        def step(carry, qkv):
            return retention_chunk(carry, qkv[0], qkv[1], qkv[2], log_gamma)

        S, o = lax.scan(step, S, (to_blocks(q), to_blocks(k), to_blocks(v)))
        outs.append(o.swapaxes(0, 1).reshape(B, n_full * RET_CHUNK, RET_HEADS, RET_DV))
    if T % RET_CHUNK:
        t0 = n_full * RET_CHUNK
        S, o = retention_chunk(S, q[:, t0:], k[:, t0:], v[:, t0:], log_gamma)
        outs.append(o)
    return S, jnp.concatenate(outs, axis=1)


def retention_mixer(xn, S0, pos, lead, w_in, w_out):
    B, T, _ = xn.shape
    hk, hv = RET_HEADS * RET_DK, RET_HEADS * RET_DV
    q, k, v, g = jnp.split(xn @ w_in, [hk, 2 * hk, 2 * hk + hv], axis=-1)
    q = rotate_pairs(q.reshape(B, T, RET_HEADS, RET_DK), pos)
    k = rotate_pairs(k.reshape(B, T, RET_HEADS, RET_DK), pos) * (RET_DK ** -0.5)
    v = v.reshape(B, T, RET_HEADS, RET_DV).astype(jnp.float32)
    log_gamma = jnp.log(1.0 - 2.0 ** (-5.0 - jnp.arange(RET_HEADS, dtype=jnp.float32)))
    S = S0.astype(jnp.float32)
    if lead > 0:
        S, o_lead = retention_chunk(S, q[:, :lead], k[:, :lead], v[:, :lead], log_gamma)
        S, o_rest = retention_blocks(S, q[:, lead:], k[:, lead:], v[:, lead:], log_gamma)
        o = jnp.concatenate([o_lead, o_rest], axis=1)
    else:
        S, o = retention_blocks(S, q, k, v, log_gamma)
    o = o * lax.rsqrt(jnp.mean(o * o, axis=-1, keepdims=True) + EPS)
    y = jax.nn.silu(g.astype(jnp.float32)) * o.reshape(B, T, hv)
    return y.astype(xn.dtype) @ w_out, S.astype(xn.dtype)


def wkv7_scan(S0, r, decay, k, v, kk, a):
    def step(S, inp):
        r_t, w_t, k_t, v_t, kk_t, a_t = inp
        sa = jnp.einsum('bhvk,bhk->bhv', S, -kk_t)
        S = S * w_t[:, :, None, :] + sa[..., None] * (kk_t * a_t)[:, :, None, :] + v_t[..., None] * k_t[:, :, None, :]
        return S, jnp.einsum('bhvk,bhk->bhv', S, r_t)

    xs = tuple(t.swapaxes(0, 1) for t in (r, decay, k, v, kk, a))
    S, y = lax.scan(step, S0, xs)
    return S, y.swapaxes(0, 1)


def rwkv7_mixer(xn, shift_prev, S0, mix, w_rkv, w0, w1, w2, a0, a1, a2, g1, g2, k_k, k_a, r_k, ln_g, ln_b, w_o):
    B, T, D = xn.shape
    H, N = RWKV_HEADS, RWKV_HEAD
    x_prev = jnp.concatenate([shift_prev[:, None].astype(xn.dtype), xn[:, :-1]], axis=1)
    xx = x_prev - xn
    xm = xn[:, :, None, :] + xx[:, :, None, :] * mix[None, None]
    r, k, v = jnp.split(jnp.einsum('btjd,jde->btje', xm[:, :, :3], w_rkv), 3, axis=2)
    r, k, v = r[:, :, 0].astype(jnp.float32), k[:, :, 0].astype(jnp.float32), v[:, :, 0].astype(jnp.float32)
    xw, xa, xg = xm[:, :, 3], xm[:, :, 4], xm[:, :, 5]
    w = -jax.nn.softplus(-(w0 + jnp.tanh(xw @ w1) @ w2).astype(jnp.float32)) - 0.5
    decay = jnp.exp(-jnp.exp(w))
    a = jax.nn.sigmoid((a0 + (xa @ a1) @ a2).astype(jnp.float32))
    g = (jax.nn.sigmoid(xg @ g1) @ g2).astype(jnp.float32)
    kk = (k * k_k.astype(jnp.float32)).reshape(B, T, H, N)
    kk = kk / jnp.maximum(jnp.sqrt(jnp.sum(kk * kk, axis=-1, keepdims=True)), 1e-12)
    k = k * (1.0 + (a - 1.0) * k_a.astype(jnp.float32))
    hs = lambda t: t.reshape(B, T, H, N)
    r, decay, k, v, a = hs(r), hs(decay), hs(k), hs(v), hs(a)
    S, y = wkv7_scan(S0.astype(jnp.float32), r, decay, k, v, kk, a)
    mu = jnp.mean(y, axis=-1, keepdims=True)
    var = jnp.mean((y - mu) ** 2, axis=-1, keepdims=True)
    y = ((y - mu) * lax.rsqrt(var + RWKV_GN_EPS)).reshape(B, T, D) * ln_g.astype(jnp.float32) + ln_b.astype(jnp.float32)
    bonus = jnp.sum(r * k * r_k.astype(jnp.float32)[None, None], axis=-1, keepdims=True) * v
    y = (y + bonus.reshape(B, T, D)) * g
    return y.astype(xn.dtype) @ w_o, xn[:, -1], S.astype(xn.dtype)


def short_conv_mixer(xn, buf, w_in, conv_w, w_out):
    T = xn.shape[1]
    b, c, h = jnp.split(xn @ w_in, 3, axis=-1)
    u = c * h
    up = jnp.concatenate([buf.astype(u.dtype), u], axis=1)
    y = conv_w[0] * up[:, :T] + conv_w[1] * up[:, 1:T + 1] + conv_w[2] * up[:, 2:T + 2]
    return (b * y) @ w_out, up[:, -(CONV_WIDTH - 1):]


def sq_relu_mlp(xn, w1, w2):
    return jnp.square(jax.nn.relu(xn @ w1)) @ w2


def trunk(h, pos, lead, ret_states, rwkv_shift, rwkv_state, conv_buf,
          norm_mix, norm_mlp, norm_final, ret_w_in, ret_w_out,
          rwkv_mix, rwkv_w_rkv, rwkv_w0, rwkv_w1, rwkv_w2, rwkv_a0, rwkv_a1, rwkv_a2,
          rwkv_g1, rwkv_g2, rwkv_k_k, rwkv_k_a, rwkv_r_k, rwkv_ln_g, rwkv_ln_b, rwkv_w_o,
          conv_w_in, conv_w, conv_w_out, mlp_w1, mlp_w2):
    new_ret = []
    for i in range(DEPTH):
        xn = rmsnorm(h, norm_mix[i])
        kind = i % N_MIXERS
        if kind == 0:
            j = i // N_MIXERS
            out, s = retention_mixer(xn, ret_states[j], pos, lead, ret_w_in[j], ret_w_out[j])
            new_ret.append(s)
        elif kind == 1:
            out, rwkv_shift, rwkv_state = rwkv7_mixer(xn, rwkv_shift, rwkv_state, rwkv_mix, rwkv_w_rkv, rwkv_w0, rwkv_w1, rwkv_w2,
                                                      rwkv_a0, rwkv_a1, rwkv_a2, rwkv_g1, rwkv_g2, rwkv_k_k, rwkv_k_a, rwkv_r_k,
                                                      rwkv_ln_g, rwkv_ln_b, rwkv_w_o)
        else:
            out, conv_buf = short_conv_mixer(xn, conv_buf, conv_w_in, conv_w, conv_w_out)
        h = h + out
        h = h + sq_relu_mlp(rmsnorm(h, norm_mlp[i]), mlp_w1[i], mlp_w2[i])
    return rmsnorm(h, norm_final), new_ret, rwkv_shift, rwkv_state, conv_buf


def setup_inputs(seed: int = 0) -> dict:
    key = jax.random.key(seed)
    ks = iter(jax.random.split(key, 48))

    def nrm(shape, scale):
        return jax.random.normal(next(ks), shape, jnp.float32) * scale

    D = D_MODEL
    hk, hv = RET_HEADS * RET_DK, RET_HEADS * RET_DV
    return {
        'x_prompt': nrm((BATCH, SEQ, D), 1.0),
        'x_sample': nrm((DEC_BATCH, DEC_SEQ, D), 1.0),
        'state_ret_l0': nrm((DEC_BATCH, RET_HEADS, RET_DK, RET_DV), 0.5),
        'state_rwkv_shift_l1': nrm((DEC_BATCH, D), 1.0),
        'state_rwkv_wkv_l1': nrm((DEC_BATCH, RWKV_HEADS, RWKV_HEAD, RWKV_HEAD), 0.3),
        'state_conv_l2': nrm((DEC_BATCH, CONV_WIDTH - 1, D), 1.0),
        'state_ret_l3': nrm((DEC_BATCH, RET_HEADS, RET_DK, RET_DV), 0.5),
        'meta_tokens': nrm((N_META, D), 1.0),
        'norm_mix': 1.0 + nrm((DEPTH, D), 0.02),
        'norm_mlp': 1.0 + nrm((DEPTH, D), 0.02),
        'norm_final': 1.0 + nrm((D,), 0.02),
        'ret_w_in': nrm((N_RET_LAYERS, D, 2 * hk + 2 * hv), D ** -0.5),
        'ret_w_out': nrm((N_RET_LAYERS, hv, D), hv ** -0.5),
        'rwkv_mix': jax.random.uniform(next(ks), (6, D), jnp.float32, 0.0, 1.0),
        'rwkv_w_rkv': nrm((3, D, D), D ** -0.5),
        'rwkv_w0': jnp.linspace(-6.0, -1.0, D, dtype=jnp.float32) + nrm((D,), 0.1),
        'rwkv_w1': nrm((D, RWKV_DECAY_LORA), D ** -0.5),
        'rwkv_w2': nrm((RWKV_DECAY_LORA, D), 0.1 * RWKV_DECAY_LORA ** -0.5),
        'rwkv_a0': nrm((D,), 0.1),
        'rwkv_a1': nrm((D, RWKV_A_LORA), D ** -0.5),
        'rwkv_a2': nrm((RWKV_A_LORA, D), 0.5 * RWKV_A_LORA ** -0.5),
        'rwkv_g1': nrm((D, RWKV_GATE_LORA), D ** -0.5),
        'rwkv_g2': nrm((RWKV_GATE_LORA, D), RWKV_GATE_LORA ** -0.5),
        'rwkv_k_k': 0.85 + nrm((D,), 0.05),
        'rwkv_k_a': 1.0 + nrm((D,), 0.05),
        'rwkv_r_k': nrm((RWKV_HEADS, RWKV_HEAD), 0.1),
        'rwkv_ln_g': 1.0 + nrm((D,), 0.02),
        'rwkv_ln_b': nrm((D,), 0.02),
        'rwkv_w_o': nrm((D, D), D ** -0.5),
        'conv_w_in': nrm((D, 3 * D), D ** -0.5),
        'conv_w': nrm((CONV_WIDTH, D), CONV_WIDTH ** -0.5),
        'conv_w_out': nrm((D, D), D ** -0.5),
        'mlp_w1': nrm((DEPTH, D, D_FF), D ** -0.5),
        'mlp_w2': nrm((DEPTH, D_FF, D), D_FF ** -0.5),
    }


def reference(x_prompt, x_sample, state_ret_l0, state_rwkv_shift_l1, state_rwkv_wkv_l1, state_conv_l2, state_ret_l3,
              meta_tokens, norm_mix, norm_mlp, norm_final, ret_w_in, ret_w_out,
              rwkv_mix, rwkv_w_rkv, rwkv_w0, rwkv_w1, rwkv_w2, rwkv_a0, rwkv_a1, rwkv_a2,
              rwkv_g1, rwkv_g2, rwkv_k_k, rwkv_k_a, rwkv_r_k, rwkv_ln_g, rwkv_ln_b, rwkv_w_o,
              conv_w_in, conv_w, conv_w_out, mlp_w1, mlp_w2):
    weights = (norm_mix, norm_mlp, norm_final, ret_w_in, ret_w_out,
               rwkv_mix, rwkv_w_rkv, rwkv_w0, rwkv_w1, rwkv_w2, rwkv_a0, rwkv_a1, rwkv_a2,
               rwkv_g1, rwkv_g2, rwkv_k_k, rwkv_k_a, rwkv_r_k, rwkv_ln_g, rwkv_ln_b, rwkv_w_o,
               conv_w_in, conv_w, conv_w_out, mlp_w1, mlp_w2)
    dt = x_prompt.dtype
    B = x_prompt.shape[0]

    h_p = jnp.concatenate([jnp.broadcast_to(meta_tokens[None].astype(dt), (B, N_META, D_MODEL)), x_prompt], axis=1)
    pos_p = jnp.arange(SEQ + N_META)
    zero_ret = [jnp.zeros((B, RET_HEADS, RET_DK, RET_DV), dt) for _ in range(N_RET_LAYERS)]
    out_p, ret_p, shift_p, wkv_p, conv_p = trunk(
        h_p, pos_p, N_META, zero_ret, jnp.zeros((B, D_MODEL), dt),
        jnp.zeros((B, RWKV_HEADS, RWKV_HEAD, RWKV_HEAD), dt), jnp.zeros((B, CONV_WIDTH - 1, D_MODEL), dt), *weights)
    y_prompt = out_p[:, N_META:]

    pos_s = PAST_LEN + jnp.arange(DEC_SEQ)
    y_sample, ret_s, shift_s, wkv_s, conv_s = trunk(
        x_sample, pos_s, 0, [state_ret_l0, state_ret_l3], state_rwkv_shift_l1, state_rwkv_wkv_l1, state_conv_l2, *weights)

    return (y_prompt, y_sample, ret_p[0], ret_s[0], shift_p, shift_s, wkv_p, wkv_s, conv_p, conv_s, ret_p[1], ret_s[1])
```

```python
import functools
import math

import jax
import jax.numpy as jnp
from jax import lax
from jax.experimental import pallas as pl
from jax.experimental.pallas import tpu as pltpu

F32 = jnp.float32
BF16 = jnp.bfloat16

D = 1024
N_META = 16
RH = 4
DK = D // RH
DV = 2 * D // RH
HK = RH * DK
HV = RH * DV
RET_CHUNK = 128
ROPE_BASE = 10000.0
WH = 16
WN = 64
WKV_CHUNK = 64
D_FF = 4 * D
EPS = 1e-6
GN_EPS = 64e-5
LANES = 128
VMEM_LIMIT_V7X = 56 * 1024 * 1024

NT = (((1,), (1,)), ((), ()))
TN = (((0,), (0,)), ((), ()))


def _cp(n_axes):
    return pltpu.CompilerParams(dimension_semantics=("arbitrary",) * n_axes,
                                vmem_limit_bytes=VMEM_LIMIT_V7X)


def _dot(a, b):
    return jnp.dot(a, b, preferred_element_type=F32)


def _rms(x, g):
    return x * lax.rsqrt(jnp.mean(x * x, axis=-1, keepdims=True) + EPS) * g


def _sigmoid(x):
    return 1.0 / (1.0 + jnp.exp(-x))


def _split(x):
    hi = x.astype(BF16)
    lo = (x - hi.astype(F32)).astype(BF16)
    return hi, lo


def _seg_sum(x, eh, eht):
    hi, lo = _split(x)
    s = _dot(hi, eh) + _dot(lo, eh)
    shi, slo = _split(s)
    return _dot(shi, eht) + _dot(slo, eht)


def _norm_matmul_kernel(x_ref, g_ref, w_ref, cos_ref, sin_ref, o_ref, xn_ref, *, tn, n_rope_tiles, k_tile_start):
    j = pl.program_id(1)

    @pl.when(j == 0)
    def _():
        xn_ref[...] = _rms(x_ref[...], g_ref[...]).astype(BF16)

    acc = _dot(xn_ref[...], w_ref[...])
    if n_rope_tiles == 0:
        o_ref[...] = acc.astype(o_ref.dtype)
        return

    @pl.when(j < n_rope_tiles)
    def _():
        c = cos_ref[...]
        s = sin_ref[...]
        scale = jnp.where(j >= k_tile_start, DK ** -0.5, 1.0).astype(F32)
        half = DK // 2
        for hh in range(tn // DK):
            x1 = acc[:, hh * DK:hh * DK + half]
            x2 = acc[:, hh * DK + half:(hh + 1) * DK]
            o_ref[:, hh * DK:hh * DK + half] = ((x1 * c - x2 * s) * scale).astype(o_ref.dtype)
            o_ref[:, hh * DK + half:(hh + 1) * DK] = ((x1 * s + x2 * c) * scale).astype(o_ref.dtype)

    @pl.when(j >= n_rope_tiles)
    def _():
        o_ref[...] = acc.astype(o_ref.dtype)


def _norm_matmul(x, g, w, *, tm, tn, out_dtype, rope=None):
    M, K = x.shape
    N = w.shape[1]
    if rope is None:
        cos = sin = jnp.zeros((8, LANES), F32)
        cs_spec = pl.BlockSpec((8, LANES), lambda i, j: (0, 0))
        n_rope_tiles = k_tile_start = 0
    else:
        cos, sin = rope
        nblk = cos.shape[0] // tm
        cs_spec = pl.BlockSpec((tm, LANES), lambda i, j: (i % nblk, 0))
        n_rope_tiles = 2 * HK // tn
        k_tile_start = HK // tn
    return pl.pallas_call(
        functools.partial(_norm_matmul_kernel, tn=tn, n_rope_tiles=n_rope_tiles, k_tile_start=k_tile_start),
        grid=(M // tm, N // tn),
        in_specs=[pl.BlockSpec((tm, K), lambda i, j: (i, 0)),
                  pl.BlockSpec((1, K), lambda i, j: (0, 0)),
                  pl.BlockSpec((K, tn), lambda i, j: (0, j)),
                  cs_spec, cs_spec],
        out_specs=pl.BlockSpec((tm, tn), lambda i, j: (i, j)),
        out_shape=jax.ShapeDtypeStruct((M, N), out_dtype),
        scratch_shapes=[pltpu.VMEM((tm, K), BF16)],
        compiler_params=_cp(2),
        name="norm_matmul",
    )(x, g, w, cos, sin)


def _norm_matmul_t_kernel(x_ref, g_ref, wt_ref, cos_ref, sin_ref, o_ref):
    xn = _rms(x_ref[...], g_ref[...]).astype(BF16)
    acc = lax.dot_general(wt_ref[...], xn, NT, preferred_element_type=F32)
    c = cos_ref[...]
    s = sin_ref[...]
    half = DK // 2
    for hh in range(2 * RH):
        scale = 1.0 if hh < RH else DK ** -0.5
        x1 = acc[hh * DK:hh * DK + half, :]
        x2 = acc[hh * DK + half:(hh + 1) * DK, :]
        o_ref[hh * DK:hh * DK + half, :] = (x1 * c - x2 * s) * scale
        o_ref[hh * DK + half:(hh + 1) * DK, :] = (x1 * s + x2 * c) * scale


def _norm_matmul_t(x, g, wt, cos_t, sin_t, *, rows):
    K = x.shape[1]
    return pl.pallas_call(
        _norm_matmul_t_kernel,
        grid=(1,),
        in_specs=[pl.BlockSpec((rows, K), lambda i: (0, 0)),
                  pl.BlockSpec((1, K), lambda i: (0, 0)),
                  pl.BlockSpec((2 * HK, K), lambda i: (0, 0)),
                  pl.BlockSpec((DK // 2, rows), lambda i: (0, 0)),
                  pl.BlockSpec((DK // 2, rows), lambda i: (0, 0))],
        out_specs=pl.BlockSpec((2 * HK, rows), lambda i: (0, 0)),
        out_shape=jax.ShapeDtypeStruct((2 * HK, rows), F32),
        compiler_params=_cp(1),
        name="norm_matmul_t",
    )(x, g, wt, cos_t, sin_t)


def _mlp_kernel(h_ref, g_ref, w1_ref, w2_ref, o_ref, xn_ref, acc_ref):
    f = pl.program_id(1)

    @pl.when(f == 0)
    def _():
        xn_ref[...] = _rms(h_ref[...], g_ref[...]).astype(BF16)
        acc_ref[...] = jnp.zeros_like(acc_ref)

    a = _dot(xn_ref[...], w1_ref[...])
    a = jnp.square(jnp.maximum(a, 0.0)).astype(BF16)
    acc_ref[...] += _dot(a, w2_ref[...])

    @pl.when(f == pl.num_programs(1) - 1)
    def _():
        o_ref[...] = h_ref[...] + acc_ref[...]


def _mlp(h, g, w1, w2, *, tm, tf):
    M = h.shape[0]
    return pl.pallas_call(
        _mlp_kernel,
        grid=(M // tm, D_FF // tf),
        in_specs=[pl.BlockSpec((tm, D), lambda i, f: (i, 0)),
                  pl.BlockSpec((1, D), lambda i, f: (0, 0)),
                  pl.BlockSpec((D, tf), lambda i, f: (0, f)),
                  pl.BlockSpec((tf, D), lambda i, f: (f, 0))],
        out_specs=pl.BlockSpec((tm, D), lambda i, f: (i, 0)),
        out_shape=jax.ShapeDtypeStruct((M, D), F32),
        scratch_shapes=[pltpu.VMEM((tm, D), BF16), pltpu.VMEM((tm, D), F32)],
        compiler_params=_cp(2),
        name="mlp",
    )(h, g, w1, w2)


def _matmul_res_kernel(a_ref, w_ref, h_ref, o_ref):
    o_ref[...] = h_ref[...] + _dot(a_ref[...].astype(BF16), w_ref[...])


def _matmul_res(a, w, h, *, tm):
    M, K = a.shape
    return pl.pallas_call(
        _matmul_res_kernel,
        grid=(M // tm,),
        in_specs=[pl.BlockSpec((tm, K), lambda i: (i, 0)),
                  pl.BlockSpec((K, D), lambda i: (0, 0)),
                  pl.BlockSpec((tm, D), lambda i: (i, 0))],
        out_specs=pl.BlockSpec((tm, D), lambda i: (i, 0)),
        out_shape=jax.ShapeDtypeStruct((M, D), F32),
        compiler_params=_cp(1),
        name="matmul_res",
    )(a, w, h)


def _final_norm_kernel(h_ref, g_ref, o_ref):
    o_ref[...] = _rms(h_ref[...], g_ref[...])


def _final_norm(h, g, *, rows, tm):
    return pl.pallas_call(
        _final_norm_kernel,
        grid=(rows // tm,),
        in_specs=[pl.BlockSpec((tm, D), lambda i: (i, 0)),
                  pl.BlockSpec((1, D), lambda i: (0, 0))],
        out_specs=pl.BlockSpec((tm, D), lambda i: (i, 0)),
        out_shape=jax.ShapeDtypeStruct((rows, D), F32),
        compiler_params=_cp(1),
        name="final_norm",
    )(h, g)


def _log_gamma(h):
    return math.log(1.0 - 2.0 ** (-5.0 - h))


def _ret_chunk_kernel(p_ref, s0_ref, y_ref, so_ref, s_ref, *, L, nC):
    c = pl.program_id(1)
    half = DK // 2

    @pl.when(c == 0)
    def _():
        for h in range(RH):
            s_ref[h, 0:half, :] = s0_ref[0, h, :, 0:DV]
            s_ref[h, half:DK, :] = s0_ref[0, h, :, DV:2 * DV]

    ti = lax.broadcasted_iota(jnp.int32, (L, L), 0)
    si = lax.broadcasted_iota(jnp.int32, (L, L), 1)
    diff = (ti - si).astype(F32)
    ri = lax.broadcasted_iota(jnp.int32, (L, 1), 0).astype(F32)
    for h in range(RH):
        lg = _log_gamma(h)
        mask = jnp.where(diff >= 0, jnp.exp(jnp.maximum(diff, 0.0) * lg), 0.0)
        qd = jnp.exp((ri + 1.0) * lg)
        kd = jnp.exp((L - 1.0 - ri) * lg)
        q = p_ref[:, h * DK:(h + 1) * DK]
        k = p_ref[:, HK + h * DK:HK + (h + 1) * DK]
        v = p_ref[:, 2 * HK + h * DV:2 * HK + (h + 1) * DV].astype(BF16)
        g = p_ref[:, 2 * HK + HV + h * DV:2 * HK + HV + (h + 1) * DV].astype(F32)
        sc = lax.dot_general(q.astype(BF16), k.astype(BF16), NT, preferred_element_type=F32) * mask
        inner = _dot(sc.astype(BF16), v)
        s_old = s_ref[h]
        cross = _dot((q.astype(F32) * qd).astype(BF16), s_old.astype(BF16))
        o = inner + cross
        s_ref[h] = s_old * math.exp(L * lg) + lax.dot_general(
            (k.astype(F32) * kd).astype(BF16), v, TN, preferred_element_type=F32)
        o = o * lax.rsqrt(jnp.mean(o * o, axis=-1, keepdims=True) + EPS)
        y_ref[:, h * DV:(h + 1) * DV] = (g * _sigmoid(g) * o).astype(y_ref.dtype)

    @pl.when(c == nC - 1)
    def _():
        for h in range(RH):
            so_ref[0, h, :, 0:DV] = s_ref[h, 0:half, :]
            so_ref[0, h, :, DV:2 * DV] = s_ref[h, half:DK, :]


def _ret_chunk(p, s0, *, nb, nC, L, row_block0, out_dtype):
    shared = s0.shape[0] == 1
    return pl.pallas_call(
        functools.partial(_ret_chunk_kernel, L=L, nC=nC),
        grid=(nb, nC),
        in_specs=[pl.BlockSpec((L, 2 * HK + 2 * HV), lambda b, c: (row_block0 + b * nC + c, 0)),
                  pl.BlockSpec((1, RH, DK // 2, 2 * DV), lambda b, c: (0 if shared else b, 0, 0, 0))],
        out_specs=[pl.BlockSpec((L, HV), lambda b, c: (b * nC + c, 0)),
                   pl.BlockSpec((1, RH, DK // 2, 2 * DV), lambda b, c: (b, 0, 0, 0))],
        out_shape=[jax.ShapeDtypeStruct((nb * nC * L, HV), out_dtype),
                   jax.ShapeDtypeStruct((nb, RH, DK // 2, 2 * DV), F32)],
        scratch_shapes=[pltpu.VMEM((RH, DK, DV), F32)],
        compiler_params=_cp(2),
        name="ret_chunk",
    )(p, s0)


def _ret_dec_kernel(qkt_ref, p_ref, s_ref, y_ref, so_ref, *, bb, nbatch):
    step = pl.program_id(0)
    lane = lax.broadcasted_iota(jnp.int32, (1, nbatch), 1)
    half = DK // 2

    def body(jb, carry):
        b = step * bb + jb
        onehot = (lane == b).astype(F32)
        rows = []
        for h in range(RH):
            gam = math.exp(_log_gamma(h))
            vrow = p_ref[pl.ds(b, 1), 2 * HK + h * DV:2 * HK + (h + 1) * DV]
            o = jnp.zeros((1, DV), F32)
            for hf in range(2):
                r0 = h * DK + hf * half
                qcol = jnp.sum(qkt_ref[r0:r0 + half, :] * onehot, axis=1, keepdims=True)
                kcol = jnp.sum(qkt_ref[HK + r0:HK + r0 + half, :] * onehot, axis=1, keepdims=True)
                sn = gam * s_ref[jb, h, :, hf * DV:(hf + 1) * DV] + kcol * vrow
                so_ref[jb, h, :, hf * DV:(hf + 1) * DV] = sn
                o = o + jnp.sum(qcol * sn, axis=0, keepdims=True)
            o = o * lax.rsqrt(jnp.mean(o * o, axis=-1, keepdims=True) + EPS)
            g = p_ref[pl.ds(b, 1), 2 * HK + HV + h * DV:2 * HK + HV + (h + 1) * DV]
            rows.append(g * _sigmoid(g) * o)
        y_ref[b] = jnp.concatenate(rows, axis=1)
        return carry

    lax.fori_loop(0, bb, body, 0)


def _ret_dec(qkt, p, s, *, bb):
    nbatch = s.shape[0]
    return pl.pallas_call(
        functools.partial(_ret_dec_kernel, bb=bb, nbatch=nbatch),
        grid=(nbatch // bb,),
        in_specs=[pl.BlockSpec(qkt.shape, lambda i: (0, 0)),
                  pl.BlockSpec(p.shape, lambda i: (0, 0)),
                  pl.BlockSpec((bb, RH, DK // 2, 2 * DV), lambda i: (i, 0, 0, 0))],
        out_specs=[pl.BlockSpec((nbatch, 1, HV), lambda i: (0, 0, 0)),
                   pl.BlockSpec((bb, RH, DK // 2, 2 * DV), lambda i: (i, 0, 0, 0))],
        out_shape=[jax.ShapeDtypeStruct((nbatch, 1, HV), F32),
                   jax.ShapeDtypeStruct(s.shape, F32)],
        compiler_params=_cp(1),
        name="ret_dec",
    )(qkt, p, s)


N_RWKV_OUT = 8


def _rwkv_core(xn, xprev, w_refs, out_refs):
    (mix_ref, wrkv_ref, w0_ref, w1_ref, w2_ref, a0_ref, a1_ref, a2_ref, g1_ref, g2_ref,
     kk_ref, ka_ref, rk_ref, eh_ref, eht_ref) = w_refs
    r_o, lw_o, k_o, v_o, kk_o, b_o, g_o, bonus_o = out_refs
    xx = xprev - xn

    def xm(j):
        return (xn + xx * mix_ref[j:j + 1, :]).astype(BF16)

    r = _dot(xm(0), wrkv_ref[0])
    k = _dot(xm(1), wrkv_ref[1])
    v = _dot(xm(2), wrkv_ref[2])
    wl = _dot(jnp.tanh(_dot(xm(3), w1_ref[...])).astype(BF16), w2_ref[...])
    nx = -(w0_ref[...] + wl)
    softplus = jnp.maximum(nx, 0.0) + jnp.log(1.0 + jnp.exp(-jnp.abs(nx)))
    w = -softplus - 0.5
    lw_o[...] = -jnp.exp(w)
    al = _dot(_dot(xm(4), a1_ref[...]).astype(BF16), a2_ref[...])
    a = _sigmoid(a0_ref[...] + al)
    g_o[...] = _dot(_sigmoid(_dot(xm(5), g1_ref[...])).astype(BF16), g2_ref[...]).astype(g_o.dtype)
    kk = k * kk_ref[...]
    ssq = _seg_sum(kk * kk, eh_ref[...], eht_ref[...])
    kkn = kk / jnp.maximum(jnp.sqrt(ssq), 1e-12)
    k2 = k * (1.0 + (a - 1.0) * ka_ref[...])
    rk = _seg_sum(r * k2 * rk_ref[...], eh_ref[...], eht_ref[...])
    r_o[...] = r.astype(r_o.dtype)
    k_o[...] = k2.astype(k_o.dtype)
    v_o[...] = v.astype(v_o.dtype)
    kk_o[...] = kkn.astype(kk_o.dtype)
    b_o[...] = (kkn * a).astype(b_o.dtype)
    bonus_o[...] = (rk * v).astype(bonus_o.dtype)


def _rwkv_proj_main_kernel(h_ref, hp_ref, sh0_ref, g_ref, *refs, tm, tps):
    w_refs = refs[:15]
    out_refs = refs[15:15 + N_RWKV_OUT]
    sho_ref = refs[15 + N_RWKV_OUT]
    i = pl.program_id(0)
    xn = _rms(h_ref[...], g_ref[...])
    prevn = _rms(hp_ref[...], g_ref[...])[7:8, :]
    prev = jnp.where(i % tps == 0, sh0_ref[...], prevn)
    row = lax.broadcasted_iota(jnp.int32, (tm, 1), 0)
    xprev = jnp.where(row == 0, prev, pltpu.roll(xn, 1, 0))
    _rwkv_core(xn, xprev, w_refs, out_refs)
    sho_ref[0] = xn[tm - 1:tm, :]


def _rwkv_proj_small_kernel(h_ref, shp_ref, g_ref, *refs, n_dec):
    w_refs = refs[:15]
    out_refs = refs[15:15 + N_RWKV_OUT]
    xn_ref = refs[15 + N_RWKV_OUT]
    m = h_ref.shape[0]
    xn = _rms(h_ref[...], g_ref[...])
    row = lax.broadcasted_iota(jnp.int32, (m, 1), 0)
    xprev = jnp.where(row <= n_dec, shp_ref[...], pltpu.roll(xn, 1, 0))
    _rwkv_core(xn, xprev, w_refs, out_refs)
    xn_ref[...] = xn


def _rwkv_weight_specs(wts, nidx):
    zero = (lambda *idx: (0, 0))
    zero3 = (lambda *idx: (0, 0, 0))
    specs = []
    for w in wts:
        specs.append(pl.BlockSpec(w.shape, zero3 if w.ndim == 3 else zero))
    return specs


def _rwkv_proj_main(h, sh0, g, wts, *, tm, seq):
    M = h.shape[0]
    tps = seq // tm
    nseq = M // seq
    odt = [BF16, F32, BF16, BF16, BF16, BF16, BF16, BF16]
    row_spec = pl.BlockSpec((tm, D), lambda i: (i, 0))
    return pl.pallas_call(
        functools.partial(_rwkv_proj_main_kernel, tm=tm, tps=tps),
        grid=(M // tm,),
        in_specs=[row_spec,
                  pl.BlockSpec((8, D), lambda i: (jnp.maximum(i * (tm // 8) - 1, 0), 0)),
                  pl.BlockSpec((1, D), lambda i: (0, 0)),
                  pl.BlockSpec((1, D), lambda i: (0, 0))] + _rwkv_weight_specs(wts, 1),
        out_specs=[row_spec] * N_RWKV_OUT + [pl.BlockSpec((1, 1, D), lambda i: (i // tps, 0, 0))],
        out_shape=[jax.ShapeDtypeStruct((M, D), dt) for dt in odt] + [jax.ShapeDtypeStruct((nseq, 1, D), F32)],
        compiler_params=_cp(1),
        name="rwkv_proj_main",
    )(h, h, sh0, g, *wts)


def _rwkv_proj_small(h, shp, g, wts, *, n_dec):
    M = h.shape[0]
    full = pl.BlockSpec((M, D), lambda i: (0, 0))
    return pl.pallas_call(
        functools.partial(_rwkv_proj_small_kernel, n_dec=n_dec),
        grid=(1,),
        in_specs=[full, full, pl.BlockSpec((1, D), lambda i: (0, 0))] + _rwkv_weight_specs(wts, 1),
        out_specs=[full] * (N_RWKV_OUT + 1),
        out_shape=[jax.ShapeDtypeStruct((M, D), F32)] * (N_RWKV_OUT + 1),
        compiler_params=_cp(1),
        name="rwkv_proj_small",
    )(h, shp, g, *wts)


def _mm3(a, b):
    ah, al = _split(a)
    bh, bl = _split(b)
    return _dot(ah, bh) + _dot(ah, bl) + _dot(al, bh)


def _tri_inv(n, C):
    ti = lax.broadcasted_iota(jnp.int32, (C, C), 0)
    si = lax.broadcasted_iota(jnp.int32, (C, C), 1)
    t = jnp.where(ti == si, 1.0, 0.0).astype(F32) + n
    p = n
    for _ in range(int(math.log2(C)) - 1):
        p = _mm3(p, p)
        t = t + _mm3(t, p)
    return t


def _wkv_chunk_kernel(r_ref, lw_ref, k_ref, v_ref, kk_ref, b_ref, s0_ref, y_ref, so_ref, s_ref, *, C, nC):
    c = pl.program_id(1)

    @pl.when(c == 0)
    def _():
        s_ref[...] = s0_ref[0]

    ti = lax.broadcasted_iota(jnp.int32, (C, C), 0)
    si = lax.broadcasted_iota(jnp.int32, (C, C), 1)
    incl = si <= ti
    strict = si < ti
    tri = jnp.where(incl, 1.0, 0.0).astype(BF16)

    lw = lw_ref[...].astype(F32)
    l0 = lw.astype(BF16)
    rem = lw - l0.astype(F32)
    l1 = rem.astype(BF16)
    l2 = (rem - l1.astype(F32)).astype(BF16)
    cum = _dot(tri, l0) + _dot(tri, l1) + _dot(tri, l2)
    tot = cum[C - 1:C, :]
    w_in = jnp.exp(-cum)
    w_end = jnp.exp(tot - cum)
    kf = k_ref[...].astype(F32)
    bf = b_ref[...].astype(F32)
    rt = (r_ref[...].astype(F32) * jnp.exp(cum)).astype(BF16)
    at = (-kk_ref[...].astype(F32) * jnp.exp(cum - lw)).astype(BF16)
    kt = (kf * w_in).astype(BF16)
    bt = (bf * w_in).astype(BF16)
    ke = (kf * w_end).astype(BF16)
    be = (bf * w_end).astype(BF16)
    wc = jnp.exp(tot)
    vb = v_ref[...].astype(BF16)

    for h in range(WH):
        sl = slice(h * WN, (h + 1) * WN)
        lhs = jnp.concatenate([at[:, sl], rt[:, sl]], axis=0)
        rhs = jnp.concatenate([kt[:, sl], bt[:, sl]], axis=0)
        gm = lax.dot_general(lhs, rhs, NT, preferred_element_type=F32)
        a_ak = jnp.where(strict, gm[0:C, 0:C], 0.0)
        a_ab = jnp.where(strict, gm[0:C, C:2 * C], 0.0)
        a_rk = jnp.where(incl, gm[C:2 * C, 0:C], 0.0)
        a_rb = jnp.where(incl, gm[C:2 * C, C:2 * C], 0.0)
        tinv = _tri_inv(a_ab, C)
        s_old = s_ref[h]
        lh = lax.dot_general(lhs, s_old.astype(BF16), NT, preferred_element_type=F32)
        vh = vb[:, sl]
        u = _mm3(tinv, lh[0:C] + _dot(a_ak.astype(BF16), vh))
        vu = jnp.concatenate([vh, u.astype(BF16)], axis=0)
        y = lh[C:2 * C] + _dot(jnp.concatenate([a_rk, a_rb], axis=1).astype(BF16), vu)
        y_ref[:, sl] = y
        s_ref[h] = s_old * wc[:, sl] + lax.dot_general(
            vu, jnp.concatenate([ke[:, sl], be[:, sl]], axis=0), TN, preferred_element_type=F32)

    @pl.when(c == nC - 1)
    def _():
        so_ref[0] = s_ref[...]


def _wkv_chunk(ins, s0, *, nb, nC, C, row_block0):
    shared = s0.shape[0] == 1
    row_spec = pl.BlockSpec((C, D), lambda b, c: (row_block0 + b * nC + c, 0))
    return pl.pallas_call(
        functools.partial(_wkv_chunk_kernel, C=C, nC=nC),
        grid=(nb, nC),
        in_specs=[row_spec] * 6 + [pl.BlockSpec((1, WH, WN, WN), lambda b, c: (0 if shared else b, 0, 0, 0))],
        out_specs=[pl.BlockSpec((C, D), lambda b, c: (b * nC + c, 0)),
                   pl.BlockSpec((1, WH, WN, WN), lambda b, c: (b, 0, 0, 0))],
        out_shape=[jax.ShapeDtypeStruct((nb * nC * C, D), F32),
                   jax.ShapeDtypeStruct((nb, WH, WN, WN), F32)],
        scratch_shapes=[pltpu.VMEM((WH, WN, WN), F32)],
        compiler_params=_cp(2),
        name="wkv_chunk",
    )(*ins, s0)


def _wkv_dec_kernel(r_ref, lw_ref, k_ref, v_ref, kk_ref, b_ref, s_ref, y_ref, so_ref, *, bb):
    step = pl.program_id(0)
    eye = jnp.where(lax.broadcasted_iota(jnp.int32, (WN, WN), 0) == lax.broadcasted_iota(jnp.int32, (WN, WN), 1),
                    1.0, 0.0).astype(F32)

    def body(jb, carry):
        b = step * bb + jb
        rr = r_ref[pl.ds(b, 1), :]
        ww = jnp.exp(lw_ref[pl.ds(b, 1), :])
        k2 = k_ref[pl.ds(b, 1), :]
        vv = v_ref[pl.ds(b, 1), :]
        kk = kk_ref[pl.ds(b, 1), :]
        bv = b_ref[pl.ds(b, 1), :]
        rows = []
        for h in range(WH):
            sl = slice(h * WN, (h + 1) * WN)
            s_old = s_ref[jb, h]
            sa = jnp.sum(s_old * (-kk[:, sl]), axis=1, keepdims=True)
            vcol = jnp.sum(eye * vv[:, sl], axis=1, keepdims=True)
            sn = s_old * ww[:, sl] + sa * bv[:, sl] + vcol * k2[:, sl]
            so_ref[jb, h] = sn
            ycol = jnp.sum(sn * rr[:, sl], axis=1, keepdims=True)
            rows.append(jnp.sum(eye * ycol, axis=0, keepdims=True))
        y_ref[b] = jnp.concatenate(rows, axis=1)
        return carry

    lax.fori_loop(0, bb, body, 0)


def _wkv_dec(ins, s, *, bb):
    nbatch = s.shape[0]
    full = pl.BlockSpec(ins[0].shape, lambda i: (0, 0))
    st = pl.BlockSpec((bb, WH, WN, WN), lambda i: (i, 0, 0, 0))
    return pl.pallas_call(
        functools.partial(_wkv_dec_kernel, bb=bb),
        grid=(nbatch // bb,),
        in_specs=[full] * 6 + [st],
        out_specs=[pl.BlockSpec((nbatch, 1, D), lambda i: (0, 0, 0)), st],
        out_shape=[jax.ShapeDtypeStruct((nbatch, 1, D), F32), jax.ShapeDtypeStruct(s.shape, F32)],
        compiler_params=_cp(1),
        name="wkv_dec",
    )(*ins, s)


def _rwkv_out_kernel(y_ref, bonus_ref, g_ref, h_ref, lng_ref, lnb_ref, wo_ref, eh_ref, eht_ref, o_ref):
    y = y_ref[...]
    eh = eh_ref[...]
    eht = eht_ref[...]
    mu = _seg_sum(y, eh, eht) * (1.0 / WN)
    d = y - mu
    var = _seg_sum(d * d, eh, eht) * (1.0 / WN)
    yn = d * lax.rsqrt(var + GN_EPS) * lng_ref[...] + lnb_ref[...]
    z = (yn + bonus_ref[...].astype(F32)) * g_ref[...].astype(F32)
    o_ref[...] = h_ref[...] + _dot(z.astype(BF16), wo_ref[...])


def _rwkv_out(y, bonus, g, h, lng, lnb, wo, eh, eht, *, tm):
    M = h.shape[0]
    row = pl.BlockSpec((tm, D), lambda i: (i, 0))
    vec = pl.BlockSpec((1, D), lambda i: (0, 0))
    return pl.pallas_call(
        _rwkv_out_kernel,
        grid=(M // tm,),
        in_specs=[row, row, row, row, vec, vec,
                  pl.BlockSpec((D, D), lambda i: (0, 0)),
                  pl.BlockSpec(eh.shape, lambda i: (0, 0)),
                  pl.BlockSpec(eht.shape, lambda i: (0, 0))],
        out_specs=row,
        out_shape=jax.ShapeDtypeStruct((M, D), F32),
        compiler_params=_cp(1),
        name="rwkv_out",
    )(y, bonus, g, h, lng, lnb, wo, eh, eht)


def _conv_main_kernel(p_ref, pp_ref, buf_ref, cw_ref, wo_ref, h_ref, o_ref, nb_ref, *, tm, tps):
    i = pl.program_id(0)
    bq = p_ref[:, 0:D].astype(F32)
    u = p_ref[:, D:2 * D].astype(F32) * p_ref[:, 2 * D:3 * D].astype(F32)
    up = pp_ref[:, D:2 * D].astype(F32) * pp_ref[:, 2 * D:3 * D].astype(F32)
    npr = pp_ref.shape[0]
    first = i % tps == 0
    prev1 = jnp.where(first, buf_ref[1:2, :], up[npr - 1:npr, :])
    prev2 = jnp.where(first, buf_ref[0:1, :], up[npr - 2:npr - 1, :])
    row = lax.broadcasted_iota(jnp.int32, (tm, 1), 0)
    m1 = jnp.where(row == 0, prev1, pltpu.roll(u, 1, 0))
    m2 = jnp.where(row == 0, prev2, jnp.where(row == 1, prev1, pltpu.roll(u, 2, 0)))
    y = cw_ref[0:1, :] * m2 + cw_ref[1:2, :] * m1 + cw_ref[2:3, :] * u
    o_ref[...] = h_ref[...] + _dot((bq * y).astype(BF16), wo_ref[...])
    nb_ref[0] = u[tm - 2:tm, :]


def _conv_main(p, buf, cw, wo, h, *, tm, seq):
    M = h.shape[0]
    tps = seq // tm
    npr = 16
    return pl.pallas_call(
        functools.partial(_conv_main_kernel, tm=tm, tps=tps),
        grid=(M // tm,),
        in_specs=[pl.BlockSpec((tm, 3 * D), lambda i: (i, 0)),
                  pl.BlockSpec((npr, 3 * D), lambda i: (jnp.maximum(i * (tm // npr) - 1, 0), 0)),
                  pl.BlockSpec((2, D), lambda i: (0, 0)),
                  pl.BlockSpec((3, D), lambda i: (0, 0)),
                  pl.BlockSpec((D, D), lambda i: (0, 0)),
                  pl.BlockSpec((tm, D), lambda i: (i, 0))],
        out_specs=[pl.BlockSpec((tm, D), lambda i: (i, 0)),
                   pl.BlockSpec((1, 2, D), lambda i: (i // tps, 0, 0))],
        out_shape=[jax.ShapeDtypeStruct((M, D), F32), jax.ShapeDtypeStruct((M // seq, 2, D), F32)],
        compiler_params=_cp(1),
        name="conv_main",
    )(p, p, buf, cw, wo, h)


def _conv_small_kernel(p_ref, b0_ref, b1_ref, cw_ref, wo_ref, h_ref, o_ref, u_ref, *, n_dec):
    m = h_ref.shape[0]
    bq = p_ref[:, 0:D]
    u = p_ref[:, D:2 * D] * p_ref[:, 2 * D:3 * D]
    row = lax.broadcasted_iota(jnp.int32, (m, 1), 0)
    m1 = jnp.where(row <= n_dec, b1_ref[...], pltpu.roll(u, 1, 0))
    m2 = jnp.where(row <= n_dec + 1, b0_ref[...], pltpu.roll(u, 2, 0))
    y = cw_ref[0:1, :] * m2 + cw_ref[1:2, :] * m1 + cw_ref[2:3, :] * u
    o_ref[...] = h_ref[...] + _dot((bq * y).astype(BF16), wo_ref[...])
    u_ref[...] = u


def _conv_small(p, b0, b1, cw, wo, h, *, n_dec):
    M = h.shape[0]
    full = pl.BlockSpec((M, D), lambda i: (0, 0))
    return pl.pallas_call(
        functools.partial(_conv_small_kernel, n_dec=n_dec),
        grid=(1,),
        in_specs=[pl.BlockSpec((M, 3 * D), lambda i: (0, 0)), full, full,
                  pl.BlockSpec((3, D), lambda i: (0, 0)),
                  pl.BlockSpec((D, D), lambda i: (0, 0)), full],
        out_specs=[full, full],
        out_shape=[jax.ShapeDtypeStruct((M, D), F32)] * 2,
        compiler_params=_cp(1),
        name="conv_small",
    )(p, b0, b1, cw, wo, h)


def _deinterleave_cols(w):
    K = w.shape[0]
    return w.reshape(K, RH, DK // 2, 2).transpose(0, 1, 3, 2).reshape(K, RH * DK)


def _rope_tables(pos):
    inv = 1.0 / (ROPE_BASE ** jnp.linspace(0.0, 1.0, DK // 2, dtype=F32))
    ang = pos.astype(F32)[:, None] * inv[None, :]
    return jnp.cos(ang), jnp.sin(ang)


def kernel(x_prompt, x_sample, state_ret_l0, state_rwkv_shift_l1, state_rwkv_wkv_l1, state_conv_l2, state_ret_l3,
           meta_tokens, norm_mix, norm_mlp, norm_final, ret_w_in, ret_w_out, rwkv_mix, rwkv_w_rkv, rwkv_w0, rwkv_w1,
           rwkv_w2, rwkv_a0, rwkv_a1, rwkv_a2, rwkv_g1, rwkv_g2, rwkv_k_k, rwkv_k_a, rwkv_r_k, rwkv_ln_g, rwkv_ln_b,
           rwkv_w_o, conv_w_in, conv_w, conv_w_out, mlp_w1, mlp_w2):
    B, T, _ = x_prompt.shape
    NB = x_sample.shape[0]
    MS = NB + N_META
    past_len = 16384

    def ret_w(j):
        w = ret_w_in[j]
        return jnp.concatenate([_deinterleave_cols(w[:, :HK]), _deinterleave_cols(w[:, HK:2 * HK]), w[:, 2 * HK:]],
                               axis=1).astype(BF16)

    ret_wp = [ret_w(0), ret_w(1)]
    ret_wt = [w[:, :2 * HK].T for w in ret_wp]
    ret_wo = [ret_w_out[0].astype(BF16), ret_w_out[1].astype(BF16)]
    w1b = mlp_w1.astype(BF16)
    w2b = mlp_w2.astype(BF16)
    head_of_lane = jnp.arange(D) // WN
    eh = (head_of_lane[:, None] == jnp.arange(LANES)[None, :]).astype(BF16)
    eht = eh.T
    vec = lambda a: a.reshape(1, D).astype(F32)
    rwkv_wts = (rwkv_mix.astype(F32), rwkv_w_rkv.astype(BF16), vec(rwkv_w0), rwkv_w1.astype(BF16), rwkv_w2.astype(BF16),
                vec(rwkv_a0), rwkv_a1.astype(BF16), rwkv_a2.astype(BF16), rwkv_g1.astype(BF16), rwkv_g2.astype(BF16),
                vec(rwkv_k_k), vec(rwkv_k_a), vec(rwkv_r_k), eh, eht)
    wo_rwkv = rwkv_w_o.astype(BF16)
    conv_wi = conv_w_in.astype(BF16)
    conv_wo = conv_w_out.astype(BF16)
    conv_wf = conv_w.astype(F32)

    cos_m, sin_m = _rope_tables(N_META + jnp.arange(T))
    pos_s = jnp.concatenate([jnp.full((NB,), past_len, jnp.int32), jnp.arange(N_META, dtype=jnp.int32)])
    cos_s, sin_s = _rope_tables(pos_s)
    cos_t, sin_t = cos_s[:NB].T, sin_s[:NB].T

    def ret_view(s):
        return s.reshape(s.shape[0], RH, DK // 2, 2 * DV)

    h = jnp.concatenate([x_sample.reshape(NB, D), meta_tokens.astype(F32)], axis=0)
    meta_blk = NB // N_META
    zero_ret = jnp.zeros((1, RH, DK // 2, 2 * DV), F32)
    ret_meta, ret_dec_out = [], []
    for i in range(4):
        g_mix = norm_mix[i].reshape(1, D)
        kind = i % 3
        if kind == 0:
            j = i // 3
            p = _norm_matmul(h, g_mix, ret_wp[j], tm=MS, tn=1024, out_dtype=F32, rope=(cos_s, sin_s))
            qkt = _norm_matmul_t(h, g_mix, ret_wt[j], cos_t, sin_t, rows=NB)
            s_in = ret_view(state_ret_l0 if j == 0 else state_ret_l3)
            y_dec, s_dec = _ret_dec(qkt, p, s_in, bb=2)
            y_meta, s_meta = _ret_chunk(p, zero_ret, nb=1, nC=1, L=N_META, row_block0=meta_blk, out_dtype=F32)
            ret_meta.append(s_meta)
            ret_dec_out.append(s_dec.reshape(NB, RH, DK, DV))
            h = _matmul_res(jnp.concatenate([y_dec.reshape(NB, HV), y_meta], axis=0), ret_wo[j], h, tm=MS)
        elif kind == 1:
            shp = jnp.concatenate([state_rwkv_shift_l1.astype(F32), jnp.zeros((N_META, D), F32)], axis=0)
            outs = _rwkv_proj_small(h, shp, g_mix, rwkv_wts, n_dec=NB)
            r_, lw_, k_, v_, kk_, b_, gate_, bonus_, xn_ = outs
            y_dec, wkv_dec = _wkv_dec((r_, lw_, k_, v_, kk_, b_), state_rwkv_wkv_l1.astype(F32), bb=8)
            y_meta, wkv_meta = _wkv_chunk((r_, lw_, k_, v_, kk_, b_), jnp.zeros((1, WH, WN, WN), F32),
                                          nb=1, nC=1, C=N_META, row_block0=meta_blk)
            shift_dec = xn_[:NB]
            shift_meta = xn_[MS - 1:MS]
            h = _rwkv_out(jnp.concatenate([y_dec.reshape(NB, D), y_meta], axis=0), bonus_, gate_, h, vec(rwkv_ln_g), vec(rwkv_ln_b),
                          wo_rwkv, eh, eht, tm=MS)
        else:
            p = _norm_matmul(h, g_mix, conv_wi, tm=MS, tn=1024, out_dtype=F32)
            zpad = jnp.zeros((N_META, D), F32)
            b0 = jnp.concatenate([state_conv_l2[:, 0].astype(F32), zpad], axis=0)
            b1 = jnp.concatenate([state_conv_l2[:, 1].astype(F32), zpad], axis=0)
            h, u = _conv_small(p, b0, b1, conv_wf, conv_wo, h, n_dec=NB)
            conv_dec = jnp.stack([state_conv_l2[:, 1].astype(F32), u[:NB]], axis=1)
            conv_meta = u[MS - 2:MS]
        h = _mlp(h, norm_mlp[i].reshape(1, D), w1b[i], w2b[i], tm=MS, tf=512)
    y_sample = _final_norm(h, norm_final.reshape(1, D), rows=NB, tm=NB).reshape(NB, 1, D)

    h = x_prompt.reshape(B * T, D)
    nC = T // RET_CHUNK
    ret_main = []
    for i in range(4):
        g_mix = norm_mix[i].reshape(1, D)
        kind = i % 3
        if kind == 0:
            j = i // 3
            p = _norm_matmul(h, g_mix, ret_wp[j], tm=1024, tn=1024, out_dtype=BF16, rope=(cos_m, sin_m))
            y, s_fin = _ret_chunk(p, ret_meta[j], nb=B, nC=nC, L=RET_CHUNK, row_block0=0, out_dtype=BF16)
            ret_main.append(s_fin.reshape(B, RH, DK, DV))
            h = _matmul_res(y, ret_wo[j], h, tm=512)
        elif kind == 1:
            outs = _rwkv_proj_main(h, shift_meta, g_mix, rwkv_wts, tm=256, seq=T)
            r_, lw_, k_, v_, kk_, b_, gate_, bonus_, shift_main = outs
            y, wkv_main = _wkv_chunk((r_, lw_, k_, v_, kk_, b_), wkv_meta, nb=B, nC=T // WKV_CHUNK, C=WKV_CHUNK,
                                     row_block0=0)
            h = _rwkv_out(y, bonus_, gate_, h, vec(rwkv_ln_g), vec(rwkv_ln_b), wo_rwkv, eh, eht, tm=512)
        else:
            p = _norm_matmul(h, g_mix, conv_wi, tm=1024, tn=1024, out_dtype=BF16)
            h, conv_main = _conv_main(p, conv_meta, conv_wf, conv_wo, h, tm=512, seq=T)
        h = _mlp(h, norm_mlp[i].reshape(1, D), w1b[i], w2b[i], tm=1024, tf=512)
    y_prompt = _final_norm(h, norm_final.reshape(1, D), rows=B * T, tm=1024).reshape(B, T, D)

    return (y_prompt, y_sample, ret_main[0], ret_dec_out[0], shift_main.reshape(B, D), shift_dec,
            wkv_main, wkv_dec, conv_main, conv_dec, ret_main[1], ret_dec_out[1])
```

```python
import functools
import math

import jax
import jax.numpy as jnp
from jax import lax
from jax.experimental import pallas as pl
from jax.experimental.pallas import tpu as pltpu

F32 = jnp.float32
BF16 = jnp.bfloat16

D = 1024
N_META = 16
RH = 4
DK = D // RH
DV = 2 * D // RH
HK = RH * DK
HV = RH * DV
RET_CHUNK = 128
ROPE_BASE = 10000.0
WH = 16
WN = 64
WKV_CHUNK = 64
INV_PASSES = 1
D_FF = 4 * D
EPS = 1e-6
GN_EPS = 64e-5
LANES = 128
VMEM_LIMIT_V7X = 56 * 1024 * 1024

NT = (((1,), (1,)), ((), ()))
TN = (((0,), (0,)), ((), ()))


def _cp(n_axes):
    return pltpu.CompilerParams(dimension_semantics=("arbitrary",) * n_axes,
                                vmem_limit_bytes=VMEM_LIMIT_V7X)


def _dot(a, b):
    return jnp.dot(a, b, preferred_element_type=F32)


def _rms(x, g):
    return x * lax.rsqrt(jnp.mean(x * x, axis=-1, keepdims=True) + EPS) * g


def _sigmoid(x):
    return 1.0 / (1.0 + jnp.exp(-x))


def _split(x):
    hi = x.astype(BF16)
    lo = (x - hi.astype(F32)).astype(BF16)
    return hi, lo


def _seg_sum(x, eh, eht):
    hi, lo = _split(x)
    s = _dot(hi, eh) + _dot(lo, eh)
    shi, slo = _split(s)
    return _dot(shi, eht) + _dot(slo, eht)


def _norm_matmul_kernel(x_ref, g_ref, w_ref, cos_ref, sin_ref, o_ref, xn_ref, *, tn, n_rope_tiles, k_tile_start):
    j = pl.program_id(1)

    @pl.when(j == 0)
    def _():
        xn_ref[...] = _rms(x_ref[...], g_ref[...]).astype(BF16)

    acc = _dot(xn_ref[...], w_ref[...])
    if n_rope_tiles == 0:
        o_ref[...] = acc.astype(o_ref.dtype)
        return

    @pl.when(j < n_rope_tiles)
    def _():
        c = cos_ref[...]
        s = sin_ref[...]
        scale = jnp.where(j >= k_tile_start, DK ** -0.5, 1.0).astype(F32)
        half = DK // 2
        for hh in range(tn // DK):
            x1 = acc[:, hh * DK:hh * DK + half]
            x2 = acc[:, hh * DK + half:(hh + 1) * DK]
            o_ref[:, hh * DK:hh * DK + half] = ((x1 * c - x2 * s) * scale).astype(o_ref.dtype)
            o_ref[:, hh * DK + half:(hh + 1) * DK] = ((x1 * s + x2 * c) * scale).astype(o_ref.dtype)

    @pl.when(j >= n_rope_tiles)
    def _():
        o_ref[...] = acc.astype(o_ref.dtype)


def _norm_matmul(x, g, w, *, tm, tn, out_dtype, rope=None):
    M, K = x.shape
    N = w.shape[1]
    if rope is None:
        cos = sin = jnp.zeros((8, LANES), F32)
        cs_spec = pl.BlockSpec((8, LANES), lambda i, j: (0, 0))
        n_rope_tiles = k_tile_start = 0
    else:
        cos, sin = rope
        nblk = cos.shape[0] // tm
        cs_spec = pl.BlockSpec((tm, LANES), lambda i, j: (i % nblk, 0))
        n_rope_tiles = 2 * HK // tn
        k_tile_start = HK // tn
    return pl.pallas_call(
        functools.partial(_norm_matmul_kernel, tn=tn, n_rope_tiles=n_rope_tiles, k_tile_start=k_tile_start),
        grid=(M // tm, N // tn),
        in_specs=[pl.BlockSpec((tm, K), lambda i, j: (i, 0)),
                  pl.BlockSpec((1, K), lambda i, j: (0, 0)),
                  pl.BlockSpec((K, tn), lambda i, j: (0, j)),
                  cs_spec, cs_spec],
        out_specs=pl.BlockSpec((tm, tn), lambda i, j: (i, j)),
        out_shape=jax.ShapeDtypeStruct((M, N), out_dtype),
        scratch_shapes=[pltpu.VMEM((tm, K), BF16)],
        compiler_params=_cp(2),
        name="norm_matmul",
    )(x, g, w, cos, sin)


def _norm_matmul_t_kernel(x_ref, g_ref, wt_ref, cos_ref, sin_ref, o_ref):
    xn = _rms(x_ref[...], g_ref[...]).astype(BF16)
    acc = lax.dot_general(wt_ref[...], xn, NT, preferred_element_type=F32)
    c = cos_ref[...]
    s = sin_ref[...]
    even = lax.broadcasted_iota(jnp.int32, (DK, 1), 0) % 2 == 0
    for hh in range(2 * RH):
        scale = 1.0 if hh < RH else DK ** -0.5
        x = acc[hh * DK:(hh + 1) * DK, :]
        partner = jnp.where(even, pltpu.roll(x, DK - 1, 0), pltpu.roll(x, 1, 0))
        o_ref[hh * DK:(hh + 1) * DK, :] = (x * c + partner * s) * scale


def _norm_matmul_t(x, g, wt, cos_t, sin_t, *, rows):
    K = x.shape[1]
    return pl.pallas_call(
        _norm_matmul_t_kernel,
        grid=(1,),
        in_specs=[pl.BlockSpec((rows, K), lambda i: (0, 0)),
                  pl.BlockSpec((1, K), lambda i: (0, 0)),
                  pl.BlockSpec((2 * HK, K), lambda i: (0, 0)),
                  pl.BlockSpec((DK, rows), lambda i: (0, 0)),
                  pl.BlockSpec((DK, rows), lambda i: (0, 0))],
        out_specs=pl.BlockSpec((2 * HK, rows), lambda i: (0, 0)),
        out_shape=jax.ShapeDtypeStruct((2 * HK, rows), F32),
        compiler_params=_cp(1),
        name="norm_matmul_t",
    )(x, g, wt, cos_t, sin_t)


def _mlp_kernel(h_ref, g_ref, w1_ref, w2_ref, o_ref, xn_ref, acc_ref):
    f = pl.program_id(1)

    @pl.when(f == 0)
    def _():
        xn_ref[...] = _rms(h_ref[...], g_ref[...]).astype(BF16)
        acc_ref[...] = jnp.zeros_like(acc_ref)

    a = _dot(xn_ref[...], w1_ref[...])
    a = jnp.square(jnp.maximum(a, 0.0)).astype(BF16)
    acc_ref[...] += _dot(a, w2_ref[...])

    @pl.when(f == pl.num_programs(1) - 1)
    def _():
        o_ref[...] = h_ref[...] + acc_ref[...]


def _mlp(h, g, w1, w2, *, tm, tf):
    M = h.shape[0]
    return pl.pallas_call(
        _mlp_kernel,
        grid=(M // tm, D_FF // tf),
        in_specs=[pl.BlockSpec((tm, D), lambda i, f: (i, 0)),
                  pl.BlockSpec((1, D), lambda i, f: (0, 0)),
                  pl.BlockSpec((D, tf), lambda i, f: (0, f)),
                  pl.BlockSpec((tf, D), lambda i, f: (f, 0))],
        out_specs=pl.BlockSpec((tm, D), lambda i, f: (i, 0)),
        out_shape=jax.ShapeDtypeStruct((M, D), F32),
        scratch_shapes=[pltpu.VMEM((tm, D), BF16), pltpu.VMEM((tm, D), F32)],
        compiler_params=_cp(2),
        name="mlp",
    )(h, g, w1, w2)


def _matmul_res_kernel(a_ref, w_ref, h_ref, o_ref):
    o_ref[...] = h_ref[...] + _dot(a_ref[...].astype(BF16), w_ref[...])


def _matmul_res(a, w, h, *, tm):
    M, K = a.shape
    return pl.pallas_call(
        _matmul_res_kernel,
        grid=(M // tm,),
        in_specs=[pl.BlockSpec((tm, K), lambda i: (i, 0)),
                  pl.BlockSpec((K, D), lambda i: (0, 0)),
                  pl.BlockSpec((tm, D), lambda i: (i, 0))],
        out_specs=pl.BlockSpec((tm, D), lambda i: (i, 0)),
        out_shape=jax.ShapeDtypeStruct((M, D), F32),
        compiler_params=_cp(1),
        name="matmul_res",
    )(a, w, h)


def _final_norm_kernel(h_ref, g_ref, o_ref):
    o_ref[...] = _rms(h_ref[...], g_ref[...])


def _final_norm(h, g, *, rows, tm):
    return pl.pallas_call(
        _final_norm_kernel,
        grid=(rows // tm,),
        in_specs=[pl.BlockSpec((tm, D), lambda i: (i, 0)),
                  pl.BlockSpec((1, D), lambda i: (0, 0))],
        out_specs=pl.BlockSpec((tm, D), lambda i: (i, 0)),
        out_shape=jax.ShapeDtypeStruct((rows, D), F32),
        compiler_params=_cp(1),
        name="final_norm",
    )(h, g)


def _log_gamma(h):
    return math.log(1.0 - 2.0 ** (-5.0 - h))


def _ret_chunk_kernel(p_ref, s0_ref, y_ref, so_ref, s_ref, *, L, nC, natural_out):
    c = pl.program_id(1)
    half = DK // 2

    @pl.when(c == 0)
    def _():
        for h in range(RH):
            s_ref[h] = s0_ref[0, h]

    ti = lax.broadcasted_iota(jnp.int32, (L, L), 0)
    si = lax.broadcasted_iota(jnp.int32, (L, L), 1)
    diff = (ti - si).astype(F32)
    ri = lax.broadcasted_iota(jnp.int32, (L, 1), 0).astype(F32)
    for h in range(RH):
        lg = _log_gamma(h)
        mask = jnp.where(diff >= 0, jnp.exp(jnp.maximum(diff, 0.0) * lg), 0.0)
        qd = jnp.exp((ri + 1.0) * lg)
        kd = jnp.exp((L - 1.0 - ri) * lg)
        q = p_ref[:, h * DK:(h + 1) * DK]
        k = p_ref[:, HK + h * DK:HK + (h + 1) * DK]
        v = p_ref[:, 2 * HK + h * DV:2 * HK + (h + 1) * DV].astype(BF16)
        g = p_ref[:, 2 * HK + HV + h * DV:2 * HK + HV + (h + 1) * DV].astype(F32)
        sc = lax.dot_general(q.astype(BF16), k.astype(BF16), NT, preferred_element_type=F32) * mask
        inner = _dot(sc.astype(BF16), v)
        s_old = s_ref[h]
        cross = _dot((q.astype(F32) * qd).astype(BF16), s_old.astype(BF16))
        o = inner + cross
        s_ref[h] = s_old * math.exp(L * lg) + lax.dot_general(
            (k.astype(F32) * kd).astype(BF16), v, TN, preferred_element_type=F32)
        o = o * lax.rsqrt(jnp.mean(o * o, axis=-1, keepdims=True) + EPS)
        y_ref[:, h * DV:(h + 1) * DV] = (g * _sigmoid(g) * o).astype(y_ref.dtype)

    @pl.when(c == nC - 1)
    def _():
        if not natural_out:
            so_ref[0] = s_ref[...]
            return
        d = lax.broadcasted_iota(jnp.int32, (DK, DK), 0)
        pcol = lax.broadcasted_iota(jnp.int32, (DK, DK), 1)
        perm = jnp.where(pcol == (d % 2) * half + d // 2, 1.0, 0.0).astype(BF16)
        for h in range(RH):
            x = s_ref[h]
            x0 = x.astype(BF16)
            r1 = x - x0.astype(F32)
            x1 = r1.astype(BF16)
            x2 = (r1 - x1.astype(F32)).astype(BF16)
            so_ref[0, h] = _dot(perm, x0) + _dot(perm, x1) + _dot(perm, x2)


def _ret_chunk(p, s0, *, nb, nC, L, row_block0, out_dtype, natural_out):
    shared = s0.shape[0] == 1
    return pl.pallas_call(
        functools.partial(_ret_chunk_kernel, L=L, nC=nC, natural_out=natural_out),
        grid=(nb, nC),
        in_specs=[pl.BlockSpec((L, 2 * HK + 2 * HV), lambda b, c: (row_block0 + b * nC + c, 0)),
                  pl.BlockSpec((1, RH, DK, DV), lambda b, c: (0 if shared else b, 0, 0, 0))],
        out_specs=[pl.BlockSpec((L, HV), lambda b, c: (b * nC + c, 0)),
                   pl.BlockSpec((1, RH, DK, DV), lambda b, c: (b, 0, 0, 0))],
        out_shape=[jax.ShapeDtypeStruct((nb * nC * L, HV), out_dtype),
                   jax.ShapeDtypeStruct((nb, RH, DK, DV), F32)],
        scratch_shapes=[pltpu.VMEM((RH, DK, DV), F32)],
        compiler_params=_cp(2),
        name="ret_chunk",
    )(p, s0)


def _ret_dec_kernel(qkt_ref, p_ref, s_ref, y_ref, so_ref, *, bb, nbatch):
    step = pl.program_id(0)
    lane = lax.broadcasted_iota(jnp.int32, (1, nbatch), 1)

    def body(jb, carry):
        b = step * bb + jb
        onehot = (lane == b).astype(F32)
        rows = []
        for h in range(RH):
            gam = math.exp(_log_gamma(h))
            vrow = p_ref[pl.ds(b, 1), 2 * HK + h * DV:2 * HK + (h + 1) * DV]
            qcol = jnp.sum(qkt_ref[h * DK:(h + 1) * DK, :] * onehot, axis=1, keepdims=True)
            kcol = jnp.sum(qkt_ref[HK + h * DK:HK + (h + 1) * DK, :] * onehot, axis=1, keepdims=True)
            sn = gam * s_ref[jb, h] + kcol * vrow
            so_ref[jb, h] = sn
            o = jnp.sum(qcol * sn, axis=0, keepdims=True)
            o = o * lax.rsqrt(jnp.mean(o * o, axis=-1, keepdims=True) + EPS)
            g = p_ref[pl.ds(b, 1), 2 * HK + HV + h * DV:2 * HK + HV + (h + 1) * DV]
            rows.append(g * _sigmoid(g) * o)
        y_ref[b] = jnp.concatenate(rows, axis=1)
        return carry

    lax.fori_loop(0, bb, body, 0)


def _ret_dec(qkt, p, s, *, bb):
    nbatch = s.shape[0]
    return pl.pallas_call(
        functools.partial(_ret_dec_kernel, bb=bb, nbatch=nbatch),
        grid=(nbatch // bb,),
        in_specs=[pl.BlockSpec(qkt.shape, lambda i: (0, 0)),
                  pl.BlockSpec(p.shape, lambda i: (0, 0)),
                  pl.BlockSpec((bb, RH, DK, DV), lambda i: (i, 0, 0, 0))],
        out_specs=[pl.BlockSpec((nbatch, 1, HV), lambda i: (0, 0, 0)),
                   pl.BlockSpec((bb, RH, DK, DV), lambda i: (i, 0, 0, 0))],
        out_shape=[jax.ShapeDtypeStruct((nbatch, 1, HV), F32),
                   jax.ShapeDtypeStruct(s.shape, F32)],
        compiler_params=_cp(1),
        name="ret_dec",
    )(qkt, p, s)


N_RWKV_OUT = 8


def _rwkv_core(xn, xprev, w_refs, out_refs):
    (mix_ref, wrkv_ref, w0_ref, w1_ref, w2_ref, a0_ref, a1_ref, a2_ref, g1_ref, g2_ref,
     kk_ref, ka_ref, rk_ref, eh_ref, eht_ref) = w_refs
    r_o, lw_o, k_o, v_o, kk_o, b_o, g_o, bonus_o = out_refs
    xx = xprev - xn

    def xm(j):
        return (xn + xx * mix_ref[j:j + 1, :]).astype(BF16)

    r = _dot(xm(0), wrkv_ref[0])
    k = _dot(xm(1), wrkv_ref[1])
    v = _dot(xm(2), wrkv_ref[2])
    wl = _dot(jnp.tanh(_dot(xm(3), w1_ref[...])).astype(BF16), w2_ref[...])
    nx = -(w0_ref[...] + wl)
    softplus = jnp.maximum(nx, 0.0) + jnp.log(1.0 + jnp.exp(-jnp.abs(nx)))
    w = -softplus - 0.5
    lw_o[...] = -jnp.exp(w)
    al = _dot(_dot(xm(4), a1_ref[...]).astype(BF16), a2_ref[...])
    a = _sigmoid(a0_ref[...] + al)
    g_o[...] = _dot(_sigmoid(_dot(xm(5), g1_ref[...])).astype(BF16), g2_ref[...]).astype(g_o.dtype)
    kk = k * kk_ref[...]
    ssq = _seg_sum(kk * kk, eh_ref[...], eht_ref[...])
    kkn = kk / jnp.maximum(jnp.sqrt(ssq), 1e-12)
    k2 = k * (1.0 + (a - 1.0) * ka_ref[...])
    rk = _seg_sum(r * k2 * rk_ref[...], eh_ref[...], eht_ref[...])
    r_o[...] = r.astype(r_o.dtype)
    k_o[...] = k2.astype(k_o.dtype)
    v_o[...] = v.astype(v_o.dtype)
    kk_o[...] = kkn.astype(kk_o.dtype)
    b_o[...] = (kkn * a).astype(b_o.dtype)
    bonus_o[...] = (rk * v).astype(bonus_o.dtype)


def _rwkv_proj_main_kernel(h_ref, hp_ref, sh0_ref, g_ref, *refs, tm, tps):
    w_refs = refs[:15]
    out_refs = refs[15:15 + N_RWKV_OUT]
    sho_ref = refs[15 + N_RWKV_OUT]
    i = pl.program_id(0)
    xn = _rms(h_ref[...], g_ref[...])
    prevn = _rms(hp_ref[...], g_ref[...])[7:8, :]
    prev = jnp.where(i % tps == 0, sh0_ref[...], prevn)
    row = lax.broadcasted_iota(jnp.int32, (tm, 1), 0)
    xprev = jnp.where(row == 0, prev, pltpu.roll(xn, 1, 0))
    _rwkv_core(xn, xprev, w_refs, out_refs)
    sho_ref[0] = xn[tm - 1:tm, :]


def _rwkv_proj_small_kernel(h_ref, shp_ref, g_ref, *refs, n_dec):
    w_refs = refs[:15]
    out_refs = refs[15:15 + N_RWKV_OUT]
    xn_ref = refs[15 + N_RWKV_OUT]
    m = h_ref.shape[0]
    xn = _rms(h_ref[...], g_ref[...])
    row = lax.broadcasted_iota(jnp.int32, (m, 1), 0)
    xprev = jnp.where(row <= n_dec, shp_ref[...], pltpu.roll(xn, 1, 0))
    _rwkv_core(xn, xprev, w_refs, out_refs)
    xn_ref[...] = xn


def _rwkv_weight_specs(wts, nidx):
    zero = (lambda *idx: (0, 0))
    zero3 = (lambda *idx: (0, 0, 0))
    specs = []
    for w in wts:
        specs.append(pl.BlockSpec(w.shape, zero3 if w.ndim == 3 else zero))
    return specs


def _rwkv_proj_main(h, sh0, g, wts, *, tm, seq):
    M = h.shape[0]
    tps = seq // tm
    nseq = M // seq
    odt = [BF16, F32, BF16, BF16, BF16, BF16, BF16, BF16]
    row_spec = pl.BlockSpec((tm, D), lambda i: (i, 0))
    return pl.pallas_call(
        functools.partial(_rwkv_proj_main_kernel, tm=tm, tps=tps),
        grid=(M // tm,),
        in_specs=[row_spec,
                  pl.BlockSpec((8, D), lambda i: (jnp.maximum(i * (tm // 8) - 1, 0), 0)),
                  pl.BlockSpec((1, D), lambda i: (0, 0)),
                  pl.BlockSpec((1, D), lambda i: (0, 0))] + _rwkv_weight_specs(wts, 1),
        out_specs=[row_spec] * N_RWKV_OUT + [pl.BlockSpec((1, 1, D), lambda i: (i // tps, 0, 0))],
        out_shape=[jax.ShapeDtypeStruct((M, D), dt) for dt in odt] + [jax.ShapeDtypeStruct((nseq, 1, D), F32)],
        compiler_params=_cp(1),
        name="rwkv_proj_main",
    )(h, h, sh0, g, *wts)


def _rwkv_proj_small(h, shp, g, wts, *, n_dec):
    M = h.shape[0]
    full = pl.BlockSpec((M, D), lambda i: (0, 0))
    return pl.pallas_call(
        functools.partial(_rwkv_proj_small_kernel, n_dec=n_dec),
        grid=(1,),
        in_specs=[full, full, pl.BlockSpec((1, D), lambda i: (0, 0))] + _rwkv_weight_specs(wts, 1),
        out_specs=[full] * (N_RWKV_OUT + 1),
        out_shape=[jax.ShapeDtypeStruct((M, D), F32)] * (N_RWKV_OUT + 1),
        compiler_params=_cp(1),
        name="rwkv_proj_small",
    )(h, shp, g, *wts)


def _mm(a, b, passes):
    if passes == 1:
        return _dot(a.astype(BF16), b.astype(BF16))
    ah, al = _split(a)
    bh, bl = _split(b)
    return _dot(ah, bh) + _dot(ah, bl) + _dot(al, bh)


def _wkv_chunk_kernel(r_ref, lw_ref, k_ref, v_ref, kk_ref, b_ref, s0_ref, y_ref, so_ref, s_ref, *, C, nC):
    c = pl.program_id(1)

    @pl.when(c == 0)
    def _():
        s_ref[...] = s0_ref[0]

    ti = lax.broadcasted_iota(jnp.int32, (C, C), 0)
    si = lax.broadcasted_iota(jnp.int32, (C, C), 1)
    incl = si <= ti
    strict = si < ti
    tri = jnp.where(incl, 1.0, 0.0).astype(BF16)
    eye = jnp.where(ti == si, 1.0, 0.0).astype(F32)

    lw = lw_ref[...].astype(F32)
    l0 = lw.astype(BF16)
    rem = lw - l0.astype(F32)
    l1 = rem.astype(BF16)
    l2 = (rem - l1.astype(F32)).astype(BF16)
    cum = _dot(tri, l0) + _dot(tri, l1) + _dot(tri, l2)
    tot = cum[C - 1:C, :]
    w_in = jnp.exp(-cum)
    w_end = jnp.exp(tot - cum)
    kf = k_ref[...].astype(F32)
    bf = b_ref[...].astype(F32)
    rt = (r_ref[...].astype(F32) * jnp.exp(cum)).astype(BF16)
    at = (-kk_ref[...].astype(F32) * jnp.exp(cum - lw)).astype(BF16)
    kt = (kf * w_in).astype(BF16)
    bt = (bf * w_in).astype(BF16)
    ke = (kf * w_end).astype(BF16)
    be = (bf * w_end).astype(BF16)
    wc = jnp.exp(tot)
    vb = v_ref[...].astype(BF16)

    heads = range(WH)
    sls = [slice(h * WN, (h + 1) * WN) for h in heads]
    lhs = [jnp.concatenate([at[:, sl], rt[:, sl]], axis=0) for sl in sls]
    rhs = [jnp.concatenate([kt[:, sl], bt[:, sl]], axis=0) for sl in sls]
    gm = [lax.dot_general(lhs[h], rhs[h], NT, preferred_element_type=F32) for h in heads]
    s_old = [s_ref[h] for h in heads]
    lh = [lax.dot_general(lhs[h], s_old[h].astype(BF16), NT, preferred_element_type=F32) for h in heads]
    vh = [vb[:, sl] for sl in sls]
    a_ak = [jnp.where(strict, g[0:C, 0:C], 0.0).astype(BF16) for g in gm]
    p = [jnp.where(strict, g[0:C, C:2 * C], 0.0) for g in gm]
    t2 = lax.broadcasted_iota(jnp.int32, (C, 2 * C), 0)
    s2 = lax.broadcasted_iota(jnp.int32, (C, 2 * C), 1)
    incl2 = jnp.where(s2 >= C, s2 - C, s2) <= t2
    a_r = [jnp.where(incl2, g[C:2 * C, :], 0.0).astype(BF16) for g in gm]
    rhs0 = [lh[h][0:C] + _dot(a_ak[h], vh[h]) for h in heads]
    t = [eye + p[h] for h in heads]
    for _ in range(int(math.log2(C)) - 1):
        p = [_mm(p[h], p[h], INV_PASSES) for h in heads]
        t = [t[h] + _mm(t[h], p[h], INV_PASSES) for h in heads]
    u = [_mm(t[h], rhs0[h], INV_PASSES) for h in heads]
    vu = [jnp.concatenate([vh[h], u[h].astype(BF16)], axis=0) for h in heads]
    y = [lh[h][C:2 * C] + _dot(a_r[h], vu[h]) for h in heads]
    y_ref[...] = jnp.concatenate(y, axis=1)
    for h in heads:
        sl = sls[h]
        s_ref[h] = s_old[h] * wc[:, sl] + lax.dot_general(
            vu[h], jnp.concatenate([ke[:, sl], be[:, sl]], axis=0), TN, preferred_element_type=F32)

    @pl.when(c == nC - 1)
    def _():
        so_ref[0] = s_ref[...]


def _wkv_chunk(ins, s0, *, nb, nC, C, row_block0):
    shared = s0.shape[0] == 1
    row_spec = pl.BlockSpec((C, D), lambda b, c: (row_block0 + b * nC + c, 0))
    return pl.pallas_call(
        functools.partial(_wkv_chunk_kernel, C=C, nC=nC),
        grid=(nb, nC),
        in_specs=[row_spec] * 6 + [pl.BlockSpec((1, WH, WN, WN), lambda b, c: (0 if shared else b, 0, 0, 0))],
        out_specs=[pl.BlockSpec((C, D), lambda b, c: (b * nC + c, 0)),
                   pl.BlockSpec((1, WH, WN, WN), lambda b, c: (b, 0, 0, 0))],
        out_shape=[jax.ShapeDtypeStruct((nb * nC * C, D), F32),
                   jax.ShapeDtypeStruct((nb, WH, WN, WN), F32)],
        scratch_shapes=[pltpu.VMEM((WH, WN, WN), F32)],
        compiler_params=_cp(2),
        name="wkv_chunk",
    )(*ins, s0)


def _wkv_dec_kernel(r_ref, lw_ref, k_ref, v_ref, kk_ref, b_ref, s_ref, yt_ref, so_ref, *, bb, nbatch):
    step = pl.program_id(0)
    rows_all = WH * WN
    ones = jnp.ones((WN, LANES), BF16)
    eye_t = jnp.where(lax.broadcasted_iota(jnp.int32, (rows_all, WN), 0) % WN
                      == lax.broadcasted_iota(jnp.int32, (rows_all, WN), 1), 1.0, 0.0).astype(F32)
    lane = lax.broadcasted_iota(jnp.int32, (1, nbatch), 1)

    @pl.when(step == 0)
    def _():
        yt_ref[...] = jnp.zeros_like(yt_ref)

    def rowsum(x):
        hi, lo = _split(x)
        return _dot(hi, ones) + _dot(lo, ones)

    def per_head_rows(row):
        return jnp.concatenate([jnp.broadcast_to(row[:, h * WN:(h + 1) * WN], (WN, WN)) for h in range(WH)], axis=0)

    def body(jb, carry):
        b = step * bb + jb
        s_old = s_ref[jb].reshape(rows_all, WN)
        rm = per_head_rows(r_ref[pl.ds(b, 1), :])
        wm = per_head_rows(jnp.exp(lw_ref[pl.ds(b, 1), :]))
        km = per_head_rows(k_ref[pl.ds(b, 1), :])
        vm = per_head_rows(v_ref[pl.ds(b, 1), :])
        kkm = per_head_rows(kk_ref[pl.ds(b, 1), :])
        bm = per_head_rows(b_ref[pl.ds(b, 1), :])
        sa = rowsum(s_old * (-kkm))[:, 0:WN]
        vcol = rowsum(eye_t * vm)[:, 0:WN]
        sn = s_old * wm + sa * bm + vcol * km
        so_ref[jb] = sn.reshape(WH, WN, WN)
        yt_ref[...] = jnp.where(lane == b, rowsum(sn * rm), yt_ref[...])
        return carry

    lax.fori_loop(0, bb, body, 0)


def _wkv_dec(ins, s, *, bb):
    nbatch = s.shape[0]
    full = pl.BlockSpec(ins[0].shape, lambda i: (0, 0))
    st = pl.BlockSpec((bb, WH, WN, WN), lambda i: (i, 0, 0, 0))
    return pl.pallas_call(
        functools.partial(_wkv_dec_kernel, bb=bb, nbatch=nbatch),
        grid=(nbatch // bb,),
        in_specs=[full] * 6 + [st],
        out_specs=[pl.BlockSpec((D, nbatch), lambda i: (0, 0)), st],
        out_shape=[jax.ShapeDtypeStruct((D, nbatch), F32), jax.ShapeDtypeStruct(s.shape, F32)],
        compiler_params=_cp(1),
        name="wkv_dec",
    )(*ins, s)


def _rwkv_out_kernel(y_ref, bonus_ref, g_ref, h_ref, lng_ref, lnb_ref, wo_ref, eh_ref, eht_ref, o_ref):
    y = y_ref[...]
    eh = eh_ref[...]
    eht = eht_ref[...]
    mu = _seg_sum(y, eh, eht) * (1.0 / WN)
    d = y - mu
    var = _seg_sum(d * d, eh, eht) * (1.0 / WN)
    yn = d * lax.rsqrt(var + GN_EPS) * lng_ref[...] + lnb_ref[...]
    z = (yn + bonus_ref[...].astype(F32)) * g_ref[...].astype(F32)
    o_ref[...] = h_ref[...] + _dot(z.astype(BF16), wo_ref[...])


def _rwkv_out(y, bonus, g, h, lng, lnb, wo, eh, eht, *, tm):
    M = h.shape[0]
    row = pl.BlockSpec((tm, D), lambda i: (i, 0))
    vec = pl.BlockSpec((1, D), lambda i: (0, 0))
    return pl.pallas_call(
        _rwkv_out_kernel,
        grid=(M // tm,),
        in_specs=[row, row, row, row, vec, vec,
                  pl.BlockSpec((D, D), lambda i: (0, 0)),
                  pl.BlockSpec(eh.shape, lambda i: (0, 0)),
                  pl.BlockSpec(eht.shape, lambda i: (0, 0))],
        out_specs=row,
        out_shape=jax.ShapeDtypeStruct((M, D), F32),
        compiler_params=_cp(1),
        name="rwkv_out",
    )(y, bonus, g, h, lng, lnb, wo, eh, eht)


def _conv_main_kernel(p_ref, pp_ref, buf_ref, cw_ref, wo_ref, h_ref, o_ref, nb_ref, *, tm, tps):
    i = pl.program_id(0)
    bq = p_ref[:, 0:D].astype(F32)
    u = p_ref[:, D:2 * D].astype(F32) * p_ref[:, 2 * D:3 * D].astype(F32)
    up = pp_ref[:, D:2 * D].astype(F32) * pp_ref[:, 2 * D:3 * D].astype(F32)
    npr = pp_ref.shape[0]
    first = i % tps == 0
    prev1 = jnp.where(first, buf_ref[1:2, :], up[npr - 1:npr, :])
    prev2 = jnp.where(first, buf_ref[0:1, :], up[npr - 2:npr - 1, :])
    row = lax.broadcasted_iota(jnp.int32, (tm, 1), 0)
    m1 = jnp.where(row == 0, prev1, pltpu.roll(u, 1, 0))
    m2 = jnp.where(row == 0, prev2, jnp.where(row == 1, prev1, pltpu.roll(u, 2, 0)))
    y = cw_ref[0:1, :] * m2 + cw_ref[1:2, :] * m1 + cw_ref[2:3, :] * u
    o_ref[...] = h_ref[...] + _dot((bq * y).astype(BF16), wo_ref[...])
    nb_ref[0] = u[tm - 2:tm, :]


def _conv_main(p, buf, cw, wo, h, *, tm, seq):
    M = h.shape[0]
    tps = seq // tm
    npr = 16
    return pl.pallas_call(
        functools.partial(_conv_main_kernel, tm=tm, tps=tps),
        grid=(M // tm,),
        in_specs=[pl.BlockSpec((tm, 3 * D), lambda i: (i, 0)),
                  pl.BlockSpec((npr, 3 * D), lambda i: (jnp.maximum(i * (tm // npr) - 1, 0), 0)),
                  pl.BlockSpec((2, D), lambda i: (0, 0)),
                  pl.BlockSpec((3, D), lambda i: (0, 0)),
                  pl.BlockSpec((D, D), lambda i: (0, 0)),
                  pl.BlockSpec((tm, D), lambda i: (i, 0))],
        out_specs=[pl.BlockSpec((tm, D), lambda i: (i, 0)),
                   pl.BlockSpec((1, 2, D), lambda i: (i // tps, 0, 0))],
        out_shape=[jax.ShapeDtypeStruct((M, D), F32), jax.ShapeDtypeStruct((M // seq, 2, D), F32)],
        compiler_params=_cp(1),
        name="conv_main",
    )(p, p, buf, cw, wo, h)


def _conv_small_kernel(p_ref, b0_ref, b1_ref, cw_ref, wo_ref, h_ref, o_ref, u_ref, *, n_dec):
    m = h_ref.shape[0]
    bq = p_ref[:, 0:D]
    u = p_ref[:, D:2 * D] * p_ref[:, 2 * D:3 * D]
    row = lax.broadcasted_iota(jnp.int32, (m, 1), 0)
    m1 = jnp.where(row <= n_dec, b1_ref[...], pltpu.roll(u, 1, 0))
    m2 = jnp.where(row <= n_dec + 1, b0_ref[...], pltpu.roll(u, 2, 0))
    y = cw_ref[0:1, :] * m2 + cw_ref[1:2, :] * m1 + cw_ref[2:3, :] * u
    o_ref[...] = h_ref[...] + _dot((bq * y).astype(BF16), wo_ref[...])
    u_ref[...] = u


def _conv_small(p, b0, b1, cw, wo, h, *, n_dec):
    M = h.shape[0]
    full = pl.BlockSpec((M, D), lambda i: (0, 0))
    return pl.pallas_call(
        functools.partial(_conv_small_kernel, n_dec=n_dec),
        grid=(1,),
        in_specs=[pl.BlockSpec((M, 3 * D), lambda i: (0, 0)), full, full,
                  pl.BlockSpec((3, D), lambda i: (0, 0)),
                  pl.BlockSpec((D, D), lambda i: (0, 0)), full],
        out_specs=[full, full],
        out_shape=[jax.ShapeDtypeStruct((M, D), F32)] * 2,
        compiler_params=_cp(1),
        name="conv_small",
    )(p, b0, b1, cw, wo, h)


def _deinterleave_cols(w):
    K = w.shape[0]
    return w.reshape(K, RH, DK // 2, 2).transpose(0, 1, 3, 2).reshape(K, RH * DK)


def _rope_tables(pos):
    inv = 1.0 / (ROPE_BASE ** jnp.linspace(0.0, 1.0, DK // 2, dtype=F32))
    ang = pos.astype(F32)[:, None] * inv[None, :]
    return jnp.cos(ang), jnp.sin(ang)


def kernel(x_prompt, x_sample, state_ret_l0, state_rwkv_shift_l1, state_rwkv_wkv_l1, state_conv_l2, state_ret_l3,
           meta_tokens, norm_mix, norm_mlp, norm_final, ret_w_in, ret_w_out, rwkv_mix, rwkv_w_rkv, rwkv_w0, rwkv_w1,
           rwkv_w2, rwkv_a0, rwkv_a1, rwkv_a2, rwkv_g1, rwkv_g2, rwkv_k_k, rwkv_k_a, rwkv_r_k, rwkv_ln_g, rwkv_ln_b,
           rwkv_w_o, conv_w_in, conv_w, conv_w_out, mlp_w1, mlp_w2):
    B, T, _ = x_prompt.shape
    NB = x_sample.shape[0]
    MS = NB + N_META
    past_len = 16384

    def ret_w(j):
        w = ret_w_in[j]
        return jnp.concatenate([_deinterleave_cols(w[:, :HK]), _deinterleave_cols(w[:, HK:2 * HK]), w[:, 2 * HK:]],
                               axis=1).astype(BF16)

    ret_wp = [ret_w(0), ret_w(1)]
    ret_wt = [ret_w_in[j][:, :2 * HK].T.astype(BF16) for j in range(2)]
    ret_wo = [ret_w_out[0].astype(BF16), ret_w_out[1].astype(BF16)]
    w1b = mlp_w1.astype(BF16)
    w2b = mlp_w2.astype(BF16)
    head_of_lane = jnp.arange(D) // WN
    eh = (head_of_lane[:, None] == jnp.arange(LANES)[None, :]).astype(BF16)
    eht = eh.T
    vec = lambda a: a.reshape(1, D).astype(F32)
    rwkv_wts = (rwkv_mix.astype(F32), rwkv_w_rkv.astype(BF16), vec(rwkv_w0), rwkv_w1.astype(BF16), rwkv_w2.astype(BF16),
                vec(rwkv_a0), rwkv_a1.astype(BF16), rwkv_a2.astype(BF16), rwkv_g1.astype(BF16), rwkv_g2.astype(BF16),
                vec(rwkv_k_k), vec(rwkv_k_a), vec(rwkv_r_k), eh, eht)
    wo_rwkv = rwkv_w_o.astype(BF16)
    conv_wi = conv_w_in.astype(BF16)
    conv_wo = conv_w_out.astype(BF16)
    conv_wf = conv_w.astype(F32)

    cos_m, sin_m = _rope_tables(N_META + jnp.arange(T))
    pos_s = jnp.concatenate([jnp.full((NB,), past_len, jnp.int32), jnp.arange(N_META, dtype=jnp.int32)])
    cos_s, sin_s = _rope_tables(pos_s)
    cos_t = jnp.repeat(cos_s[:NB].T, 2, axis=0)
    sin_t = jnp.repeat(sin_s[:NB].T, 2, axis=0) * jnp.where(jnp.arange(DK) % 2 == 0, -1.0, 1.0)[:, None]

    h = jnp.concatenate([x_sample.reshape(NB, D), meta_tokens.astype(F32)], axis=0)
    meta_blk = NB // N_META
    zero_ret = jnp.zeros((1, RH, DK, DV), F32)
    ret_meta, ret_dec_out = [], []
    for i in range(4):
        g_mix = norm_mix[i].reshape(1, D)
        kind = i % 3
        if kind == 0:
            j = i // 3
            p = _norm_matmul(h, g_mix, ret_wp[j], tm=MS, tn=1024, out_dtype=F32, rope=(cos_s, sin_s))
            qkt = _norm_matmul_t(h, g_mix, ret_wt[j], cos_t, sin_t, rows=NB)
            s_in = (state_ret_l0 if j == 0 else state_ret_l3).astype(F32)
            y_dec, s_dec = _ret_dec(qkt, p, s_in, bb=2)
            y_meta, s_meta = _ret_chunk(p, zero_ret, nb=1, nC=1, L=N_META, row_block0=meta_blk, out_dtype=F32,
                                        natural_out=False)
            ret_meta.append(s_meta)
            ret_dec_out.append(s_dec)
            h = _matmul_res(jnp.concatenate([y_dec.reshape(NB, HV), y_meta], axis=0), ret_wo[j], h, tm=MS)
        elif kind == 1:
            shp = jnp.concatenate([state_rwkv_shift_l1.astype(F32), jnp.zeros((N_META, D), F32)], axis=0)
            outs = _rwkv_proj_small(h, shp, g_mix, rwkv_wts, n_dec=NB)
            r_, lw_, k_, v_, kk_, b_, gate_, bonus_, xn_ = outs
            y_dec, wkv_dec = _wkv_dec((r_, lw_, k_, v_, kk_, b_), state_rwkv_wkv_l1.astype(F32), bb=8)
            y_meta, wkv_meta = _wkv_chunk((r_, lw_, k_, v_, kk_, b_), jnp.zeros((1, WH, WN, WN), F32),
                                          nb=1, nC=1, C=N_META, row_block0=meta_blk)
            shift_dec = xn_[:NB]
            shift_meta = xn_[MS - 1:MS]
            h = _rwkv_out(jnp.concatenate([y_dec.T, y_meta], axis=0), bonus_, gate_, h, vec(rwkv_ln_g), vec(rwkv_ln_b),
                          wo_rwkv, eh, eht, tm=MS)
        else:
            p = _norm_matmul(h, g_mix, conv_wi, tm=MS, tn=1024, out_dtype=F32)
            zpad = jnp.zeros((N_META, D), F32)
            b0 = jnp.concatenate([state_conv_l2[:, 0].astype(F32), zpad], axis=0)
            b1 = jnp.concatenate([state_conv_l2[:, 1].astype(F32), zpad], axis=0)
            h, u = _conv_small(p, b0, b1, conv_wf, conv_wo, h, n_dec=NB)
            conv_dec = jnp.stack([state_conv_l2[:, 1].astype(F32), u[:NB]], axis=1)
            conv_meta = u[MS - 2:MS]
        h = _mlp(h, norm_mlp[i].reshape(1, D), w1b[i], w2b[i], tm=MS, tf=512)
    y_sample = _final_norm(h, norm_final.reshape(1, D), rows=NB, tm=NB).reshape(NB, 1, D)

    h = x_prompt.reshape(B * T, D)
    nC = T // RET_CHUNK
    ret_main = []
    for i in range(4):
        g_mix = norm_mix[i].reshape(1, D)
        kind = i % 3
        if kind == 0:
            j = i // 3
            p = _norm_matmul(h, g_mix, ret_wp[j], tm=1024, tn=1024, out_dtype=BF16, rope=(cos_m, sin_m))
            y, s_fin = _ret_chunk(p, ret_meta[j], nb=B, nC=nC, L=RET_CHUNK, row_block0=0, out_dtype=BF16,
                                  natural_out=True)
            ret_main.append(s_fin)
            h = _matmul_res(y, ret_wo[j], h, tm=512)
        elif kind == 1:
            outs = _rwkv_proj_main(h, shift_meta, g_mix, rwkv_wts, tm=256, seq=T)
            r_, lw_, k_, v_, kk_, b_, gate_, bonus_, shift_main = outs
            y, wkv_main = _wkv_chunk((r_, lw_, k_, v_, kk_, b_), wkv_meta, nb=B, nC=T // WKV_CHUNK, C=WKV_CHUNK,
                                     row_block0=0)
            h = _rwkv_out(y, bonus_, gate_, h, vec(rwkv_ln_g), vec(rwkv_ln_b), wo_rwkv, eh, eht, tm=512)
        else:
            p = _norm_matmul(h, g_mix, conv_wi, tm=1024, tn=1024, out_dtype=BF16)
            h, conv_main = _conv_main(p, conv_meta, conv_wf, conv_wo, h, tm=512, seq=T)
        h = _mlp(h, norm_mlp[i].reshape(1, D), w1b[i], w2b[i], tm=1024, tf=512)
    y_prompt = _final_norm(h, norm_final.reshape(1, D), rows=B * T, tm=1024).reshape(B, T, D)

    return (y_prompt, y_sample, ret_main[0], ret_dec_out[0], shift_main.reshape(B, D), shift_dec,
            wkv_main, wkv_dec, conv_main, conv_dec, ret_main[1], ret_dec_out[1])
```

```python
import functools
import math

import jax
import jax.numpy as jnp
from jax import lax
from jax.experimental import pallas as pl
from jax.experimental.pallas import tpu as pltpu

F32 = jnp.float32
BF16 = jnp.bfloat16

D = 1024
N_META = 16
RH = 4
DK = D // RH
DV = 2 * D // RH
HK = RH * DK
HV = RH * DV
RET_CHUNK = 128
ROPE_BASE = 10000.0
WH = 16
WN = 64
WKV_CHUNK = 64
INV_HI_LEVELS = 4
WKV_GROUP = 2
D_FF = 4 * D
EPS = 1e-6
GN_EPS = 64e-5
LANES = 128
VMEM_LIMIT_V7X = 56 * 1024 * 1024

NT = (((1,), (1,)), ((), ()))
TN = (((0,), (0,)), ((), ()))


def _cp(n_axes):
    return pltpu.CompilerParams(dimension_semantics=("arbitrary",) * n_axes,
                                vmem_limit_bytes=VMEM_LIMIT_V7X)


def _dot(a, b):
    return jnp.dot(a, b, preferred_element_type=F32)


def _rms(x, g):
    return x * lax.rsqrt(jnp.mean(x * x, axis=-1, keepdims=True) + EPS) * g


def _sigmoid(x):
    return 1.0 / (1.0 + jnp.exp(-x))


def _split(x):
    hi = x.astype(BF16)
    lo = (x - hi.astype(F32)).astype(BF16)
    return hi, lo


def _seg_sum(x, eh, eht):
    return _dot(_dot(x.astype(BF16), eh).astype(BF16), eht)


def _resident(shape):
    return pl.BlockSpec(shape, lambda *idx: (0,) * len(shape), pipeline_mode=pl.Buffered(1))


def _norm_matmul_kernel(x_ref, g_ref, cos_ref, sin_ref, *refs, tn, rope):
    w_refs, o_ref = refs[:-1], refs[-1]
    xn = _rms(x_ref[...], g_ref[...]).astype(BF16)
    half = DK // 2
    col = 0
    for gi, w_ref in enumerate(w_refs):
        for j in range(w_ref.shape[1] // tn):
            acc = _dot(xn, w_ref[:, j * tn:(j + 1) * tn])
            if not (rope and gi == 0):
                o_ref[:, col:col + tn] = acc.astype(o_ref.dtype)
            else:
                c = cos_ref[...]
                s = sin_ref[...]
                scale = DK ** -0.5 if col >= HK else 1.0
                for hh in range(tn // DK):
                    x1 = acc[:, hh * DK:hh * DK + half]
                    x2 = acc[:, hh * DK + half:(hh + 1) * DK]
                    lo = col + hh * DK
                    o_ref[:, lo:lo + half] = ((x1 * c - x2 * s) * scale).astype(o_ref.dtype)
                    o_ref[:, lo + half:lo + DK] = ((x1 * s + x2 * c) * scale).astype(o_ref.dtype)
            col += tn


def _norm_matmul(x, g, ws, *, tm, tn, out_dtype, rope=None):
    M, K = x.shape
    N = sum(w.shape[1] for w in ws)
    if rope is None:
        cos = sin = jnp.zeros((8, LANES), F32)
        cs_spec = pl.BlockSpec((8, LANES), lambda i: (0, 0))
    else:
        cos, sin = rope
        nblk = cos.shape[0] // tm
        cs_spec = pl.BlockSpec((tm, LANES), lambda i: (i % nblk, 0))
    return pl.pallas_call(
        functools.partial(_norm_matmul_kernel, tn=tn, rope=rope is not None),
        grid=(M // tm,),
        in_specs=[pl.BlockSpec((tm, K), lambda i: (i, 0)), _resident((1, K)), cs_spec, cs_spec]
        + [_resident(w.shape) for w in ws],
        out_specs=pl.BlockSpec((tm, N), lambda i: (i, 0)),
        out_shape=jax.ShapeDtypeStruct((M, N), out_dtype),
        compiler_params=_cp(1),
        name="norm_matmul",
    )(x, g, cos, sin, *ws)


def _norm_matmul_t_kernel(x_ref, g_ref, wt_ref, cos_ref, sin_ref, o_ref):
    xn = _rms(x_ref[...], g_ref[...]).astype(BF16)
    acc = lax.dot_general(wt_ref[...], xn, NT, preferred_element_type=F32)
    c = cos_ref[...]
    s = sin_ref[...]
    even = lax.broadcasted_iota(jnp.int32, (DK, 1), 0) % 2 == 0
    for hh in range(2 * RH):
        scale = 1.0 if hh < RH else DK ** -0.5
        x = acc[hh * DK:(hh + 1) * DK, :]
        partner = jnp.where(even, pltpu.roll(x, DK - 1, 0), pltpu.roll(x, 1, 0))
        o_ref[hh * DK:(hh + 1) * DK, :] = (x * c + partner * s) * scale


def _norm_matmul_t(x, g, wt, cos_t, sin_t, *, rows):
    K = x.shape[1]
    return pl.pallas_call(
        _norm_matmul_t_kernel,
        grid=(1,),
        in_specs=[pl.BlockSpec((rows, K), lambda i: (0, 0)),
                  pl.BlockSpec((1, K), lambda i: (0, 0)),
                  pl.BlockSpec((2 * HK, K), lambda i: (0, 0)),
                  pl.BlockSpec((DK, rows), lambda i: (0, 0)),
                  pl.BlockSpec((DK, rows), lambda i: (0, 0))],
        out_specs=pl.BlockSpec((2 * HK, rows), lambda i: (0, 0)),
        out_shape=jax.ShapeDtypeStruct((2 * HK, rows), F32),
        compiler_params=_cp(1),
        name="norm_matmul_t",
    )(x, g, wt, cos_t, sin_t)


def _mlp_kernel(h_ref, g_ref, w1_ref, w2_ref, o_ref, *, tf):
    x = h_ref[...]
    xn = _rms(x, g_ref[...]).astype(BF16)
    acc = None
    for f in range(D_FF // tf):
        a = _dot(xn, w1_ref[:, f * tf:(f + 1) * tf])
        a = jnp.square(jnp.maximum(a, 0.0)).astype(BF16)
        part = _dot(a, w2_ref[f * tf:(f + 1) * tf, :])
        acc = part if acc is None else acc + part
    o_ref[...] = x + acc


def _mlp(h, g, w1, w2, *, tm, tf):
    M = h.shape[0]
    return pl.pallas_call(
        functools.partial(_mlp_kernel, tf=tf),
        grid=(M // tm,),
        in_specs=[pl.BlockSpec((tm, D), lambda i: (i, 0)),
                  _resident((1, D)), _resident((D, D_FF)), _resident((D_FF, D))],
        out_specs=pl.BlockSpec((tm, D), lambda i: (i, 0)),
        out_shape=jax.ShapeDtypeStruct((M, D), F32),
        compiler_params=_cp(1),
        name="mlp",
    )(h, g, w1, w2)


def _matmul_res_kernel(a_ref, w_ref, h_ref, o_ref):
    o_ref[...] = h_ref[...] + _dot(a_ref[...].astype(BF16), w_ref[...])


def _matmul_res(a, w, h, *, tm):
    M, K = a.shape
    return pl.pallas_call(
        _matmul_res_kernel,
        grid=(M // tm,),
        in_specs=[pl.BlockSpec((tm, K), lambda i: (i, 0)),
                  _resident((K, D)),
                  pl.BlockSpec((tm, D), lambda i: (i, 0))],
        out_specs=pl.BlockSpec((tm, D), lambda i: (i, 0)),
        out_shape=jax.ShapeDtypeStruct((M, D), F32),
        compiler_params=_cp(1),
        name="matmul_res",
    )(a, w, h)


def _final_norm_kernel(h_ref, g_ref, o_ref):
    o_ref[...] = _rms(h_ref[...], g_ref[...])


def _final_norm(h, g, *, rows, tm):
    return pl.pallas_call(
        _final_norm_kernel,
        grid=(rows // tm,),
        in_specs=[pl.BlockSpec((tm, D), lambda i: (i, 0)),
                  pl.BlockSpec((1, D), lambda i: (0, 0))],
        out_specs=pl.BlockSpec((tm, D), lambda i: (i, 0)),
        out_shape=jax.ShapeDtypeStruct((rows, D), F32),
        compiler_params=_cp(1),
        name="final_norm",
    )(h, g)


def _log_gamma(h):
    return math.log(1.0 - 2.0 ** (-5.0 - h))


def _ret_chunk_kernel(p_ref, s0_ref, y_ref, so_ref, s_ref, *, L, nC, natural_out):
    c = pl.program_id(1)
    half = DK // 2

    @pl.when(c == 0)
    def _():
        for h in range(RH):
            s_ref[h] = s0_ref[0, h]

    ti = lax.broadcasted_iota(jnp.int32, (L, L), 0)
    si = lax.broadcasted_iota(jnp.int32, (L, L), 1)
    diff = (ti - si).astype(F32)
    ri = lax.broadcasted_iota(jnp.int32, (L, 1), 0).astype(F32)
    for h in range(RH):
        lg = _log_gamma(h)
        mask = jnp.where(diff >= 0, jnp.exp(jnp.maximum(diff, 0.0) * lg), 0.0)
        qd = jnp.exp((ri + 1.0) * lg)
        kd = jnp.exp((L - 1.0 - ri) * lg)
        q = p_ref[:, h * DK:(h + 1) * DK]
        k = p_ref[:, HK + h * DK:HK + (h + 1) * DK]
        v = p_ref[:, 2 * HK + h * DV:2 * HK + (h + 1) * DV].astype(BF16)
        g = p_ref[:, 2 * HK + HV + h * DV:2 * HK + HV + (h + 1) * DV].astype(F32)
        sc = lax.dot_general(q.astype(BF16), k.astype(BF16), NT, preferred_element_type=F32) * mask
        inner = _dot(sc.astype(BF16), v)
        s_old = s_ref[h]
        cross = _dot((q.astype(F32) * qd).astype(BF16), s_old.astype(BF16))
        o = inner + cross
        s_ref[h] = s_old * math.exp(L * lg) + lax.dot_general(
            (k.astype(F32) * kd).astype(BF16), v, TN, preferred_element_type=F32)
        o = o * lax.rsqrt(jnp.mean(o * o, axis=-1, keepdims=True) + EPS)
        y_ref[:, h * DV:(h + 1) * DV] = (g * _sigmoid(g) * o).astype(y_ref.dtype)

    @pl.when(c == nC - 1)
    def _():
        if not natural_out:
            so_ref[0] = s_ref[...]
            return
        d = lax.broadcasted_iota(jnp.int32, (DK, DK), 0)
        pcol = lax.broadcasted_iota(jnp.int32, (DK, DK), 1)
        perm = jnp.where(pcol == (d % 2) * half + d // 2, 1.0, 0.0).astype(BF16)
        for h in range(RH):
            x = s_ref[h]
            x0 = x.astype(BF16)
            r1 = x - x0.astype(F32)
            x1 = r1.astype(BF16)
            x2 = (r1 - x1.astype(F32)).astype(BF16)
            so_ref[0, h] = _dot(perm, x0) + _dot(perm, x1) + _dot(perm, x2)


def _ret_chunk(p, s0, *, nb, nC, L, row_block0, out_dtype, natural_out):
    shared = s0.shape[0] == 1
    return pl.pallas_call(
        functools.partial(_ret_chunk_kernel, L=L, nC=nC, natural_out=natural_out),
        grid=(nb, nC),
        in_specs=[pl.BlockSpec((L, 2 * HK + 2 * HV), lambda b, c: (row_block0 + b * nC + c, 0)),
                  pl.BlockSpec((1, RH, DK, DV), lambda b, c: (0 if shared else b, 0, 0, 0))],
        out_specs=[pl.BlockSpec((L, HV), lambda b, c: (b * nC + c, 0)),
                   pl.BlockSpec((1, RH, DK, DV), lambda b, c: (b, 0, 0, 0))],
        out_shape=[jax.ShapeDtypeStruct((nb * nC * L, HV), out_dtype),
                   jax.ShapeDtypeStruct((nb, RH, DK, DV), F32)],
        scratch_shapes=[pltpu.VMEM((RH, DK, DV), F32)],
        compiler_params=_cp(2),
        name="ret_chunk",
    )(p, s0)


def _ret_dec_kernel(qkt_ref, p_ref, s_ref, y_ref, so_ref, *, bb, nbatch):
    step = pl.program_id(0)
    lane = lax.broadcasted_iota(jnp.int32, (1, nbatch), 1)

    def body(jb, carry):
        b = step * bb + jb
        onehot = (lane == b).astype(F32)
        rows = []
        for h in range(RH):
            gam = math.exp(_log_gamma(h))
            vrow = p_ref[pl.ds(b, 1), 2 * HK + h * DV:2 * HK + (h + 1) * DV]
            qcol = jnp.sum(qkt_ref[h * DK:(h + 1) * DK, :] * onehot, axis=1, keepdims=True)
            kcol = jnp.sum(qkt_ref[HK + h * DK:HK + (h + 1) * DK, :] * onehot, axis=1, keepdims=True)
            sn = gam * s_ref[jb, h] + kcol * vrow
            so_ref[jb, h] = sn
            o = jnp.sum(qcol * sn, axis=0, keepdims=True)
            o = o * lax.rsqrt(jnp.mean(o * o, axis=-1, keepdims=True) + EPS)
            g = p_ref[pl.ds(b, 1), 2 * HK + HV + h * DV:2 * HK + HV + (h + 1) * DV]
            rows.append(g * _sigmoid(g) * o)
        y_ref[b] = jnp.concatenate(rows, axis=1)
        return carry

    lax.fori_loop(0, bb, body, 0)


def _ret_dec(qkt, p, s, *, bb):
    nbatch = s.shape[0]
    return pl.pallas_call(
        functools.partial(_ret_dec_kernel, bb=bb, nbatch=nbatch),
        grid=(nbatch // bb,),
        in_specs=[pl.BlockSpec(qkt.shape, lambda i: (0, 0)),
                  pl.BlockSpec(p.shape, lambda i: (0, 0)),
                  pl.BlockSpec((bb, RH, DK, DV), lambda i: (i, 0, 0, 0))],
        out_specs=[pl.BlockSpec((nbatch, 1, HV), lambda i: (0, 0, 0)),
                   pl.BlockSpec((bb, RH, DK, DV), lambda i: (i, 0, 0, 0))],
        out_shape=[jax.ShapeDtypeStruct((nbatch, 1, HV), F32),
                   jax.ShapeDtypeStruct(s.shape, F32)],
        compiler_params=_cp(1),
        name="ret_dec",
    )(qkt, p, s)


N_RWKV_OUT = 8


def _rwkv_core(xn, xprev, w_refs, out_refs):
    (mix_ref, wrkv_ref, w0_ref, w1_ref, w2_ref, a0_ref, a1_ref, a2_ref, g1_ref, g2_ref,
     kk_ref, ka_ref, rk_ref, eh_ref, eht_ref) = w_refs
    r_o, lw_o, k_o, v_o, kk_o, b_o, g_o, bonus_o = out_refs
    xx = xprev - xn

    def xm(j):
        return (xn + xx * mix_ref[j:j + 1, :]).astype(BF16)

    r = _dot(xm(0), wrkv_ref[0])
    k = _dot(xm(1), wrkv_ref[1])
    v = _dot(xm(2), wrkv_ref[2])
    wl = _dot(jnp.tanh(_dot(xm(3), w1_ref[...])).astype(BF16), w2_ref[...])
    nx = -(w0_ref[...] + wl)
    softplus = jnp.maximum(nx, 0.0) + jnp.log(1.0 + jnp.exp(-jnp.abs(nx)))
    w = -softplus - 0.5
    lw_o[...] = -jnp.exp(w)
    al = _dot(_dot(xm(4), a1_ref[...]).astype(BF16), a2_ref[...])
    a = _sigmoid(a0_ref[...] + al)
    g_o[...] = _dot(_sigmoid(_dot(xm(5), g1_ref[...])).astype(BF16), g2_ref[...]).astype(g_o.dtype)
    kk = k * kk_ref[...]
    ssq = _seg_sum(kk * kk, eh_ref[...], eht_ref[...])
    kkn = kk / jnp.maximum(jnp.sqrt(ssq), 1e-12)
    k2 = k * (1.0 + (a - 1.0) * ka_ref[...])
    rk = _seg_sum(r * k2 * rk_ref[...], eh_ref[...], eht_ref[...])
    r_o[...] = r.astype(r_o.dtype)
    k_o[...] = k2.astype(k_o.dtype)
    v_o[...] = v.astype(v_o.dtype)
    kk_o[...] = kkn.astype(kk_o.dtype)
    b_o[...] = (kkn * a).astype(b_o.dtype)
    bonus_o[...] = (rk * v).astype(bonus_o.dtype)


def _rwkv_proj_main_kernel(h_ref, hp_ref, sh0_ref, g_ref, *refs, tm, tps):
    w_refs = refs[:15]
    out_refs = refs[15:15 + N_RWKV_OUT]
    sho_ref = refs[15 + N_RWKV_OUT]
    i = pl.program_id(0)
    xn = _rms(h_ref[...], g_ref[...])
    prevn = _rms(hp_ref[...], g_ref[...])[7:8, :]
    prev = jnp.where(i % tps == 0, sh0_ref[...], prevn)
    row = lax.broadcasted_iota(jnp.int32, (tm, 1), 0)
    xprev = jnp.where(row == 0, prev, pltpu.roll(xn, 1, 0))
    _rwkv_core(xn, xprev, w_refs, out_refs)
    sho_ref[0] = xn[tm - 1:tm, :]


def _rwkv_proj_small_kernel(h_ref, shp_ref, g_ref, *refs, n_dec):
    w_refs = refs[:15]
    out_refs = refs[15:15 + N_RWKV_OUT]
    xn_ref = refs[15 + N_RWKV_OUT]
    t_refs = refs[16 + N_RWKV_OUT:]
    m = h_ref.shape[0]
    xn = _rms(h_ref[...], g_ref[...])
    row = lax.broadcasted_iota(jnp.int32, (m, 1), 0)
    xprev = jnp.where(row <= n_dec, shp_ref[...], pltpu.roll(xn, 1, 0))
    _rwkv_core(xn, xprev, w_refs, out_refs)
    xn_ref[...] = xn
    for src, dst in zip(out_refs[:6], t_refs):
        dst[...] = src[0:n_dec, :].T


def _rwkv_weight_specs(wts):
    return [_resident(w.shape) for w in wts]


def _rwkv_proj_main(h, sh0, g, wts, *, tm, seq):
    M = h.shape[0]
    tps = seq // tm
    nseq = M // seq
    odt = [BF16, F32, BF16, BF16, BF16, BF16, BF16, BF16]
    row_spec = pl.BlockSpec((tm, D), lambda i: (i, 0))
    return pl.pallas_call(
        functools.partial(_rwkv_proj_main_kernel, tm=tm, tps=tps),
        grid=(M // tm,),
        in_specs=[row_spec,
                  pl.BlockSpec((8, D), lambda i: (jnp.maximum(i * (tm // 8) - 1, 0), 0)),
                  pl.BlockSpec((1, D), lambda i: (0, 0)),
                  pl.BlockSpec((1, D), lambda i: (0, 0))] + _rwkv_weight_specs(wts),
        out_specs=[row_spec] * N_RWKV_OUT + [pl.BlockSpec((1, 1, D), lambda i: (i // tps, 0, 0))],
        out_shape=[jax.ShapeDtypeStruct((M, D), dt) for dt in odt] + [jax.ShapeDtypeStruct((nseq, 1, D), F32)],
        compiler_params=_cp(1),
        name="rwkv_proj_main",
    )(h, h, sh0, g, *wts)


def _rwkv_proj_small(h, shp, g, wts, *, n_dec):
    M = h.shape[0]
    full = pl.BlockSpec((M, D), lambda i: (0, 0))
    return pl.pallas_call(
        functools.partial(_rwkv_proj_small_kernel, n_dec=n_dec),
        grid=(1,),
        in_specs=[full, full, pl.BlockSpec((1, D), lambda i: (0, 0))] + _rwkv_weight_specs(wts),
        out_specs=[full] * (N_RWKV_OUT + 1) + [pl.BlockSpec((D, n_dec), lambda i: (0, 0))] * 6,
        out_shape=[jax.ShapeDtypeStruct((M, D), F32)] * (N_RWKV_OUT + 1) + [jax.ShapeDtypeStruct((D, n_dec), F32)] * 6,
        compiler_params=_cp(1),
        name="rwkv_proj_small",
    )(h, shp, g, *wts)


def _mm(a, b, passes):
    if passes == 1:
        return _dot(a.astype(BF16), b.astype(BF16))
    ah, al = _split(a)
    bh, bl = _split(b)
    m = a.shape[0]
    r = _dot(jnp.concatenate([ah, al], axis=0), bh)
    return r[0:m] + r[m:2 * m] + _dot(ah, bl)


def _wkv_chunk_kernel(r_ref, lw_ref, k_ref, v_ref, kk_ref, b_ref, s0_ref, y_ref, so_ref, s_ref, *, C, nC, G, shared):
    c = pl.program_id(1)

    @pl.when(c == 0)
    def _():
        for g in range(G):
            s_ref[g] = s0_ref[0 if shared else g]

    ti = lax.broadcasted_iota(jnp.int32, (C, C), 0)
    si = lax.broadcasted_iota(jnp.int32, (C, C), 1)
    incl = si <= ti
    strict = si < ti
    tri = jnp.where(incl, 1.0, 0.0).astype(BF16)
    t2 = lax.broadcasted_iota(jnp.int32, (C, 2 * C), 0)
    s2 = lax.broadcasted_iota(jnp.int32, (C, 2 * C), 1)
    incl2 = jnp.where(s2 >= C, s2 - C, s2) <= t2

    at, rt, kt, bt, ke, be, wc, vb = [], [], [], [], [], [], [], []
    for g in range(G):
        lw = lw_ref[g].astype(F32)
        l0 = lw.astype(BF16)
        rem = lw - l0.astype(F32)
        l1 = rem.astype(BF16)
        l2 = (rem - l1.astype(F32)).astype(BF16)
        cum = _dot(tri, l0) + _dot(tri, l1) + _dot(tri, l2)
        tot = cum[C - 1:C, :]
        w_in = jnp.exp(-cum)
        w_end = jnp.exp(tot - cum)
        kf = k_ref[g].astype(F32)
        bf = b_ref[g].astype(F32)
        rt.append((r_ref[g].astype(F32) * jnp.exp(cum)).astype(BF16))
        at.append((-kk_ref[g].astype(F32) * jnp.exp(cum - lw)).astype(BF16))
        kt.append((kf * w_in).astype(BF16))
        bt.append((bf * w_in).astype(BF16))
        ke.append((kf * w_end).astype(BF16))
        be.append((bf * w_end).astype(BF16))
        wc.append(jnp.exp(tot))
        vb.append(v_ref[g].astype(BF16))

    items = [(g, h) for g in range(G) for h in range(WH)]
    n = range(len(items))
    sl = lambda h: slice(h * WN, (h + 1) * WN)
    s_old = [s_ref[g, h] for g, h in items]
    lhs = [jnp.concatenate([at[g][:, sl(h)], rt[g][:, sl(h)]], axis=0) for g, h in items]
    rhs = [jnp.concatenate([bt[g][:, sl(h)], kt[g][:, sl(h)], s_old[i].astype(BF16)], axis=0)
           for i, (g, h) in enumerate(items)]
    gm = [lax.dot_general(lhs[i], rhs[i], NT, preferred_element_type=F32) for i in n]
    lh = [m[:, 2 * C:2 * C + WN] for m in gm]
    vh = [vb[g][:, sl(h)] for g, h in items]
    a_ak = [jnp.where(strict, m[0:C, C:2 * C], 0.0).astype(BF16) for m in gm]
    a_r = [jnp.where(incl2, m[C:2 * C, 0:2 * C], 0.0).astype(BF16) for m in gm]
    cat = [jnp.concatenate([jnp.where(strict, gm[i][0:C, 0:C], 0.0), lh[i][0:C] + _dot(a_ak[i], vh[i])], axis=1)
           for i in n]
    keep_x = lax.broadcasted_iota(jnp.int32, (C, C + WN), 1) >= C
    nlev = int(math.log2(C))
    for lv in range(nlev - 1):
        passes = 3 if lv < INV_HI_LEVELS else 1
        cat = [_mm(cat[i][:, 0:C], cat[i], passes) + jnp.where(keep_x, cat[i], 0.0) for i in n]
    u = [cat[i][:, C:C + WN] + _mm(cat[i][:, 0:C], cat[i][:, C:C + WN], 1) for i in n]
    uv = [jnp.concatenate([u[i].astype(BF16), vh[i]], axis=0) for i in n]
    y = [lh[i][C:2 * C] + _dot(a_r[i], uv[i]) for i in n]
    for g in range(G):
        y_ref[g] = jnp.concatenate(y[g * WH:(g + 1) * WH], axis=1)
    for i, (g, h) in enumerate(items):
        s_ref[g, h] = s_old[i] * wc[g][:, sl(h)] + lax.dot_general(
            uv[i], jnp.concatenate([be[g][:, sl(h)], ke[g][:, sl(h)]], axis=0), TN, preferred_element_type=F32)

    @pl.when(c == nC - 1)
    def _():
        so_ref[...] = s_ref[...]


def _wkv_chunk(ins, s0, *, nC, C, G, row_block0):
    nseq = ins[0].shape[0]
    shared = s0.shape[0] == 1
    row_spec = pl.BlockSpec((G, C, D), lambda b, c: (b, row_block0 + c, 0))
    st_spec = pl.BlockSpec((G, WH, WN, WN), lambda b, c: (b, 0, 0, 0))
    s0_spec = pl.BlockSpec((1, WH, WN, WN), lambda b, c: (0, 0, 0, 0)) if shared else st_spec
    return pl.pallas_call(
        functools.partial(_wkv_chunk_kernel, C=C, nC=nC, G=G, shared=shared),
        grid=(nseq // G, nC),
        in_specs=[row_spec] * 6 + [s0_spec],
        out_specs=[pl.BlockSpec((G, C, D), lambda b, c: (b, c, 0)), st_spec],
        out_shape=[jax.ShapeDtypeStruct((nseq, nC * C, D), F32),
                   jax.ShapeDtypeStruct((nseq, WH, WN, WN), F32)],
        scratch_shapes=[pltpu.VMEM((G, WH, WN, WN), F32)],
        compiler_params=_cp(2),
        name="wkv_chunk",
    )(*ins, s0)


def _wkv_dec_kernel(r_ref, lw_ref, k_ref, v_ref, kk_ref, b_ref, s_ref, y_ref, so_ref):
    nkk = -kk_ref[...]
    w = jnp.exp(lw_ref[...])
    bb = b_ref[...]
    k2 = k_ref[...]
    rr = r_ref[...]
    for vi in range(WN):
        s_old = s_ref[0, vi]
        sa = jnp.sum(s_old * nkk, axis=0, keepdims=True)
        sn = s_old * w + sa * bb + v_ref[vi:vi + 1, :] * k2
        so_ref[0, vi] = sn
        y_ref[vi:vi + 1, :] = jnp.sum(sn * rr, axis=0, keepdims=True)


def _wkv_dec(ins, s):
    nbatch = s.shape[-1]
    vec = pl.BlockSpec((WN, nbatch), lambda h: (h, 0))
    st = pl.BlockSpec((1, WN, WN, nbatch), lambda h: (h, 0, 0, 0))
    return pl.pallas_call(
        _wkv_dec_kernel,
        grid=(WH,),
        in_specs=[vec] * 6 + [st],
        out_specs=[vec, st],
        out_shape=[jax.ShapeDtypeStruct((D, nbatch), F32), jax.ShapeDtypeStruct(s.shape, F32)],
        compiler_params=_cp(1),
        name="wkv_dec",
    )(*ins, s)


def _rwkv_out_kernel(y_ref, bonus_ref, g_ref, h_ref, lng_ref, lnb_ref, wo_ref, eh_ref, eht_ref, o_ref):
    y = y_ref[...]
    eh = eh_ref[...]
    eht = eht_ref[...]
    mu = _seg_sum(y, eh, eht) * (1.0 / WN)
    d = y - mu
    var = _seg_sum(d * d, eh, eht) * (1.0 / WN)
    yn = d * lax.rsqrt(var + GN_EPS) * lng_ref[...] + lnb_ref[...]
    z = (yn + bonus_ref[...].astype(F32)) * g_ref[...].astype(F32)
    o_ref[...] = h_ref[...] + _dot(z.astype(BF16), wo_ref[...])


def _rwkv_out(y, bonus, g, h, lng, lnb, wo, eh, eht, *, tm):
    M = h.shape[0]
    row = pl.BlockSpec((tm, D), lambda i: (i, 0))
    vec = pl.BlockSpec((1, D), lambda i: (0, 0))
    return pl.pallas_call(
        _rwkv_out_kernel,
        grid=(M // tm,),
        in_specs=[row, row, row, row, vec, vec,
                  pl.BlockSpec((D, D), lambda i: (0, 0)),
                  pl.BlockSpec(eh.shape, lambda i: (0, 0)),
                  pl.BlockSpec(eht.shape, lambda i: (0, 0))],
        out_specs=row,
        out_shape=jax.ShapeDtypeStruct((M, D), F32),
        compiler_params=_cp(1),
        name="rwkv_out",
    )(y, bonus, g, h, lng, lnb, wo, eh, eht)


def _conv_main_kernel(p_ref, pp_ref, buf_ref, cw_ref, wo_ref, h_ref, o_ref, nb_ref, *, tm, tps):
    i = pl.program_id(0)
    bq = p_ref[:, 0:D].astype(F32)
    u = p_ref[:, D:2 * D].astype(F32) * p_ref[:, 2 * D:3 * D].astype(F32)
    up = pp_ref[:, D:2 * D].astype(F32) * pp_ref[:, 2 * D:3 * D].astype(F32)
    npr = pp_ref.shape[0]
    first = i % tps == 0
    prev1 = jnp.where(first, buf_ref[1:2, :], up[npr - 1:npr, :])
    prev2 = jnp.where(first, buf_ref[0:1, :], up[npr - 2:npr - 1, :])
    row = lax.broadcasted_iota(jnp.int32, (tm, 1), 0)
    m1 = jnp.where(row == 0, prev1, pltpu.roll(u, 1, 0))
    m2 = jnp.where(row == 0, prev2, jnp.where(row == 1, prev1, pltpu.roll(u, 2, 0)))
    y = cw_ref[0:1, :] * m2 + cw_ref[1:2, :] * m1 + cw_ref[2:3, :] * u
    o_ref[...] = h_ref[...] + _dot((bq * y).astype(BF16), wo_ref[...])
    nb_ref[0] = u[tm - 2:tm, :]


def _conv_main(p, buf, cw, wo, h, *, tm, seq):
    M = h.shape[0]
    tps = seq // tm
    npr = 16
    return pl.pallas_call(
        functools.partial(_conv_main_kernel, tm=tm, tps=tps),
        grid=(M // tm,),
        in_specs=[pl.BlockSpec((tm, 3 * D), lambda i: (i, 0)),
                  pl.BlockSpec((npr, 3 * D), lambda i: (jnp.maximum(i * (tm // npr) - 1, 0), 0)),
                  pl.BlockSpec((2, D), lambda i: (0, 0)),
                  pl.BlockSpec((3, D), lambda i: (0, 0)),
                  pl.BlockSpec((D, D), lambda i: (0, 0)),
                  pl.BlockSpec((tm, D), lambda i: (i, 0))],
        out_specs=[pl.BlockSpec((tm, D), lambda i: (i, 0)),
                   pl.BlockSpec((1, 2, D), lambda i: (i // tps, 0, 0))],
        out_shape=[jax.ShapeDtypeStruct((M, D), F32), jax.ShapeDtypeStruct((M // seq, 2, D), F32)],
        compiler_params=_cp(1),
        name="conv_main",
    )(p, p, buf, cw, wo, h)


def _conv_small_kernel(p_ref, b0_ref, b1_ref, cw_ref, wo_ref, h_ref, o_ref, u_ref, *, n_dec):
    m = h_ref.shape[0]
    bq = p_ref[:, 0:D]
    u = p_ref[:, D:2 * D] * p_ref[:, 2 * D:3 * D]
    row = lax.broadcasted_iota(jnp.int32, (m, 1), 0)
    m1 = jnp.where(row <= n_dec, b1_ref[...], pltpu.roll(u, 1, 0))
    m2 = jnp.where(row <= n_dec + 1, b0_ref[...], pltpu.roll(u, 2, 0))
    y = cw_ref[0:1, :] * m2 + cw_ref[1:2, :] * m1 + cw_ref[2:3, :] * u
    o_ref[...] = h_ref[...] + _dot((bq * y).astype(BF16), wo_ref[...])
    u_ref[...] = u


def _conv_small(p, b0, b1, cw, wo, h, *, n_dec):
    M = h.shape[0]
    full = pl.BlockSpec((M, D), lambda i: (0, 0))
    return pl.pallas_call(
        functools.partial(_conv_small_kernel, n_dec=n_dec),
        grid=(1,),
        in_specs=[pl.BlockSpec((M, 3 * D), lambda i: (0, 0)), full, full,
                  pl.BlockSpec((3, D), lambda i: (0, 0)),
                  pl.BlockSpec((D, D), lambda i: (0, 0)), full],
        out_specs=[full, full],
        out_shape=[jax.ShapeDtypeStruct((M, D), F32)] * 2,
        compiler_params=_cp(1),
        name="conv_small",
    )(p, b0, b1, cw, wo, h)


def _rope_tables(pos):
    inv = 1.0 / (ROPE_BASE ** jnp.linspace(0.0, 1.0, DK // 2, dtype=F32))
    ang = pos.astype(F32)[:, None] * inv[None, :]
    return jnp.cos(ang), jnp.sin(ang)


def kernel(x_prompt, x_sample, state_ret_l0, state_rwkv_shift_l1, state_rwkv_wkv_l1, state_conv_l2, state_ret_l3,
           meta_tokens, norm_mix, norm_mlp, norm_final, ret_w_in, ret_w_out, rwkv_mix, rwkv_w_rkv, rwkv_w0, rwkv_w1,
           rwkv_w2, rwkv_a0, rwkv_a1, rwkv_a2, rwkv_g1, rwkv_g2, rwkv_k_k, rwkv_k_a, rwkv_r_k, rwkv_ln_g, rwkv_ln_b,
           rwkv_w_o, conv_w_in, conv_w, conv_w_out, mlp_w1, mlp_w2):
    B, T, _ = x_prompt.shape
    NB = x_sample.shape[0]
    MS = NB + N_META
    past_len = 16384

    def ret_qk(j):
        w = ret_w_in[j][:, :2 * HK]
        return w.reshape(D, 2 * RH, DK // 2, 2).transpose(0, 1, 3, 2).reshape(D, 2 * HK).astype(BF16)

    ret_wp = [(ret_qk(j), ret_w_in[j][:, 2 * HK:].astype(BF16)) for j in range(2)]
    ret_wt = [ret_w_in[j][:, :2 * HK].T.astype(BF16) for j in range(2)]
    ret_wo = [ret_w_out[0].astype(BF16), ret_w_out[1].astype(BF16)]
    w1b = [mlp_w1[i].astype(BF16) for i in range(4)]
    w2b = [mlp_w2[i].astype(BF16) for i in range(4)]
    head_of_lane = jnp.arange(D) // WN
    eh = (head_of_lane[:, None] == jnp.arange(LANES)[None, :]).astype(BF16)
    eht = eh.T
    vec = lambda a: a.reshape(1, D).astype(F32)
    rwkv_wts = (rwkv_mix.astype(F32), rwkv_w_rkv.astype(BF16), vec(rwkv_w0), rwkv_w1.astype(BF16), rwkv_w2.astype(BF16),
                vec(rwkv_a0), rwkv_a1.astype(BF16), rwkv_a2.astype(BF16), rwkv_g1.astype(BF16), rwkv_g2.astype(BF16),
                vec(rwkv_k_k), vec(rwkv_k_a), vec(rwkv_r_k), eh, eht)
    wo_rwkv = rwkv_w_o.astype(BF16)
    conv_wi = conv_w_in.astype(BF16)
    conv_wo = conv_w_out.astype(BF16)
    conv_wf = conv_w.astype(F32)

    cos_m, sin_m = _rope_tables(N_META + jnp.arange(T))
    pos_s = jnp.concatenate([jnp.full((NB,), past_len, jnp.int32), jnp.arange(N_META, dtype=jnp.int32)])
    cos_s, sin_s = _rope_tables(pos_s)
    cos_t = jnp.repeat(cos_s[:NB].T, 2, axis=0)
    sin_t = jnp.repeat(sin_s[:NB].T, 2, axis=0) * jnp.where(jnp.arange(DK) % 2 == 0, -1.0, 1.0)[:, None]

    h = jnp.concatenate([x_sample.reshape(NB, D), meta_tokens.astype(F32)], axis=0)
    meta_blk = NB // N_META
    zero_ret = jnp.zeros((1, RH, DK, DV), F32)
    ret_meta, ret_dec_out = [], []
    for i in range(4):
        g_mix = norm_mix[i].reshape(1, D)
        kind = i % 3
        if kind == 0:
            j = i // 3
            p = _norm_matmul(h, g_mix, ret_wp[j], tm=MS, tn=1024, out_dtype=F32, rope=(cos_s, sin_s))
            qkt = _norm_matmul_t(h, g_mix, ret_wt[j], cos_t, sin_t, rows=NB)
            s_in = (state_ret_l0 if j == 0 else state_ret_l3).astype(F32)
            y_dec, s_dec = _ret_dec(qkt, p, s_in, bb=2)
            y_meta, s_meta = _ret_chunk(p, zero_ret, nb=1, nC=1, L=N_META, row_block0=meta_blk, out_dtype=F32,
                                        natural_out=False)
            ret_meta.append(s_meta)
            ret_dec_out.append(s_dec)
            h = _matmul_res(jnp.concatenate([y_dec.reshape(NB, HV), y_meta], axis=0), ret_wo[j], h, tm=MS)
        elif kind == 1:
            shp = jnp.concatenate([state_rwkv_shift_l1.astype(F32), jnp.zeros((N_META, D), F32)], axis=0)
            outs = _rwkv_proj_small(h, shp, g_mix, rwkv_wts, n_dec=NB)
            r_, lw_, k_, v_, kk_, b_, gate_, bonus_, xn_ = outs[:9]
            y_dec, wkv_dec = _wkv_dec(outs[9:], jnp.transpose(state_rwkv_wkv_l1.astype(F32), (1, 2, 3, 0)))
            wkv_dec = jnp.transpose(wkv_dec, (3, 0, 1, 2))
            y_meta, wkv_meta = _wkv_chunk([t[None] for t in (r_, lw_, k_, v_, kk_, b_)], jnp.zeros((1, WH, WN, WN), F32),
                                          nC=1, C=N_META, G=1, row_block0=meta_blk)
            y_meta = y_meta[0]
            shift_dec = xn_[:NB]
            shift_meta = xn_[MS - 1:MS]
            h = _rwkv_out(jnp.concatenate([y_dec.T, y_meta], axis=0), bonus_, gate_, h, vec(rwkv_ln_g), vec(rwkv_ln_b),
                          wo_rwkv, eh, eht, tm=MS)
        else:
            p = _norm_matmul(h, g_mix, (conv_wi,), tm=MS, tn=1024, out_dtype=F32)
            zpad = jnp.zeros((N_META, D), F32)
            b0 = jnp.concatenate([state_conv_l2[:, 0].astype(F32), zpad], axis=0)
            b1 = jnp.concatenate([state_conv_l2[:, 1].astype(F32), zpad], axis=0)
            h, u = _conv_small(p, b0, b1, conv_wf, conv_wo, h, n_dec=NB)
            conv_dec = jnp.stack([state_conv_l2[:, 1].astype(F32), u[:NB]], axis=1)
            conv_meta = u[MS - 2:MS]
        h = _mlp(h, norm_mlp[i].reshape(1, D), w1b[i], w2b[i], tm=MS, tf=512)
    y_sample = _final_norm(h, norm_final.reshape(1, D), rows=NB, tm=NB).reshape(NB, 1, D)

    h = x_prompt.reshape(B * T, D)
    nC = T // RET_CHUNK
    ret_main = []
    for i in range(4):
        g_mix = norm_mix[i].reshape(1, D)
        kind = i % 3
        if kind == 0:
            j = i // 3
            p = _norm_matmul(h, g_mix, ret_wp[j], tm=512, tn=1024, out_dtype=BF16, rope=(cos_m, sin_m))
            y, s_fin = _ret_chunk(p, ret_meta[j], nb=B, nC=nC, L=RET_CHUNK, row_block0=0, out_dtype=BF16,
                                  natural_out=True)
            ret_main.append(s_fin)
            h = _matmul_res(y, ret_wo[j], h, tm=512)
        elif kind == 1:
            outs = _rwkv_proj_main(h, shift_meta, g_mix, rwkv_wts, tm=256, seq=T)
            r_, lw_, k_, v_, kk_, b_, gate_, bonus_, shift_main = outs
            y, wkv_main = _wkv_chunk([t.reshape(B, T, D) for t in (r_, lw_, k_, v_, kk_, b_)], wkv_meta,
                                     nC=T // WKV_CHUNK, C=WKV_CHUNK, G=WKV_GROUP, row_block0=0)
            y = y.reshape(B * T, D)
            h = _rwkv_out(y, bonus_, gate_, h, vec(rwkv_ln_g), vec(rwkv_ln_b), wo_rwkv, eh, eht, tm=512)
        else:
            p = _norm_matmul(h, g_mix, (conv_wi,), tm=512, tn=1024, out_dtype=BF16)
            h, conv_main = _conv_main(p, conv_meta, conv_wf, conv_wo, h, tm=512, seq=T)
        h = _mlp(h, norm_mlp[i].reshape(1, D), w1b[i], w2b[i], tm=1024, tf=512)
    y_prompt = _final_norm(h, norm_final.reshape(1, D), rows=B * T, tm=1024).reshape(B, T, D)

    return (y_prompt, y_sample, ret_main[0], ret_dec_out[0], shift_main.reshape(B, D), shift_dec,
            wkv_main, wkv_dec, conv_main, conv_dec, ret_main[1], ret_dec_out[1])
```

```python
import functools
import math

import jax
import jax.numpy as jnp
from jax import lax
from jax.experimental import pallas as pl
from jax.experimental.pallas import tpu as pltpu

F32 = jnp.float32
BF16 = jnp.bfloat16

D = 1024
N_META = 16
RH = 4
DK = D // RH
DV = 2 * D // RH
HK = RH * DK
HV = RH * DV
RET_CHUNK = 128
ROPE_BASE = 10000.0
WH = 16
WN = 64
WKV_CHUNK = 64
INV_HI_LEVELS = 4
WKV_GROUP = 2
D_FF = 4 * D
EPS = 1e-6
GN_EPS = 64e-5
LANES = 128
VMEM_LIMIT_V7X = 56 * 1024 * 1024

NT = (((1,), (1,)), ((), ()))
TN = (((0,), (0,)), ((), ()))


def _cp(n_axes):
    return pltpu.CompilerParams(dimension_semantics=("arbitrary",) * n_axes,
                                vmem_limit_bytes=VMEM_LIMIT_V7X)


def _dot(a, b):
    return jnp.dot(a, b, preferred_element_type=F32)


def _rms(x, g):
    return x * lax.rsqrt(jnp.mean(x * x, axis=-1, keepdims=True) + EPS) * g


def _sigmoid(x):
    return 1.0 / (1.0 + jnp.exp(-x))


def _split(x):
    hi = x.astype(BF16)
    lo = (x - hi.astype(F32)).astype(BF16)
    return hi, lo


def _seg_sum(x, eh, eht):
    return _dot(_dot(x.astype(BF16), eh).astype(BF16), eht)


def _resident(shape):
    return pl.BlockSpec(shape, lambda *idx: (0,) * len(shape), pipeline_mode=pl.Buffered(1))


def _norm_matmul_kernel(x_ref, g_ref, cos_ref, sin_ref, *refs, tn, rope):
    w_refs, o_ref = refs[:-1], refs[-1]
    xn = _rms(x_ref[...], g_ref[...]).astype(BF16)
    half = DK // 2
    col = 0
    for gi, w_ref in enumerate(w_refs):
        for j in range(w_ref.shape[1] // tn):
            acc = _dot(xn, w_ref[:, j * tn:(j + 1) * tn])
            if not (rope and gi == 0):
                o_ref[:, col:col + tn] = acc.astype(o_ref.dtype)
            else:
                c = cos_ref[...]
                s = sin_ref[...]
                scale = DK ** -0.5 if col >= HK else 1.0
                for hh in range(tn // DK):
                    x1 = acc[:, hh * DK:hh * DK + half]
                    x2 = acc[:, hh * DK + half:(hh + 1) * DK]
                    lo = col + hh * DK
                    o_ref[:, lo:lo + half] = ((x1 * c - x2 * s) * scale).astype(o_ref.dtype)
                    o_ref[:, lo + half:lo + DK] = ((x1 * s + x2 * c) * scale).astype(o_ref.dtype)
            col += tn


def _norm_matmul(x, g, ws, *, tm, tn, out_dtype, rope=None):
    M, K = x.shape
    N = sum(w.shape[1] for w in ws)
    if rope is None:
        cos = sin = jnp.zeros((8, LANES), F32)
        cs_spec = pl.BlockSpec((8, LANES), lambda i: (0, 0))
    else:
        cos, sin = rope
        nblk = cos.shape[0] // tm
        cs_spec = pl.BlockSpec((tm, LANES), lambda i: (i % nblk, 0))
    return pl.pallas_call(
        functools.partial(_norm_matmul_kernel, tn=tn, rope=rope is not None),
        grid=(M // tm,),
        in_specs=[pl.BlockSpec((tm, K), lambda i: (i, 0)), _resident((1, K)), cs_spec, cs_spec]
        + [_resident(w.shape) for w in ws],
        out_specs=pl.BlockSpec((tm, N), lambda i: (i, 0)),
        out_shape=jax.ShapeDtypeStruct((M, N), out_dtype),
        compiler_params=_cp(1),
        name="norm_matmul",
    )(x, g, cos, sin, *ws)


def _norm_matmul_t_kernel(x_ref, g_ref, wt_ref, cos_ref, sin_ref, o_ref):
    xn = _rms(x_ref[...], g_ref[...]).astype(BF16)
    acc = lax.dot_general(wt_ref[...], xn, NT, preferred_element_type=F32)
    c = cos_ref[...]
    s = sin_ref[...]
    even = lax.broadcasted_iota(jnp.int32, (DK, 1), 0) % 2 == 0
    for hh in range(2 * RH):
        scale = 1.0 if hh < RH else DK ** -0.5
        x = acc[hh * DK:(hh + 1) * DK, :]
        partner = jnp.where(even, pltpu.roll(x, DK - 1, 0), pltpu.roll(x, 1, 0))
        o_ref[hh * DK:(hh + 1) * DK, :] = (x * c + partner * s) * scale


def _norm_matmul_t(x, g, wt, cos_t, sin_t, *, rows):
    K = x.shape[1]
    return pl.pallas_call(
        _norm_matmul_t_kernel,
        grid=(1,),
        in_specs=[pl.BlockSpec((rows, K), lambda i: (0, 0)),
                  pl.BlockSpec((1, K), lambda i: (0, 0)),
                  pl.BlockSpec((2 * HK, K), lambda i: (0, 0)),
                  pl.BlockSpec((DK, rows), lambda i: (0, 0)),
                  pl.BlockSpec((DK, rows), lambda i: (0, 0))],
        out_specs=pl.BlockSpec((2 * HK, rows), lambda i: (0, 0)),
        out_shape=jax.ShapeDtypeStruct((2 * HK, rows), F32),
        compiler_params=_cp(1),
        name="norm_matmul_t",
    )(x, g, wt, cos_t, sin_t)


MLP_TF = 512


def _mlp_tail(h1, g_ref, w1_ref, w2_ref, gf_ref, o_ref):
    xn = _rms(h1, g_ref[...]).astype(BF16)
    acc = None
    for f in range(D_FF // MLP_TF):
        a = _dot(xn, w1_ref[:, f * MLP_TF:(f + 1) * MLP_TF])
        a = jnp.square(jnp.maximum(a, 0.0)).astype(BF16)
        part = _dot(a, w2_ref[f * MLP_TF:(f + 1) * MLP_TF, :])
        acc = part if acc is None else acc + part
    out = h1 + acc
    o_ref[...] = out if gf_ref is None else _rms(out, gf_ref[...])


def _layer_resident(arr, layer):
    return pl.BlockSpec((None,) + arr.shape[1:], lambda *idx: (layer, 0, 0), pipeline_mode=pl.Buffered(1))


def _tail_specs(mlp, layer, final):
    g, w1, w2, gf = mlp
    specs = [_layer_resident(g, layer), _layer_resident(w1, layer), _layer_resident(w2, layer)]
    ops = [g, w1, w2]
    if final:
        specs.append(_resident(gf.shape))
        ops.append(gf)
    return specs, ops


def _ret_tail_kernel(a_ref, wo_ref, h_ref, g_ref, w1_ref, w2_ref, *rest):
    gf_ref, o_ref = (rest[0], rest[1]) if len(rest) == 2 else (None, rest[0])
    h1 = h_ref[...] + _dot(a_ref[...].astype(BF16), wo_ref[...])
    _mlp_tail(h1, g_ref, w1_ref, w2_ref, gf_ref, o_ref)


def _ret_tail(a, wo, h, mlp, *, layer, final, tm):
    M, K = a.shape
    tspecs, tops = _tail_specs(mlp, layer, final)
    row = pl.BlockSpec((tm, D), lambda i: (i, 0))
    return pl.pallas_call(
        _ret_tail_kernel,
        grid=(M // tm,),
        in_specs=[pl.BlockSpec((tm, K), lambda i: (i, 0)), _resident((K, D)), row] + tspecs,
        out_specs=row,
        out_shape=jax.ShapeDtypeStruct((M, D), F32),
        compiler_params=_cp(1),
        name="ret_tail",
    )(a, wo, h, *tops)


def _log_gamma(h):
    return math.log(1.0 - 2.0 ** (-5.0 - h))


def _ret_chunk_kernel(p_ref, s0_ref, y_ref, so_ref, s_ref, *, L, nC, natural_out):
    c = pl.program_id(1)
    half = DK // 2

    @pl.when(c == 0)
    def _():
        for h in range(RH):
            s_ref[h] = s0_ref[0, h]

    ti = lax.broadcasted_iota(jnp.int32, (L, L), 0)
    si = lax.broadcasted_iota(jnp.int32, (L, L), 1)
    diff = (ti - si).astype(F32)
    ri = lax.broadcasted_iota(jnp.int32, (L, 1), 0).astype(F32)
    for h in range(RH):
        lg = _log_gamma(h)
        mask = jnp.where(diff >= 0, jnp.exp(jnp.maximum(diff, 0.0) * lg), 0.0)
        qd = jnp.exp((ri + 1.0) * lg)
        kd = jnp.exp((L - 1.0 - ri) * lg)
        q = p_ref[:, h * DK:(h + 1) * DK]
        k = p_ref[:, HK + h * DK:HK + (h + 1) * DK]
        v = p_ref[:, 2 * HK + h * DV:2 * HK + (h + 1) * DV].astype(BF16)
        g = p_ref[:, 2 * HK + HV + h * DV:2 * HK + HV + (h + 1) * DV].astype(F32)
        sc = lax.dot_general(q.astype(BF16), k.astype(BF16), NT, preferred_element_type=F32) * mask
        inner = _dot(sc.astype(BF16), v)
        s_old = s_ref[h]
        cross = _dot((q.astype(F32) * qd).astype(BF16), s_old.astype(BF16))
        o = inner + cross
        s_ref[h] = s_old * math.exp(L * lg) + lax.dot_general(
            (k.astype(F32) * kd).astype(BF16), v, TN, preferred_element_type=F32)
        o = o * lax.rsqrt(jnp.mean(o * o, axis=-1, keepdims=True) + EPS)
        y_ref[:, h * DV:(h + 1) * DV] = (g * _sigmoid(g) * o).astype(y_ref.dtype)

    @pl.when(c == nC - 1)
    def _():
        if not natural_out:
            so_ref[0] = s_ref[...]
            return
        d = lax.broadcasted_iota(jnp.int32, (DK, DK), 0)
        pcol = lax.broadcasted_iota(jnp.int32, (DK, DK), 1)
        perm = jnp.where(pcol == (d % 2) * half + d // 2, 1.0, 0.0).astype(BF16)
        for h in range(RH):
            x = s_ref[h]
            x0 = x.astype(BF16)
            r1 = x - x0.astype(F32)
            x1 = r1.astype(BF16)
            x2 = (r1 - x1.astype(F32)).astype(BF16)
            so_ref[0, h] = _dot(perm, x0) + _dot(perm, x1) + _dot(perm, x2)


def _ret_chunk(p, s0, *, nb, nC, L, row_block0, out_dtype, natural_out):
    shared = s0.shape[0] == 1
    return pl.pallas_call(
        functools.partial(_ret_chunk_kernel, L=L, nC=nC, natural_out=natural_out),
        grid=(nb, nC),
        in_specs=[pl.BlockSpec((L, 2 * HK + 2 * HV), lambda b, c: (row_block0 + b * nC + c, 0)),
                  pl.BlockSpec((1, RH, DK, DV), lambda b, c: (0 if shared else b, 0, 0, 0))],
        out_specs=[pl.BlockSpec((L, HV), lambda b, c: (b * nC + c, 0)),
                   pl.BlockSpec((1, RH, DK, DV), lambda b, c: (b, 0, 0, 0))],
        out_shape=[jax.ShapeDtypeStruct((nb * nC * L, HV), out_dtype),
                   jax.ShapeDtypeStruct((nb, RH, DK, DV), F32)],
        scratch_shapes=[pltpu.VMEM((RH, DK, DV), F32)],
        compiler_params=_cp(2),
        name="ret_chunk",
    )(p, s0)


def _ret_dec_kernel(qkt_ref, p_ref, s_ref, y_ref, so_ref, *, bb, nbatch):
    step = pl.program_id(0)
    lane = lax.broadcasted_iota(jnp.int32, (1, nbatch), 1)

    def body(jb, carry):
        b = step * bb + jb
        onehot = (lane == b).astype(F32)
        rows = []
        for h in range(RH):
            gam = math.exp(_log_gamma(h))
            vrow = p_ref[pl.ds(b, 1), 2 * HK + h * DV:2 * HK + (h + 1) * DV]
            qcol = jnp.sum(qkt_ref[h * DK:(h + 1) * DK, :] * onehot, axis=1, keepdims=True)
            kcol = jnp.sum(qkt_ref[HK + h * DK:HK + (h + 1) * DK, :] * onehot, axis=1, keepdims=True)
            sn = gam * s_ref[jb, h] + kcol * vrow
            so_ref[jb, h] = sn
            o = jnp.sum(qcol * sn, axis=0, keepdims=True)
            o = o * lax.rsqrt(jnp.mean(o * o, axis=-1, keepdims=True) + EPS)
            g = p_ref[pl.ds(b, 1), 2 * HK + HV + h * DV:2 * HK + HV + (h + 1) * DV]
            rows.append(g * _sigmoid(g) * o)
        y_ref[b] = jnp.concatenate(rows, axis=1)
        return carry

    lax.fori_loop(0, bb, body, 0)


def _ret_dec(qkt, p, s, *, bb):
    nbatch = s.shape[0]
    return pl.pallas_call(
        functools.partial(_ret_dec_kernel, bb=bb, nbatch=nbatch),
        grid=(nbatch // bb,),
        in_specs=[pl.BlockSpec(qkt.shape, lambda i: (0, 0)),
                  pl.BlockSpec(p.shape, lambda i: (0, 0)),
                  pl.BlockSpec((bb, RH, DK, DV), lambda i: (i, 0, 0, 0))],
        out_specs=[pl.BlockSpec((nbatch, 1, HV), lambda i: (0, 0, 0)),
                   pl.BlockSpec((bb, RH, DK, DV), lambda i: (i, 0, 0, 0))],
        out_shape=[jax.ShapeDtypeStruct((nbatch, 1, HV), F32),
                   jax.ShapeDtypeStruct(s.shape, F32)],
        compiler_params=_cp(1),
        name="ret_dec",
    )(qkt, p, s)


N_RWKV_OUT = 8


def _rwkv_core(xn, xprev, w_refs, out_refs):
    (mix_ref, wrkv_ref, w0_ref, w1_ref, w2_ref, a0_ref, a1_ref, a2_ref, g1_ref, g2_ref,
     kk_ref, ka_ref, rk_ref, eh_ref, eht_ref) = w_refs
    r_o, lw_o, k_o, v_o, kk_o, b_o, g_o, bonus_o = out_refs
    xx = xprev - xn

    def xm(j):
        return (xn + xx * mix_ref[j:j + 1, :]).astype(BF16)

    r = _dot(xm(0), wrkv_ref[0])
    k = _dot(xm(1), wrkv_ref[1])
    v = _dot(xm(2), wrkv_ref[2])
    wl = _dot(jnp.tanh(_dot(xm(3), w1_ref[...])).astype(BF16), w2_ref[...])
    nx = -(w0_ref[...] + wl)
    softplus = jnp.maximum(nx, 0.0) + jnp.log(1.0 + jnp.exp(-jnp.abs(nx)))
    w = -softplus - 0.5
    lw_o[...] = -jnp.exp(w)
    al = _dot(_dot(xm(4), a1_ref[...]).astype(BF16), a2_ref[...])
    a = _sigmoid(a0_ref[...] + al)
    g_o[...] = _dot(_sigmoid(_dot(xm(5), g1_ref[...])).astype(BF16), g2_ref[...]).astype(g_o.dtype)
    kk = k * kk_ref[...]
    ssq = _seg_sum(kk * kk, eh_ref[...], eht_ref[...])
    kkn = kk / jnp.maximum(jnp.sqrt(ssq), 1e-12)
    k2 = k * (1.0 + (a - 1.0) * ka_ref[...])
    rk = _seg_sum(r * k2 * rk_ref[...], eh_ref[...], eht_ref[...])
    r_o[...] = r.astype(r_o.dtype)
    k_o[...] = k2.astype(k_o.dtype)
    v_o[...] = v.astype(v_o.dtype)
    kk_o[...] = kkn.astype(kk_o.dtype)
    b_o[...] = (kkn * a).astype(b_o.dtype)
    bonus_o[...] = (rk * v).astype(bonus_o.dtype)


def _rwkv_proj_main_kernel(h_ref, hp_ref, sh0_ref, g_ref, *refs, tm, tps):
    w_refs = refs[:15]
    out_refs = refs[15:15 + N_RWKV_OUT]
    sho_ref = refs[15 + N_RWKV_OUT]
    i = pl.program_id(0)
    xn = _rms(h_ref[...], g_ref[...])
    prevn = _rms(hp_ref[...], g_ref[...])[7:8, :]
    prev = jnp.where(i % tps == 0, sh0_ref[...], prevn)
    row = lax.broadcasted_iota(jnp.int32, (tm, 1), 0)
    xprev = jnp.where(row == 0, prev, pltpu.roll(xn, 1, 0))
    _rwkv_core(xn, xprev, w_refs, out_refs)
    sho_ref[0] = xn[tm - 1:tm, :]


def _rwkv_proj_small_kernel(h_ref, shp_ref, g_ref, *refs, n_dec):
    w_refs = refs[:15]
    out_refs = refs[15:15 + N_RWKV_OUT]
    xn_ref = refs[15 + N_RWKV_OUT]
    t_refs = refs[16 + N_RWKV_OUT:]
    m = h_ref.shape[0]
    xn = _rms(h_ref[...], g_ref[...])
    row = lax.broadcasted_iota(jnp.int32, (m, 1), 0)
    xprev = jnp.where(row <= n_dec, shp_ref[...], pltpu.roll(xn, 1, 0))
    _rwkv_core(xn, xprev, w_refs, out_refs)
    xn_ref[...] = xn
    for src, dst in zip(out_refs[:6], t_refs):
        dst[...] = src[0:n_dec, :].T


def _rwkv_weight_specs(wts):
    return [_resident(w.shape) for w in wts]


def _rwkv_proj_main(h, sh0, g, wts, *, tm, seq):
    M = h.shape[0]
    tps = seq // tm
    nseq = M // seq
    odt = [BF16, F32, BF16, BF16, BF16, BF16, BF16, BF16]
    row_spec = pl.BlockSpec((tm, D), lambda i: (i, 0))
    return pl.pallas_call(
        functools.partial(_rwkv_proj_main_kernel, tm=tm, tps=tps),
        grid=(M // tm,),
        in_specs=[row_spec,
                  pl.BlockSpec((8, D), lambda i: (jnp.maximum(i * (tm // 8) - 1, 0), 0)),
                  pl.BlockSpec((1, D), lambda i: (0, 0)),
                  pl.BlockSpec((1, D), lambda i: (0, 0))] + _rwkv_weight_specs(wts),
        out_specs=[row_spec] * N_RWKV_OUT + [pl.BlockSpec((1, 1, D), lambda i: (i // tps, 0, 0))],
        out_shape=[jax.ShapeDtypeStruct((M, D), dt) for dt in odt] + [jax.ShapeDtypeStruct((nseq, 1, D), F32)],
        compiler_params=_cp(1),
        name="rwkv_proj_main",
    )(h, h, sh0, g, *wts)


def _rwkv_proj_small(h, shp, g, wts, *, n_dec):
    M = h.shape[0]
    full = pl.BlockSpec((M, D), lambda i: (0, 0))
    return pl.pallas_call(
        functools.partial(_rwkv_proj_small_kernel, n_dec=n_dec),
        grid=(1,),
        in_specs=[full, full, pl.BlockSpec((1, D), lambda i: (0, 0))] + _rwkv_weight_specs(wts),
        out_specs=[full] * (N_RWKV_OUT + 1) + [pl.BlockSpec((D, n_dec), lambda i: (0, 0))] * 6,
        out_shape=[jax.ShapeDtypeStruct((M, D), F32)] * (N_RWKV_OUT + 1) + [jax.ShapeDtypeStruct((D, n_dec), F32)] * 6,
        compiler_params=_cp(1),
        name="rwkv_proj_small",
    )(h, shp, g, *wts)


def _mm(a, b, passes):
    if passes == 1:
        return _dot(a.astype(BF16), b.astype(BF16))
    ah, al = _split(a)
    bh, bl = _split(b)
    m = a.shape[0]
    r = _dot(jnp.concatenate([ah, al], axis=0), bh)
    return r[0:m] + r[m:2 * m] + _dot(ah, bl)


def _wkv_chunk_kernel(r_ref, lw_ref, k_ref, v_ref, kk_ref, b_ref, s0_ref, y_ref, so_ref, s_ref, *, C, nC, G, shared):
    c = pl.program_id(1)

    @pl.when(c == 0)
    def _():
        for g in range(G):
            s_ref[g] = s0_ref[0 if shared else g]

    ti = lax.broadcasted_iota(jnp.int32, (C, C), 0)
    si = lax.broadcasted_iota(jnp.int32, (C, C), 1)
    incl = si <= ti
    strict = si < ti
    tri = jnp.where(incl, 1.0, 0.0).astype(BF16)
    t2 = lax.broadcasted_iota(jnp.int32, (C, 2 * C), 0)
    s2 = lax.broadcasted_iota(jnp.int32, (C, 2 * C), 1)
    incl2 = jnp.where(s2 >= C, s2 - C, s2) <= t2

    at, rt, kt, bt, ke, be, wc, vb = [], [], [], [], [], [], [], []
    for g in range(G):
        lw = lw_ref[g].astype(F32)
        l0 = lw.astype(BF16)
        rem = lw - l0.astype(F32)
        l1 = rem.astype(BF16)
        l2 = (rem - l1.astype(F32)).astype(BF16)
        cum = _dot(tri, l0) + _dot(tri, l1) + _dot(tri, l2)
        tot = cum[C - 1:C, :]
        w_in = jnp.exp(-cum)
        w_end = jnp.exp(tot - cum)
        kf = k_ref[g].astype(F32)
        bf = b_ref[g].astype(F32)
        rt.append((r_ref[g].astype(F32) * jnp.exp(cum)).astype(BF16))
        at.append((-kk_ref[g].astype(F32) * jnp.exp(cum - lw)).astype(BF16))
        kt.append((kf * w_in).astype(BF16))
        bt.append((bf * w_in).astype(BF16))
        ke.append((kf * w_end).astype(BF16))
        be.append((bf * w_end).astype(BF16))
        wc.append(jnp.exp(tot))
        vb.append(v_ref[g].astype(BF16))

    items = [(g, h) for g in range(G) for h in range(WH)]
    n = range(len(items))
    sl = lambda h: slice(h * WN, (h + 1) * WN)
    s_old = [s_ref[g, h] for g, h in items]
    lhs = [jnp.concatenate([at[g][:, sl(h)], rt[g][:, sl(h)]], axis=0) for g, h in items]
    rhs = [jnp.concatenate([bt[g][:, sl(h)], kt[g][:, sl(h)], s_old[i].astype(BF16)], axis=0)
           for i, (g, h) in enumerate(items)]
    gm = [lax.dot_general(lhs[i], rhs[i], NT, preferred_element_type=F32) for i in n]
    lh = [m[:, 2 * C:2 * C + WN] for m in gm]
    vh = [vb[g][:, sl(h)] for g, h in items]
    a_ak = [jnp.where(strict, m[0:C, C:2 * C], 0.0).astype(BF16) for m in gm]
    a_r = [jnp.where(incl2, m[C:2 * C, 0:2 * C], 0.0).astype(BF16) for m in gm]
    cat = [jnp.concatenate([jnp.where(strict, gm[i][0:C, 0:C], 0.0), lh[i][0:C] + _dot(a_ak[i], vh[i])], axis=1)
           for i in n]
    keep_x = lax.broadcasted_iota(jnp.int32, (C, C + WN), 1) >= C
    nlev = int(math.log2(C))
    for lv in range(nlev - 1):
        passes = 3 if lv < INV_HI_LEVELS else 1
        cat = [_mm(cat[i][:, 0:C], cat[i], passes) + jnp.where(keep_x, cat[i], 0.0) for i in n]
    u = [cat[i][:, C:C + WN] + _mm(cat[i][:, 0:C], cat[i][:, C:C + WN], 1) for i in n]
    uv = [jnp.concatenate([u[i].astype(BF16), vh[i]], axis=0) for i in n]
    y = [lh[i][C:2 * C] + _dot(a_r[i], uv[i]) for i in n]
    for g in range(G):
        y_ref[g] = jnp.concatenate(y[g * WH:(g + 1) * WH], axis=1)
    for i, (g, h) in enumerate(items):
        s_ref[g, h] = s_old[i] * wc[g][:, sl(h)] + lax.dot_general(
            uv[i], jnp.concatenate([be[g][:, sl(h)], ke[g][:, sl(h)]], axis=0), TN, preferred_element_type=F32)

    @pl.when(c == nC - 1)
    def _():
        so_ref[...] = s_ref[...]


def _wkv_chunk(ins, s0, *, nC, C, G, row_block0):
    nseq = ins[0].shape[0]
    shared = s0.shape[0] == 1
    row_spec = pl.BlockSpec((G, C, D), lambda b, c: (b, row_block0 + c, 0))
    st_spec = pl.BlockSpec((G, WH, WN, WN), lambda b, c: (b, 0, 0, 0))
    s0_spec = pl.BlockSpec((1, WH, WN, WN), lambda b, c: (0, 0, 0, 0)) if shared else st_spec
    return pl.pallas_call(
        functools.partial(_wkv_chunk_kernel, C=C, nC=nC, G=G, shared=shared),
        grid=(nseq // G, nC),
        in_specs=[row_spec] * 6 + [s0_spec],
        out_specs=[pl.BlockSpec((G, C, D), lambda b, c: (b, c, 0)), st_spec],
        out_shape=[jax.ShapeDtypeStruct((nseq, nC * C, D), F32),
                   jax.ShapeDtypeStruct((nseq, WH, WN, WN), F32)],
        scratch_shapes=[pltpu.VMEM((G, WH, WN, WN), F32)],
        compiler_params=_cp(2),
        name="wkv_chunk",
    )(*ins, s0)


def _wkv_dec_kernel(r_ref, lw_ref, k_ref, v_ref, kk_ref, b_ref, s_ref, y_ref, so_ref):
    nkk = -kk_ref[...]
    w = jnp.exp(lw_ref[...])
    bb = b_ref[...]
    k2 = k_ref[...]
    rr = r_ref[...]
    for vi in range(WN):
        s_old = s_ref[0, vi]
        sa = jnp.sum(s_old * nkk, axis=0, keepdims=True)
        sn = s_old * w + sa * bb + v_ref[vi:vi + 1, :] * k2
        so_ref[0, vi] = sn
        y_ref[vi:vi + 1, :] = jnp.sum(sn * rr, axis=0, keepdims=True)


def _wkv_dec(ins, s):
    nbatch = s.shape[-1]
    vec = pl.BlockSpec((WN, nbatch), lambda h: (h, 0))
    st = pl.BlockSpec((1, WN, WN, nbatch), lambda h: (h, 0, 0, 0))
    return pl.pallas_call(
        _wkv_dec_kernel,
        grid=(WH,),
        in_specs=[vec] * 6 + [st],
        out_specs=[vec, st],
        out_shape=[jax.ShapeDtypeStruct((D, nbatch), F32), jax.ShapeDtypeStruct(s.shape, F32)],
        compiler_params=_cp(1),
        name="wkv_dec",
    )(*ins, s)


def _rwkv_tail_kernel(y_ref, bonus_ref, gate_ref, lng_ref, lnb_ref, wo_ref, eh_ref, eht_ref, h_ref,
                      g_ref, w1_ref, w2_ref, o_ref):
    y = y_ref[...]
    eh = eh_ref[...]
    eht = eht_ref[...]
    mu = _seg_sum(y, eh, eht) * (1.0 / WN)
    d = y - mu
    var = _seg_sum(d * d, eh, eht) * (1.0 / WN)
    yn = d * lax.rsqrt(var + GN_EPS) * lng_ref[...] + lnb_ref[...]
    z = (yn + bonus_ref[...].astype(F32)) * gate_ref[...].astype(F32)
    h1 = h_ref[...] + _dot(z.astype(BF16), wo_ref[...])
    _mlp_tail(h1, g_ref, w1_ref, w2_ref, None, o_ref)


def _rwkv_tail(y, bonus, gate, h, lng, lnb, wo, eh, eht, mlp, *, layer, tm):
    M = h.shape[0]
    tspecs, tops = _tail_specs(mlp, layer, False)
    row = pl.BlockSpec((tm, D), lambda i: (i, 0))
    return pl.pallas_call(
        _rwkv_tail_kernel,
        grid=(M // tm,),
        in_specs=[row, row, row, _resident((1, D)), _resident((1, D)), _resident((D, D)),
                  _resident(eh.shape), _resident(eht.shape), row] + tspecs,
        out_specs=row,
        out_shape=jax.ShapeDtypeStruct((M, D), F32),
        compiler_params=_cp(1),
        name="rwkv_tail",
    )(y, bonus, gate, lng, lnb, wo, eh, eht, h, *tops)


def _conv_main_kernel(p_ref, pp_ref, buf_ref, cw_ref, wo_ref, h_ref, g_ref, w1_ref, w2_ref, o_ref, nb_ref, *, tm, tps):
    i = pl.program_id(0)
    bq = p_ref[:, 0:D].astype(F32)
    u = p_ref[:, D:2 * D].astype(F32) * p_ref[:, 2 * D:3 * D].astype(F32)
    up = pp_ref[:, D:2 * D].astype(F32) * pp_ref[:, 2 * D:3 * D].astype(F32)
    npr = pp_ref.shape[0]
    first = i % tps == 0
    prev1 = jnp.where(first, buf_ref[1:2, :], up[npr - 1:npr, :])
    prev2 = jnp.where(first, buf_ref[0:1, :], up[npr - 2:npr - 1, :])
    row = lax.broadcasted_iota(jnp.int32, (tm, 1), 0)
    m1 = jnp.where(row == 0, prev1, pltpu.roll(u, 1, 0))
    m2 = jnp.where(row == 0, prev2, jnp.where(row == 1, prev1, pltpu.roll(u, 2, 0)))
    y = cw_ref[0:1, :] * m2 + cw_ref[1:2, :] * m1 + cw_ref[2:3, :] * u
    nb_ref[0] = u[tm - 2:tm, :]
    h1 = h_ref[...] + _dot((bq * y).astype(BF16), wo_ref[...])
    _mlp_tail(h1, g_ref, w1_ref, w2_ref, None, o_ref)


def _conv_main(p, buf, cw, wo, h, mlp, *, layer, tm, seq):
    M = h.shape[0]
    tps = seq // tm
    npr = 16
    tspecs, tops = _tail_specs(mlp, layer, False)
    return pl.pallas_call(
        functools.partial(_conv_main_kernel, tm=tm, tps=tps),
        grid=(M // tm,),
        in_specs=[pl.BlockSpec((tm, 3 * D), lambda i: (i, 0)),
                  pl.BlockSpec((npr, 3 * D), lambda i: (jnp.maximum(i * (tm // npr) - 1, 0), 0)),
                  _resident((2, D)), _resident((3, D)), _resident((D, D)),
                  pl.BlockSpec((tm, D), lambda i: (i, 0))] + tspecs,
        out_specs=[pl.BlockSpec((tm, D), lambda i: (i, 0)),
                   pl.BlockSpec((1, 2, D), lambda i: (i // tps, 0, 0))],
        out_shape=[jax.ShapeDtypeStruct((M, D), F32), jax.ShapeDtypeStruct((M // seq, 2, D), F32)],
        compiler_params=_cp(1),
        name="conv_main",
    )(p, p, buf, cw, wo, h, *tops)


def _conv_small_kernel(p_ref, b0_ref, b1_ref, cw_ref, wo_ref, h_ref, g_ref, w1_ref, w2_ref, o_ref, u_ref, *, n_dec):
    m = h_ref.shape[0]
    bq = p_ref[:, 0:D]
    u = p_ref[:, D:2 * D] * p_ref[:, 2 * D:3 * D]
    row = lax.broadcasted_iota(jnp.int32, (m, 1), 0)
    m1 = jnp.where(row <= n_dec, b1_ref[...], pltpu.roll(u, 1, 0))
    m2 = jnp.where(row <= n_dec + 1, b0_ref[...], pltpu.roll(u, 2, 0))
    y = cw_ref[0:1, :] * m2 + cw_ref[1:2, :] * m1 + cw_ref[2:3, :] * u
    u_ref[...] = u
    h1 = h_ref[...] + _dot((bq * y).astype(BF16), wo_ref[...])
    _mlp_tail(h1, g_ref, w1_ref, w2_ref, None, o_ref)


def _conv_small(p, b0, b1, cw, wo, h, mlp, *, layer, n_dec):
    M = h.shape[0]
    full = pl.BlockSpec((M, D), lambda i: (0, 0))
    tspecs, tops = _tail_specs(mlp, layer, False)
    return pl.pallas_call(
        functools.partial(_conv_small_kernel, n_dec=n_dec),
        grid=(1,),
        in_specs=[pl.BlockSpec((M, 3 * D), lambda i: (0, 0)), full, full,
                  pl.BlockSpec((3, D), lambda i: (0, 0)),
                  pl.BlockSpec((D, D), lambda i: (0, 0)), full] + tspecs,
        out_specs=[full, full],
        out_shape=[jax.ShapeDtypeStruct((M, D), F32)] * 2,
        compiler_params=_cp(1),
        name="conv_small",
    )(p, b0, b1, cw, wo, h, *tops)


def _rope_tables(pos):
    inv = 1.0 / (ROPE_BASE ** jnp.linspace(0.0, 1.0, DK // 2, dtype=F32))
    ang = pos.astype(F32)[:, None] * inv[None, :]
    return jnp.cos(ang), jnp.sin(ang)


def kernel(x_prompt, x_sample, state_ret_l0, state_rwkv_shift_l1, state_rwkv_wkv_l1, state_conv_l2, state_ret_l3,
           meta_tokens, norm_mix, norm_mlp, norm_final, ret_w_in, ret_w_out, rwkv_mix, rwkv_w_rkv, rwkv_w0, rwkv_w1,
           rwkv_w2, rwkv_a0, rwkv_a1, rwkv_a2, rwkv_g1, rwkv_g2, rwkv_k_k, rwkv_k_a, rwkv_r_k, rwkv_ln_g, rwkv_ln_b,
           rwkv_w_o, conv_w_in, conv_w, conv_w_out, mlp_w1, mlp_w2):
    B, T, _ = x_prompt.shape
    NB = x_sample.shape[0]
    MS = NB + N_META
    past_len = 16384

    def ret_qk(j):
        w = ret_w_in[j][:, :2 * HK]
        return w.reshape(D, 2 * RH, DK // 2, 2).transpose(0, 1, 3, 2).reshape(D, 2 * HK).astype(BF16)

    ret_wp = [(ret_qk(j), ret_w_in[j][:, 2 * HK:].astype(BF16)) for j in range(2)]
    ret_wt = [ret_w_in[j][:, :2 * HK].T.astype(BF16) for j in range(2)]
    ret_wo = [ret_w_out[0].astype(BF16), ret_w_out[1].astype(BF16)]
    mlp = (norm_mlp.reshape(4, 1, D).astype(F32), mlp_w1.astype(BF16), mlp_w2.astype(BF16),
           norm_final.reshape(1, D).astype(F32))
    head_of_lane = jnp.arange(D) // WN
    eh = (head_of_lane[:, None] == jnp.arange(LANES)[None, :]).astype(BF16)
    eht = eh.T
    vec = lambda a: a.reshape(1, D).astype(F32)
    rwkv_wts = (rwkv_mix.astype(F32), rwkv_w_rkv.astype(BF16), vec(rwkv_w0), rwkv_w1.astype(BF16), rwkv_w2.astype(BF16),
                vec(rwkv_a0), rwkv_a1.astype(BF16), rwkv_a2.astype(BF16), rwkv_g1.astype(BF16), rwkv_g2.astype(BF16),
                vec(rwkv_k_k), vec(rwkv_k_a), vec(rwkv_r_k), eh, eht)
    wo_rwkv = rwkv_w_o.astype(BF16)
    conv_wi = conv_w_in.astype(BF16)
    conv_wo = conv_w_out.astype(BF16)
    conv_wf = conv_w.astype(F32)

    cos_m, sin_m = _rope_tables(N_META + jnp.arange(T))
    pos_s = jnp.concatenate([jnp.full((NB,), past_len, jnp.int32), jnp.arange(N_META, dtype=jnp.int32)])
    cos_s, sin_s = _rope_tables(pos_s)
    cos_t = jnp.repeat(cos_s[:NB].T, 2, axis=0)
    sin_t = jnp.repeat(sin_s[:NB].T, 2, axis=0) * jnp.where(jnp.arange(DK) % 2 == 0, -1.0, 1.0)[:, None]

    h = jnp.concatenate([x_sample.reshape(NB, D), meta_tokens.astype(F32)], axis=0)
    meta_blk = NB // N_META
    zero_ret = jnp.zeros((1, RH, DK, DV), F32)
    ret_meta, ret_dec_out = [], []
    for i in range(4):
        g_mix = norm_mix[i].reshape(1, D)
        kind = i % 3
        if kind == 0:
            j = i // 3
            p = _norm_matmul(h, g_mix, ret_wp[j], tm=MS, tn=1024, out_dtype=F32, rope=(cos_s, sin_s))
            qkt = _norm_matmul_t(h, g_mix, ret_wt[j], cos_t, sin_t, rows=NB)
            s_in = (state_ret_l0 if j == 0 else state_ret_l3).astype(F32)
            y_dec, s_dec = _ret_dec(qkt, p, s_in, bb=2)
            y_meta, s_meta = _ret_chunk(p, zero_ret, nb=1, nC=1, L=N_META, row_block0=meta_blk, out_dtype=F32,
                                        natural_out=False)
            ret_meta.append(s_meta)
            ret_dec_out.append(s_dec)
            h = _ret_tail(jnp.concatenate([y_dec.reshape(NB, HV), y_meta], axis=0), ret_wo[j], h, mlp,
                          layer=i, final=i == 3, tm=MS)
        elif kind == 1:
            shp = jnp.concatenate([state_rwkv_shift_l1.astype(F32), jnp.zeros((N_META, D), F32)], axis=0)
            outs = _rwkv_proj_small(h, shp, g_mix, rwkv_wts, n_dec=NB)
            r_, lw_, k_, v_, kk_, b_, gate_, bonus_, xn_ = outs[:9]
            y_dec, wkv_dec = _wkv_dec(outs[9:], jnp.transpose(state_rwkv_wkv_l1.astype(F32), (1, 2, 3, 0)))
            wkv_dec = jnp.transpose(wkv_dec, (3, 0, 1, 2))
            y_meta, wkv_meta = _wkv_chunk([t[None] for t in (r_, lw_, k_, v_, kk_, b_)], jnp.zeros((1, WH, WN, WN), F32),
                                          nC=1, C=N_META, G=1, row_block0=meta_blk)
            y_meta = y_meta[0]
            shift_dec = xn_[:NB]
            shift_meta = xn_[MS - 1:MS]
            h = _rwkv_tail(jnp.concatenate([y_dec.T, y_meta], axis=0), bonus_, gate_, h, vec(rwkv_ln_g),
                           vec(rwkv_ln_b), wo_rwkv, eh, eht, mlp, layer=i, tm=MS)
        else:
            p = _norm_matmul(h, g_mix, (conv_wi,), tm=MS, tn=1024, out_dtype=F32)
            zpad = jnp.zeros((N_META, D), F32)
            b0 = jnp.concatenate([state_conv_l2[:, 0].astype(F32), zpad], axis=0)
            b1 = jnp.concatenate([state_conv_l2[:, 1].astype(F32), zpad], axis=0)
            h, u = _conv_small(p, b0, b1, conv_wf, conv_wo, h, mlp, layer=i, n_dec=NB)
            conv_dec = jnp.stack([state_conv_l2[:, 1].astype(F32), u[:NB]], axis=1)
            conv_meta = u[MS - 2:MS]
    y_sample = h[:NB].reshape(NB, 1, D)

    h = x_prompt.reshape(B * T, D)
    nC = T // RET_CHUNK
    ret_main = []
    for i in range(4):
        g_mix = norm_mix[i].reshape(1, D)
        kind = i % 3
        if kind == 0:
            j = i // 3
            p = _norm_matmul(h, g_mix, ret_wp[j], tm=512, tn=1024, out_dtype=BF16, rope=(cos_m, sin_m))
            y, s_fin = _ret_chunk(p, ret_meta[j], nb=B, nC=nC, L=RET_CHUNK, row_block0=0, out_dtype=BF16,
                                  natural_out=True)
            ret_main.append(s_fin)
            h = _ret_tail(y, ret_wo[j], h, mlp, layer=i, final=i == 3, tm=512)
        elif kind == 1:
            outs = _rwkv_proj_main(h, shift_meta, g_mix, rwkv_wts, tm=256, seq=T)
            r_, lw_, k_, v_, kk_, b_, gate_, bonus_, shift_main = outs
            y, wkv_main = _wkv_chunk([t.reshape(B, T, D) for t in (r_, lw_, k_, v_, kk_, b_)], wkv_meta,
                                     nC=T // WKV_CHUNK, C=WKV_CHUNK, G=WKV_GROUP, row_block0=0)
            y = y.reshape(B * T, D)
            h = _rwkv_tail(y, bonus_, gate_, h, vec(rwkv_ln_g), vec(rwkv_ln_b), wo_rwkv, eh, eht, mlp, layer=i, tm=512)
        else:
            p = _norm_matmul(h, g_mix, (conv_wi,), tm=512, tn=1024, out_dtype=BF16)
            h, conv_main = _conv_main(p, conv_meta, conv_wf, conv_wo, h, mlp, layer=i, tm=512, seq=T)
    y_prompt = h.reshape(B, T, D)

    return (y_prompt, y_sample, ret_main[0], ret_dec_out[0], shift_main.reshape(B, D), shift_dec,
            wkv_main, wkv_dec, conv_main, conv_dec, ret_main[1], ret_dec_out[1])
```

```python
import functools
import math

import jax
import jax.numpy as jnp
from jax import lax
from jax.experimental import pallas as pl
from jax.experimental.pallas import tpu as pltpu

F32 = jnp.float32
BF16 = jnp.bfloat16

D = 1024
N_META = 16
RH = 4
DK = D // RH
DV = 2 * D // RH
HK = RH * DK
HV = RH * DV
RET_CHUNK = 128
RET_SUB = 4
ROPE_BASE = 10000.0
WH = 16
WN = 64
WKV_CHUNK = 64
INV_HI_LEVELS = 4
WKV_GROUP = 2
D_FF = 4 * D
EPS = 1e-6
GN_EPS = 64e-5
LANES = 128
VMEM_LIMIT_V7X = 56 * 1024 * 1024

NT = (((1,), (1,)), ((), ()))
TN = (((0,), (0,)), ((), ()))


def _cp(n_axes):
    return pltpu.CompilerParams(dimension_semantics=("arbitrary",) * n_axes,
                                vmem_limit_bytes=VMEM_LIMIT_V7X)


def _dot(a, b):
    return jnp.dot(a, b, preferred_element_type=F32)


def _rms(x, g):
    return x * lax.rsqrt(jnp.mean(x * x, axis=-1, keepdims=True) + EPS) * g


def _sigmoid(x):
    return 1.0 / (1.0 + jnp.exp(-x))


def _split(x):
    hi = x.astype(BF16)
    lo = (x - hi.astype(F32)).astype(BF16)
    return hi, lo


def _seg_sum(x, eh, eht):
    return _dot(_dot(x.astype(BF16), eh).astype(BF16), eht)


def _resident(shape):
    return pl.BlockSpec(shape, lambda *idx: (0,) * len(shape), pipeline_mode=pl.Buffered(1))


def _norm_matmul_kernel(x_ref, g_ref, cos_ref, sin_ref, *refs, tn, rope):
    w_refs, o_ref = refs[:-1], refs[-1]
    xn = _rms(x_ref[...], g_ref[...]).astype(BF16)
    even = lax.broadcasted_iota(jnp.int32, (1, DK), 1) % 2 == 0
    col = 0
    for gi, w_ref in enumerate(w_refs):
        for j in range(w_ref.shape[1] // tn):
            acc = _dot(xn, w_ref[:, j * tn:(j + 1) * tn])
            if not (rope and gi == 0):
                o_ref[:, col:col + tn] = acc.astype(o_ref.dtype)
            else:
                c = cos_ref[...]
                s = sin_ref[...]
                scale = DK ** -0.5 if col >= HK else 1.0
                for hh in range(tn // DK):
                    x = acc[:, hh * DK:(hh + 1) * DK]
                    partner = jnp.where(even, pltpu.roll(x, DK - 1, 1), pltpu.roll(x, 1, 1))
                    o_ref[:, col + hh * DK:col + (hh + 1) * DK] = ((x * c + partner * s) * scale).astype(o_ref.dtype)
            col += tn


def _weight_spec(w):
    if not isinstance(w, tuple):
        return _resident(w.shape), w, w.shape[1]
    arr, layer, col0, width = w
    blk = col0 // width
    spec = pl.BlockSpec((None, arr.shape[1], width), lambda *idx: (layer, 0, blk), pipeline_mode=pl.Buffered(1))
    return spec, arr, width


def _norm_matmul(x, g, ws, *, tm, tn, out_dtype, rope=None):
    M, K = x.shape
    wspecs, wops, widths = zip(*[_weight_spec(w) for w in ws])
    N = sum(widths)
    if rope is None:
        cos = sin = jnp.zeros((8, DK), F32)
        cs_spec = pl.BlockSpec((8, DK), lambda i: (0, 0))
    else:
        cos, sin = rope
        nblk = cos.shape[0] // tm
        cs_spec = pl.BlockSpec((tm, DK), lambda i: (i % nblk, 0))
    return pl.pallas_call(
        functools.partial(_norm_matmul_kernel, tn=tn, rope=rope is not None),
        grid=(M // tm,),
        in_specs=[pl.BlockSpec((tm, K), lambda i: (i, 0)), _resident((1, K)), cs_spec, cs_spec] + list(wspecs),
        out_specs=pl.BlockSpec((tm, N), lambda i: (i, 0)),
        out_shape=jax.ShapeDtypeStruct((M, N), out_dtype),
        compiler_params=_cp(1),
        name="norm_matmul",
    )(x, g, cos, sin, *wops)


def _norm_matmul_t_kernel(x_ref, g_ref, wt_ref, cos_ref, sin_ref, o_ref):
    xn = _rms(x_ref[...], g_ref[...]).astype(BF16)
    acc = lax.dot_general(wt_ref[...], xn, NT, preferred_element_type=F32)
    c = cos_ref[...]
    s = sin_ref[...]
    even = lax.broadcasted_iota(jnp.int32, (DK, 1), 0) % 2 == 0
    for hh in range(2 * RH):
        scale = 1.0 if hh < RH else DK ** -0.5
        x = acc[hh * DK:(hh + 1) * DK, :]
        partner = jnp.where(even, pltpu.roll(x, DK - 1, 0), pltpu.roll(x, 1, 0))
        o_ref[hh * DK:(hh + 1) * DK, :] = (x * c + partner * s) * scale


def _norm_matmul_t(x, g, wt, cos_t, sin_t, *, rows):
    K = x.shape[1]
    return pl.pallas_call(
        _norm_matmul_t_kernel,
        grid=(1,),
        in_specs=[pl.BlockSpec((rows, K), lambda i: (0, 0)),
                  pl.BlockSpec((1, K), lambda i: (0, 0)),
                  pl.BlockSpec((2 * HK, K), lambda i: (0, 0)),
                  pl.BlockSpec((DK, rows), lambda i: (0, 0)),
                  pl.BlockSpec((DK, rows), lambda i: (0, 0))],
        out_specs=pl.BlockSpec((2 * HK, rows), lambda i: (0, 0)),
        out_shape=jax.ShapeDtypeStruct((2 * HK, rows), F32),
        compiler_params=_cp(1),
        name="norm_matmul_t",
    )(x, g, wt, cos_t, sin_t)


MLP_TF = 512


def _mlp_tail(h1, g_ref, w1_ref, w2_ref, gf_ref, o_ref):
    xn = _rms(h1, g_ref[...]).astype(BF16)
    acc = None
    for f in range(D_FF // MLP_TF):
        a = _dot(xn, w1_ref[:, f * MLP_TF:(f + 1) * MLP_TF])
        a = jnp.square(jnp.maximum(a, 0.0)).astype(BF16)
        part = _dot(a, w2_ref[f * MLP_TF:(f + 1) * MLP_TF, :])
        acc = part if acc is None else acc + part
    out = h1 + acc
    o_ref[...] = out if gf_ref is None else _rms(out, gf_ref[...])


def _layer_resident(arr, layer):
    return pl.BlockSpec((None,) + arr.shape[1:], lambda *idx: (layer, 0, 0), pipeline_mode=pl.Buffered(1))


def _tail_specs(mlp, layer, final):
    g, w1, w2, gf = mlp
    specs = [_layer_resident(g, layer), _layer_resident(w1, layer), _layer_resident(w2, layer)]
    ops = [g, w1, w2]
    if final:
        specs.append(_resident(gf.shape))
        ops.append(gf)
    return specs, ops


def _ret_tail_kernel(a_ref, wo_ref, h_ref, g_ref, w1_ref, w2_ref, *rest):
    gf_ref, o_ref = (rest[0], rest[1]) if len(rest) == 2 else (None, rest[0])
    h1 = h_ref[...] + _dot(a_ref[...].astype(BF16), wo_ref[...])
    _mlp_tail(h1, g_ref, w1_ref, w2_ref, gf_ref, o_ref)


def _ret_tail(a, wo, h, mlp, *, layer, final, tm):
    M, K = a.shape
    tspecs, tops = _tail_specs(mlp, layer, final)
    row = pl.BlockSpec((tm, D), lambda i: (i, 0))
    return pl.pallas_call(
        _ret_tail_kernel,
        grid=(M // tm,),
        in_specs=[pl.BlockSpec((tm, K), lambda i: (i, 0)), _resident((K, D)), row] + tspecs,
        out_specs=row,
        out_shape=jax.ShapeDtypeStruct((M, D), F32),
        compiler_params=_cp(1),
        name="ret_tail",
    )(a, wo, h, *tops)


def _log_gamma(h):
    return math.log(1.0 - 2.0 ** (-5.0 - h))


def _ret_chunk_kernel(p_ref, s0_ref, y_ref, so_ref, s_ref, *, L, nsub, nC):
    c = pl.program_id(1)

    @pl.when(c == 0)
    def _():
        for h in range(RH):
            s_ref[h] = s0_ref[0, h]

    ti = lax.broadcasted_iota(jnp.int32, (L, L), 0)
    si = lax.broadcasted_iota(jnp.int32, (L, L), 1)
    diff = (ti - si).astype(F32)
    ri = lax.broadcasted_iota(jnp.int32, (L, 1), 0).astype(F32)
    for j in range(nsub):
        rows = slice(j * L, (j + 1) * L)
        for h in range(RH):
            lg = _log_gamma(h)
            mask = jnp.where(diff >= 0, jnp.exp(jnp.maximum(diff, 0.0) * lg), 0.0)
            qd = jnp.exp((ri + 1.0) * lg)
            kd = jnp.exp((L - 1.0 - ri) * lg)
            q = p_ref[rows, h * DK:(h + 1) * DK]
            k = p_ref[rows, HK + h * DK:HK + (h + 1) * DK]
            v = p_ref[rows, 2 * HK + h * DV:2 * HK + (h + 1) * DV].astype(BF16)
            g = p_ref[rows, 2 * HK + HV + h * DV:2 * HK + HV + (h + 1) * DV].astype(F32)
            sc = lax.dot_general(q.astype(BF16), k.astype(BF16), NT, preferred_element_type=F32) * mask
            inner = _dot(sc.astype(BF16), v)
            s_old = s_ref[h]
            cross = _dot((q.astype(F32) * qd).astype(BF16), s_old.astype(BF16))
            o = inner + cross
            s_ref[h] = s_old * math.exp(L * lg) + lax.dot_general(
                (k.astype(F32) * kd).astype(BF16), v, TN, preferred_element_type=F32)
            o = o * lax.rsqrt(jnp.mean(o * o, axis=-1, keepdims=True) + EPS)
            y_ref[rows, h * DV:(h + 1) * DV] = (g * _sigmoid(g) * o).astype(y_ref.dtype)

    @pl.when(c == nC - 1)
    def _():
        so_ref[0] = s_ref[...]


def _ret_chunk(p, s0, *, nb, nC, L, nsub, row_block0, out_dtype):
    shared = s0.shape[0] == 1
    R = L * nsub
    return pl.pallas_call(
        functools.partial(_ret_chunk_kernel, L=L, nsub=nsub, nC=nC),
        grid=(nb, nC),
        in_specs=[pl.BlockSpec((R, 2 * HK + 2 * HV), lambda b, c: (row_block0 + b * nC + c, 0)),
                  pl.BlockSpec((1, RH, DK, DV), lambda b, c: (0 if shared else b, 0, 0, 0))],
        out_specs=[pl.BlockSpec((R, HV), lambda b, c: (b * nC + c, 0)),
                   pl.BlockSpec((1, RH, DK, DV), lambda b, c: (b, 0, 0, 0))],
        out_shape=[jax.ShapeDtypeStruct((nb * nC * R, HV), out_dtype),
                   jax.ShapeDtypeStruct((nb, RH, DK, DV), F32)],
        scratch_shapes=[pltpu.VMEM((RH, DK, DV), F32)],
        compiler_params=_cp(2),
        name="ret_chunk",
    )(p, s0)


def _ret_dec_kernel(qkt_ref, p_ref, s_ref, y_ref, so_ref, *, bb, nbatch):
    step = pl.program_id(0)
    lane = lax.broadcasted_iota(jnp.int32, (1, nbatch), 1)

    def body(jb, carry):
        b = step * bb + jb
        onehot = (lane == b).astype(F32)
        rows = []
        for h in range(RH):
            gam = math.exp(_log_gamma(h))
            vrow = p_ref[pl.ds(b, 1), 2 * HK + h * DV:2 * HK + (h + 1) * DV]
            qcol = jnp.sum(qkt_ref[h * DK:(h + 1) * DK, :] * onehot, axis=1, keepdims=True)
            kcol = jnp.sum(qkt_ref[HK + h * DK:HK + (h + 1) * DK, :] * onehot, axis=1, keepdims=True)
            sn = gam * s_ref[jb, h] + kcol * vrow
            so_ref[jb, h] = sn
            o = jnp.sum(qcol * sn, axis=0, keepdims=True)
            o = o * lax.rsqrt(jnp.mean(o * o, axis=-1, keepdims=True) + EPS)
            g = p_ref[pl.ds(b, 1), 2 * HK + HV + h * DV:2 * HK + HV + (h + 1) * DV]
            rows.append(g * _sigmoid(g) * o)
        y_ref[b] = jnp.concatenate(rows, axis=1)
        return carry

    lax.fori_loop(0, bb, body, 0)


def _ret_dec(qkt, p, s, *, bb):
    nbatch = s.shape[0]
    return pl.pallas_call(
        functools.partial(_ret_dec_kernel, bb=bb, nbatch=nbatch),
        grid=(nbatch // bb,),
        in_specs=[pl.BlockSpec(qkt.shape, lambda i: (0, 0)),
                  pl.BlockSpec(p.shape, lambda i: (0, 0)),
                  pl.BlockSpec((bb, RH, DK, DV), lambda i: (i, 0, 0, 0))],
        out_specs=[pl.BlockSpec((nbatch, 1, HV), lambda i: (0, 0, 0)),
                   pl.BlockSpec((bb, RH, DK, DV), lambda i: (i, 0, 0, 0))],
        out_shape=[jax.ShapeDtypeStruct((nbatch, 1, HV), F32),
                   jax.ShapeDtypeStruct(s.shape, F32)],
        compiler_params=_cp(1),
        name="ret_dec",
    )(qkt, p, s)


N_RWKV_OUT = 8


def _rwkv_core(xn, xprev, w_refs, out_refs):
    (mix_ref, wrkv_ref, w0_ref, w1_ref, w2_ref, a0_ref, a1_ref, a2_ref, g1_ref, g2_ref,
     kk_ref, ka_ref, rk_ref, eh_ref, eht_ref) = w_refs
    r_o, lw_o, k_o, v_o, kk_o, b_o, g_o, bonus_o = out_refs
    xx = xprev - xn

    def xm(j):
        return (xn + xx * mix_ref[j:j + 1, :]).astype(BF16)

    r = _dot(xm(0), wrkv_ref[0])
    k = _dot(xm(1), wrkv_ref[1])
    v = _dot(xm(2), wrkv_ref[2])
    wl = _dot(jnp.tanh(_dot(xm(3), w1_ref[...])).astype(BF16), w2_ref[...])
    nx = -(w0_ref[...] + wl)
    softplus = jnp.maximum(nx, 0.0) + jnp.log(1.0 + jnp.exp(-jnp.abs(nx)))
    w = -softplus - 0.5
    lw_o[...] = -jnp.exp(w)
    al = _dot(_dot(xm(4), a1_ref[...]).astype(BF16), a2_ref[...])
    a = _sigmoid(a0_ref[...] + al)
    g_o[...] = _dot(_sigmoid(_dot(xm(5), g1_ref[...])).astype(BF16), g2_ref[...]).astype(g_o.dtype)
    kk = k * kk_ref[...]
    ssq = _seg_sum(kk * kk, eh_ref[...], eht_ref[...])
    kkn = kk / jnp.maximum(jnp.sqrt(ssq), 1e-12)
    k2 = k * (1.0 + (a - 1.0) * ka_ref[...])
    rk = _seg_sum(r * k2 * rk_ref[...], eh_ref[...], eht_ref[...])
    r_o[...] = r.astype(r_o.dtype)
    k_o[...] = k2.astype(k_o.dtype)
    v_o[...] = v.astype(v_o.dtype)
    kk_o[...] = kkn.astype(kk_o.dtype)
    b_o[...] = (kkn * a).astype(b_o.dtype)
    bonus_o[...] = (rk * v).astype(bonus_o.dtype)


def _rwkv_proj_main_kernel(h_ref, hp_ref, sh0_ref, g_ref, *refs, tm, tps):
    w_refs = refs[:15]
    out_refs = refs[15:15 + N_RWKV_OUT]
    sho_ref = refs[15 + N_RWKV_OUT]
    i = pl.program_id(0)
    xn = _rms(h_ref[...], g_ref[...])
    prevn = _rms(hp_ref[...], g_ref[...])[7:8, :]
    prev = jnp.where(i % tps == 0, sh0_ref[...], prevn)
    row = lax.broadcasted_iota(jnp.int32, (tm, 1), 0)
    xprev = jnp.where(row == 0, prev, pltpu.roll(xn, 1, 0))
    _rwkv_core(xn, xprev, w_refs, out_refs)
    sho_ref[0] = xn[tm - 1:tm, :]


def _rwkv_proj_small_kernel(h_ref, shp_ref, g_ref, *refs, n_dec):
    w_refs = refs[:15]
    out_refs = refs[15:15 + N_RWKV_OUT]
    xn_ref = refs[15 + N_RWKV_OUT]
    t_refs = refs[16 + N_RWKV_OUT:]
    m = h_ref.shape[0]
    xn = _rms(h_ref[...], g_ref[...])
    row = lax.broadcasted_iota(jnp.int32, (m, 1), 0)
    xprev = jnp.where(row <= n_dec, shp_ref[...], pltpu.roll(xn, 1, 0))
    _rwkv_core(xn, xprev, w_refs, out_refs)
    xn_ref[...] = xn
    for src, dst in zip(out_refs[:6], t_refs):
        dst[...] = src[0:n_dec, :].T


def _rwkv_weight_specs(wts):
    return [_resident(w.shape) for w in wts]


def _rwkv_proj_main(h, sh0, g, wts, *, tm, seq):
    M = h.shape[0]
    tps = seq // tm
    nseq = M // seq
    odt = [BF16, F32, BF16, BF16, BF16, BF16, BF16, BF16]
    row_spec = pl.BlockSpec((tm, D), lambda i: (i, 0))
    return pl.pallas_call(
        functools.partial(_rwkv_proj_main_kernel, tm=tm, tps=tps),
        grid=(M // tm,),
        in_specs=[row_spec,
                  pl.BlockSpec((8, D), lambda i: (jnp.maximum(i * (tm // 8) - 1, 0), 0)),
                  pl.BlockSpec((1, D), lambda i: (0, 0)),
                  pl.BlockSpec((1, D), lambda i: (0, 0))] + _rwkv_weight_specs(wts),
        out_specs=[row_spec] * N_RWKV_OUT + [pl.BlockSpec((1, 1, D), lambda i: (i // tps, 0, 0))],
        out_shape=[jax.ShapeDtypeStruct((M, D), dt) for dt in odt] + [jax.ShapeDtypeStruct((nseq, 1, D), F32)],
        compiler_params=_cp(1),
        name="rwkv_proj_main",
    )(h, h, sh0, g, *wts)


def _rwkv_proj_small(h, shp, g, wts, *, n_dec):
    M = h.shape[0]
    full = pl.BlockSpec((M, D), lambda i: (0, 0))
    return pl.pallas_call(
        functools.partial(_rwkv_proj_small_kernel, n_dec=n_dec),
        grid=(1,),
        in_specs=[full, full, pl.BlockSpec((1, D), lambda i: (0, 0))] + _rwkv_weight_specs(wts),
        out_specs=[full] * (N_RWKV_OUT + 1) + [pl.BlockSpec((D, n_dec), lambda i: (0, 0))] * 6,
        out_shape=[jax.ShapeDtypeStruct((M, D), F32)] * (N_RWKV_OUT + 1) + [jax.ShapeDtypeStruct((D, n_dec), F32)] * 6,
        compiler_params=_cp(1),
        name="rwkv_proj_small",
    )(h, shp, g, *wts)


def _mm(a, b, passes):
    if passes == 1:
        return _dot(a.astype(BF16), b.astype(BF16))
    ah, al = _split(a)
    bh, bl = _split(b)
    m = a.shape[0]
    r = _dot(jnp.concatenate([ah, al], axis=0), bh)
    return r[0:m] + r[m:2 * m] + _dot(ah, bl)


def _wkv_chunk_kernel(r_ref, lw_ref, k_ref, v_ref, kk_ref, b_ref, s0_ref, y_ref, so_ref, s_ref, *, C, nC, G, shared):
    c = pl.program_id(1)

    @pl.when(c == 0)
    def _():
        for g in range(G):
            s_ref[g] = s0_ref[0 if shared else g]

    ti = lax.broadcasted_iota(jnp.int32, (C, C), 0)
    si = lax.broadcasted_iota(jnp.int32, (C, C), 1)
    incl = si <= ti
    strict = si < ti
    tri = jnp.where(incl, 1.0, 0.0).astype(BF16)
    t2 = lax.broadcasted_iota(jnp.int32, (C, 2 * C), 0)
    s2 = lax.broadcasted_iota(jnp.int32, (C, 2 * C), 1)
    incl2 = jnp.where(s2 >= C, s2 - C, s2) <= t2

    at, rt, kt, bt, ke, be, wc, vb = [], [], [], [], [], [], [], []
    for g in range(G):
        lw = lw_ref[g].astype(F32)
        l0 = lw.astype(BF16)
        rem = lw - l0.astype(F32)
        l1 = rem.astype(BF16)
        l2 = (rem - l1.astype(F32)).astype(BF16)
        cum = _dot(tri, l0) + _dot(tri, l1) + _dot(tri, l2)
        tot = cum[C - 1:C, :]
        w_in = jnp.exp(-cum)
        w_end = jnp.exp(tot - cum)
        kf = k_ref[g].astype(F32)
        bf = b_ref[g].astype(F32)
        rt.append((r_ref[g].astype(F32) * jnp.exp(cum)).astype(BF16))
        at.append((-kk_ref[g].astype(F32) * jnp.exp(cum - lw)).astype(BF16))
        kt.append((kf * w_in).astype(BF16))
        bt.append((bf * w_in).astype(BF16))
        ke.append((kf * w_end).astype(BF16))
        be.append((bf * w_end).astype(BF16))
        wc.append(jnp.exp(tot))
        vb.append(v_ref[g].astype(BF16))

    items = [(g, h) for g in range(G) for h in range(WH)]
    n = range(len(items))
    sl = lambda h: slice(h * WN, (h + 1) * WN)
    s_old = [s_ref[g, h] for g, h in items]
    lhs = [jnp.concatenate([at[g][:, sl(h)], rt[g][:, sl(h)]], axis=0) for g, h in items]
    rhs = [jnp.concatenate([bt[g][:, sl(h)], kt[g][:, sl(h)], s_old[i].astype(BF16)], axis=0)
           for i, (g, h) in enumerate(items)]
    gm = [lax.dot_general(lhs[i], rhs[i], NT, preferred_element_type=F32) for i in n]
    lh = [m[:, 2 * C:2 * C + WN] for m in gm]
    vh = [vb[g][:, sl(h)] for g, h in items]
    a_ak = [jnp.where(strict, m[0:C, C:2 * C], 0.0).astype(BF16) for m in gm]
    a_r = [jnp.where(incl2, m[C:2 * C, 0:2 * C], 0.0).astype(BF16) for m in gm]
    cat = [jnp.concatenate([jnp.where(strict, gm[i][0:C, 0:C], 0.0), lh[i][0:C] + _dot(a_ak[i], vh[i])], axis=1)
           for i in n]
    keep_x = lax.broadcasted_iota(jnp.int32, (C, C + WN), 1) >= C
    nlev = int(math.log2(C))
    for lv in range(nlev - 1):
        passes = 3 if lv < INV_HI_LEVELS else 1
        cat = [_mm(cat[i][:, 0:C], cat[i], passes) + jnp.where(keep_x, cat[i], 0.0) for i in n]
    u = [cat[i][:, C:C + WN] + _mm(cat[i][:, 0:C], cat[i][:, C:C + WN], 1) for i in n]
    uv = [jnp.concatenate([u[i].astype(BF16), vh[i]], axis=0) for i in n]
    y = [lh[i][C:2 * C] + _dot(a_r[i], uv[i]) for i in n]
    for g in range(G):
        y_ref[g] = jnp.concatenate(y[g * WH:(g + 1) * WH], axis=1)
    for i, (g, h) in enumerate(items):
        s_ref[g, h] = s_old[i] * wc[g][:, sl(h)] + lax.dot_general(
            uv[i], jnp.concatenate([be[g][:, sl(h)], ke[g][:, sl(h)]], axis=0), TN, preferred_element_type=F32)

    @pl.when(c == nC - 1)
    def _():
        so_ref[...] = s_ref[...]


def _wkv_chunk(ins, s0, *, nC, C, G, row_block0):
    nseq = ins[0].shape[0]
    shared = s0.shape[0] == 1
    row_spec = pl.BlockSpec((G, C, D), lambda b, c: (b, row_block0 + c, 0))
    st_spec = pl.BlockSpec((G, WH, WN, WN), lambda b, c: (b, 0, 0, 0))
    s0_spec = pl.BlockSpec((1, WH, WN, WN), lambda b, c: (0, 0, 0, 0)) if shared else st_spec
    return pl.pallas_call(
        functools.partial(_wkv_chunk_kernel, C=C, nC=nC, G=G, shared=shared),
        grid=(nseq // G, nC),
        in_specs=[row_spec] * 6 + [s0_spec],
        out_specs=[pl.BlockSpec((G, C, D), lambda b, c: (b, c, 0)), st_spec],
        out_shape=[jax.ShapeDtypeStruct((nseq, nC * C, D), F32),
                   jax.ShapeDtypeStruct((nseq, WH, WN, WN), F32)],
        scratch_shapes=[pltpu.VMEM((G, WH, WN, WN), F32)],
        compiler_params=_cp(2),
        name="wkv_chunk",
    )(*ins, s0)


def _wkv_dec_kernel(r_ref, lw_ref, k_ref, v_ref, kk_ref, b_ref, s_ref, y_ref, so_ref):
    nkk = -kk_ref[...]
    w = jnp.exp(lw_ref[...])
    bb = b_ref[...]
    k2 = k_ref[...]
    rr = r_ref[...]
    for vi in range(WN):
        s_old = s_ref[0, vi]
        sa = jnp.sum(s_old * nkk, axis=0, keepdims=True)
        sn = s_old * w + sa * bb + v_ref[vi:vi + 1, :] * k2
        so_ref[0, vi] = sn
        y_ref[vi:vi + 1, :] = jnp.sum(sn * rr, axis=0, keepdims=True)


def _wkv_dec(ins, s):
    nbatch = s.shape[-1]
    vec = pl.BlockSpec((WN, nbatch), lambda h: (h, 0))
    st = pl.BlockSpec((1, WN, WN, nbatch), lambda h: (h, 0, 0, 0))
    return pl.pallas_call(
        _wkv_dec_kernel,
        grid=(WH,),
        in_specs=[vec] * 6 + [st],
        out_specs=[vec, st],
        out_shape=[jax.ShapeDtypeStruct((D, nbatch), F32), jax.ShapeDtypeStruct(s.shape, F32)],
        compiler_params=_cp(1),
        name="wkv_dec",
    )(*ins, s)


def _rwkv_tail_kernel(y_ref, bonus_ref, gate_ref, lng_ref, lnb_ref, wo_ref, eh_ref, eht_ref, h_ref,
                      g_ref, w1_ref, w2_ref, o_ref):
    y = y_ref[...]
    eh = eh_ref[...]
    eht = eht_ref[...]
    mu = _seg_sum(y, eh, eht) * (1.0 / WN)
    d = y - mu
    var = _seg_sum(d * d, eh, eht) * (1.0 / WN)
    yn = d * lax.rsqrt(var + GN_EPS) * lng_ref[...] + lnb_ref[...]
    z = (yn + bonus_ref[...].astype(F32)) * gate_ref[...].astype(F32)
    h1 = h_ref[...] + _dot(z.astype(BF16), wo_ref[...])
    _mlp_tail(h1, g_ref, w1_ref, w2_ref, None, o_ref)


def _rwkv_tail(y, bonus, gate, h, lng, lnb, wo, eh, eht, mlp, *, layer, tm):
    M = h.shape[0]
    tspecs, tops = _tail_specs(mlp, layer, False)
    row = pl.BlockSpec((tm, D), lambda i: (i, 0))
    return pl.pallas_call(
        _rwkv_tail_kernel,
        grid=(M // tm,),
        in_specs=[row, row, row, _resident((1, D)), _resident((1, D)), _resident((D, D)),
                  _resident(eh.shape), _resident(eht.shape), row] + tspecs,
        out_specs=row,
        out_shape=jax.ShapeDtypeStruct((M, D), F32),
        compiler_params=_cp(1),
        name="rwkv_tail",
    )(y, bonus, gate, lng, lnb, wo, eh, eht, h, *tops)


def _conv_main_kernel(p_ref, pp_ref, buf_ref, cw_ref, wo_ref, h_ref, g_ref, w1_ref, w2_ref, o_ref, nb_ref, *, tm, tps):
    i = pl.program_id(0)
    bq = p_ref[:, 0:D].astype(F32)
    u = p_ref[:, D:2 * D].astype(F32) * p_ref[:, 2 * D:3 * D].astype(F32)
    up = pp_ref[:, D:2 * D].astype(F32) * pp_ref[:, 2 * D:3 * D].astype(F32)
    npr = pp_ref.shape[0]
    first = i % tps == 0
    prev1 = jnp.where(first, buf_ref[1:2, :], up[npr - 1:npr, :])
    prev2 = jnp.where(first, buf_ref[0:1, :], up[npr - 2:npr - 1, :])
    row = lax.broadcasted_iota(jnp.int32, (tm, 1), 0)
    m1 = jnp.where(row == 0, prev1, pltpu.roll(u, 1, 0))
    m2 = jnp.where(row == 0, prev2, jnp.where(row == 1, prev1, pltpu.roll(u, 2, 0)))
    y = cw_ref[0:1, :] * m2 + cw_ref[1:2, :] * m1 + cw_ref[2:3, :] * u
    nb_ref[0] = u[tm - 2:tm, :]
    h1 = h_ref[...] + _dot((bq * y).astype(BF16), wo_ref[...])
    _mlp_tail(h1, g_ref, w1_ref, w2_ref, None, o_ref)


def _conv_main(p, buf, cw, wo, h, mlp, *, layer, tm, seq):
    M = h.shape[0]
    tps = seq // tm
    npr = 16
    tspecs, tops = _tail_specs(mlp, layer, False)
    return pl.pallas_call(
        functools.partial(_conv_main_kernel, tm=tm, tps=tps),
        grid=(M // tm,),
        in_specs=[pl.BlockSpec((tm, 3 * D), lambda i: (i, 0)),
                  pl.BlockSpec((npr, 3 * D), lambda i: (jnp.maximum(i * (tm // npr) - 1, 0), 0)),
                  _resident((2, D)), _resident((3, D)), _resident((D, D)),
                  pl.BlockSpec((tm, D), lambda i: (i, 0))] + tspecs,
        out_specs=[pl.BlockSpec((tm, D), lambda i: (i, 0)),
                   pl.BlockSpec((1, 2, D), lambda i: (i // tps, 0, 0))],
        out_shape=[jax.ShapeDtypeStruct((M, D), F32), jax.ShapeDtypeStruct((M // seq, 2, D), F32)],
        compiler_params=_cp(1),
        name="conv_main",
    )(p, p, buf, cw, wo, h, *tops)


def _conv_small_kernel(p_ref, b0_ref, b1_ref, cw_ref, wo_ref, h_ref, g_ref, w1_ref, w2_ref, o_ref, u_ref, *, n_dec):
    m = h_ref.shape[0]
    bq = p_ref[:, 0:D]
    u = p_ref[:, D:2 * D] * p_ref[:, 2 * D:3 * D]
    row = lax.broadcasted_iota(jnp.int32, (m, 1), 0)
    m1 = jnp.where(row <= n_dec, b1_ref[...], pltpu.roll(u, 1, 0))
    m2 = jnp.where(row <= n_dec + 1, b0_ref[...], pltpu.roll(u, 2, 0))
    y = cw_ref[0:1, :] * m2 + cw_ref[1:2, :] * m1 + cw_ref[2:3, :] * u
    u_ref[...] = u
    h1 = h_ref[...] + _dot((bq * y).astype(BF16), wo_ref[...])
    _mlp_tail(h1, g_ref, w1_ref, w2_ref, None, o_ref)


def _conv_small(p, b0, b1, cw, wo, h, mlp, *, layer, n_dec):
    M = h.shape[0]
    full = pl.BlockSpec((M, D), lambda i: (0, 0))
    tspecs, tops = _tail_specs(mlp, layer, False)
    return pl.pallas_call(
        functools.partial(_conv_small_kernel, n_dec=n_dec),
        grid=(1,),
        in_specs=[pl.BlockSpec((M, 3 * D), lambda i: (0, 0)), full, full,
                  pl.BlockSpec((3, D), lambda i: (0, 0)),
                  pl.BlockSpec((D, D), lambda i: (0, 0)), full] + tspecs,
        out_specs=[full, full],
        out_shape=[jax.ShapeDtypeStruct((M, D), F32)] * 2,
        compiler_params=_cp(1),
        name="conv_small",
    )(p, b0, b1, cw, wo, h, *tops)


def _rope_tables(pos):
    inv = 1.0 / (ROPE_BASE ** jnp.linspace(0.0, 1.0, DK // 2, dtype=F32))
    ang = pos.astype(F32)[:, None] * inv[None, :]
    return jnp.cos(ang), jnp.sin(ang)


def kernel(x_prompt, x_sample, state_ret_l0, state_rwkv_shift_l1, state_rwkv_wkv_l1, state_conv_l2, state_ret_l3,
           meta_tokens, norm_mix, norm_mlp, norm_final, ret_w_in, ret_w_out, rwkv_mix, rwkv_w_rkv, rwkv_w0, rwkv_w1,
           rwkv_w2, rwkv_a0, rwkv_a1, rwkv_a2, rwkv_g1, rwkv_g2, rwkv_k_k, rwkv_k_a, rwkv_r_k, rwkv_ln_g, rwkv_ln_b,
           rwkv_w_o, conv_w_in, conv_w, conv_w_out, mlp_w1, mlp_w2):
    B, T, _ = x_prompt.shape
    NB = x_sample.shape[0]
    MS = NB + N_META
    past_len = 16384

    ret_wb = ret_w_in.astype(BF16)
    ret_wp = [((ret_wb, j, 0, 2 * HK), (ret_wb, j, 2 * HK, 2 * HK), (ret_wb, j, 4 * HK, 2 * HK)) for j in range(2)]
    ret_wt = [ret_w_in[j][:, :2 * HK].T.astype(BF16) for j in range(2)]
    ret_wo = [ret_w_out[0].astype(BF16), ret_w_out[1].astype(BF16)]
    mlp = (norm_mlp.reshape(4, 1, D).astype(F32), mlp_w1.astype(BF16), mlp_w2.astype(BF16),
           norm_final.reshape(1, D).astype(F32))
    head_of_lane = jnp.arange(D) // WN
    eh = (head_of_lane[:, None] == jnp.arange(LANES)[None, :]).astype(BF16)
    eht = eh.T
    vec = lambda a: a.reshape(1, D).astype(F32)
    rwkv_wts = (rwkv_mix.astype(F32), rwkv_w_rkv.astype(BF16), vec(rwkv_w0), rwkv_w1.astype(BF16), rwkv_w2.astype(BF16),
                vec(rwkv_a0), rwkv_a1.astype(BF16), rwkv_a2.astype(BF16), rwkv_g1.astype(BF16), rwkv_g2.astype(BF16),
                vec(rwkv_k_k), vec(rwkv_k_a), vec(rwkv_r_k), eh, eht)
    wo_rwkv = rwkv_w_o.astype(BF16)
    conv_wi = conv_w_in.astype(BF16)
    conv_wo = conv_w_out.astype(BF16)
    conv_wf = conv_w.astype(F32)

    pair_sign = jnp.where(jnp.arange(DK) % 2 == 0, -1.0, 1.0).astype(F32)

    def rope_rows(pos):
        c, sn = _rope_tables(pos)
        return jnp.repeat(c, 2, axis=1), jnp.repeat(sn, 2, axis=1) * pair_sign[None, :]

    cos_m, sin_m = rope_rows(N_META + jnp.arange(T))
    pos_s = jnp.concatenate([jnp.full((NB,), past_len, jnp.int32), jnp.arange(N_META, dtype=jnp.int32)])
    cos_s, sin_s = rope_rows(pos_s)
    cos_t, sin_t = cos_s[:NB].T, sin_s[:NB].T

    h = jnp.concatenate([x_sample.reshape(NB, D), meta_tokens.astype(F32)], axis=0)
    meta_blk = NB // N_META
    zero_ret = jnp.zeros((1, RH, DK, DV), F32)
    ret_meta, ret_dec_out = [], []
    for i in range(4):
        g_mix = norm_mix[i].reshape(1, D)
        kind = i % 3
        if kind == 0:
            j = i // 3
            p = _norm_matmul(h, g_mix, ret_wp[j], tm=MS, tn=1024, out_dtype=F32, rope=(cos_s, sin_s))
            qkt = _norm_matmul_t(h, g_mix, ret_wt[j], cos_t, sin_t, rows=NB)
            s_in = (state_ret_l0 if j == 0 else state_ret_l3).astype(F32)
            y_dec, s_dec = _ret_dec(qkt, p, s_in, bb=4)
            y_meta, s_meta = _ret_chunk(p, zero_ret, nb=1, nC=1, L=N_META, nsub=1, row_block0=meta_blk, out_dtype=F32)
            ret_meta.append(s_meta)
            ret_dec_out.append(s_dec)
            h = _ret_tail(jnp.concatenate([y_dec.reshape(NB, HV), y_meta], axis=0), ret_wo[j], h, mlp,
                          layer=i, final=i == 3, tm=MS)
        elif kind == 1:
            shp = jnp.concatenate([state_rwkv_shift_l1.astype(F32), jnp.zeros((N_META, D), F32)], axis=0)
            outs = _rwkv_proj_small(h, shp, g_mix, rwkv_wts, n_dec=NB)
            r_, lw_, k_, v_, kk_, b_, gate_, bonus_, xn_ = outs[:9]
            y_dec, wkv_dec = _wkv_dec(outs[9:], jnp.transpose(state_rwkv_wkv_l1.astype(F32), (1, 2, 3, 0)))
            wkv_dec = jnp.transpose(wkv_dec, (3, 0, 1, 2))
            y_meta, wkv_meta = _wkv_chunk([t[None] for t in (r_, lw_, k_, v_, kk_, b_)], jnp.zeros((1, WH, WN, WN), F32),
                                          nC=1, C=N_META, G=1, row_block0=meta_blk)
            y_meta = y_meta[0]
            shift_dec = xn_[:NB]
            shift_meta = xn_[MS - 1:MS]
            h = _rwkv_tail(jnp.concatenate([y_dec.T, y_meta], axis=0), bonus_, gate_, h, vec(rwkv_ln_g),
                           vec(rwkv_ln_b), wo_rwkv, eh, eht, mlp, layer=i, tm=MS)
        else:
            p = _norm_matmul(h, g_mix, (conv_wi,), tm=MS, tn=1024, out_dtype=F32)
            zpad = jnp.zeros((N_META, D), F32)
            b0 = jnp.concatenate([state_conv_l2[:, 0].astype(F32), zpad], axis=0)
            b1 = jnp.concatenate([state_conv_l2[:, 1].astype(F32), zpad], axis=0)
            h, u = _conv_small(p, b0, b1, conv_wf, conv_wo, h, mlp, layer=i, n_dec=NB)
            conv_dec = jnp.stack([state_conv_l2[:, 1].astype(F32), u[:NB]], axis=1)
            conv_meta = u[MS - 2:MS]
    y_sample = h[:NB].reshape(NB, 1, D)

    h = x_prompt.reshape(B * T, D)
    nC = T // RET_CHUNK
    ret_main = []
    for i in range(4):
        g_mix = norm_mix[i].reshape(1, D)
        kind = i % 3
        if kind == 0:
            j = i // 3
            p = _norm_matmul(h, g_mix, ret_wp[j], tm=512, tn=1024, out_dtype=BF16, rope=(cos_m, sin_m))
            y, s_fin = _ret_chunk(p, ret_meta[j], nb=B, nC=nC // RET_SUB, L=RET_CHUNK, nsub=RET_SUB, row_block0=0,
                                  out_dtype=BF16)
            ret_main.append(s_fin)
            h = _ret_tail(y, ret_wo[j], h, mlp, layer=i, final=i == 3, tm=512)
        elif kind == 1:
            outs = _rwkv_proj_main(h, shift_meta, g_mix, rwkv_wts, tm=256, seq=T)
            r_, lw_, k_, v_, kk_, b_, gate_, bonus_, shift_main = outs
            y, wkv_main = _wkv_chunk([t.reshape(B, T, D) for t in (r_, lw_, k_, v_, kk_, b_)], wkv_meta,
                                     nC=T // WKV_CHUNK, C=WKV_CHUNK, G=WKV_GROUP, row_block0=0)
            y = y.reshape(B * T, D)
            h = _rwkv_tail(y, bonus_, gate_, h, vec(rwkv_ln_g), vec(rwkv_ln_b), wo_rwkv, eh, eht, mlp, layer=i, tm=512)
        else:
            p = _norm_matmul(h, g_mix, (conv_wi,), tm=512, tn=1024, out_dtype=BF16)
            h, conv_main = _conv_main(p, conv_meta, conv_wf, conv_wo, h, mlp, layer=i, tm=512, seq=T)
    y_prompt = h.reshape(B, T, D)

    return (y_prompt, y_sample, ret_main[0], ret_dec_out[0], shift_main.reshape(B, D), shift_dec,
            wkv_main, wkv_dec, conv_main, conv_dec, ret_main[1], ret_dec_out[1])
```

```python
import functools
import math

import jax
import jax.numpy as jnp
from jax import lax
from jax.experimental import pallas as pl
from jax.experimental.pallas import tpu as pltpu

F32 = jnp.float32
BF16 = jnp.bfloat16

D = 1024
N_META = 16
RH = 4
DK = D // RH
DV = 2 * D // RH
HK = RH * DK
HV = RH * DV
RET_CHUNK = 128
RET_SUB = 4
ROPE_BASE = 10000.0
WH = 16
WN = 64
WKV_CHUNK = 64
INV_HI_LEVELS = 4
WKV_SUB = 2
WKV_GROUP = 2
D_FF = 4 * D
EPS = 1e-6
GN_EPS = 64e-5
LANES = 128
VMEM_LIMIT_V7X = 56 * 1024 * 1024

NT = (((1,), (1,)), ((), ()))
TN = (((0,), (0,)), ((), ()))


def _cp(n_axes):
    return pltpu.CompilerParams(dimension_semantics=("arbitrary",) * n_axes,
                                vmem_limit_bytes=VMEM_LIMIT_V7X)


def _dot(a, b):
    return jnp.dot(a, b, preferred_element_type=F32)


def _rms(x, g):
    return x * lax.rsqrt(jnp.mean(x * x, axis=-1, keepdims=True) + EPS) * g


def _sigmoid(x):
    return 1.0 / (1.0 + jnp.exp(-x))


def _split(x):
    hi = x.astype(BF16)
    lo = (x - hi.astype(F32)).astype(BF16)
    return hi, lo


def _seg_sum(x, eh, eht):
    return _dot(_dot(x.astype(BF16), eh).astype(BF16), eht)


def _resident(shape):
    return pl.BlockSpec(shape, lambda *idx: (0,) * len(shape), pipeline_mode=pl.Buffered(1))


def _norm_matmul_kernel(x_ref, g_ref, cos_ref, sin_ref, *refs, tn, rope):
    w_refs, o_ref = refs[:-1], refs[-1]
    xn = _rms(x_ref[...], g_ref[...]).astype(BF16)
    even = lax.broadcasted_iota(jnp.int32, (1, DK), 1) % 2 == 0
    col = 0
    for gi, w_ref in enumerate(w_refs):
        for j in range(w_ref.shape[1] // tn):
            acc = _dot(xn, w_ref[:, j * tn:(j + 1) * tn])
            if not (rope and gi == 0):
                o_ref[:, col:col + tn] = acc.astype(o_ref.dtype)
            else:
                c = cos_ref[...]
                s = sin_ref[...]
                scale = DK ** -0.5 if col >= HK else 1.0
                for hh in range(tn // DK):
                    x = acc[:, hh * DK:(hh + 1) * DK]
                    partner = jnp.where(even, pltpu.roll(x, DK - 1, 1), pltpu.roll(x, 1, 1))
                    o_ref[:, col + hh * DK:col + (hh + 1) * DK] = ((x * c + partner * s) * scale).astype(o_ref.dtype)
            col += tn


def _weight_spec(w):
    if not isinstance(w, tuple):
        return _resident(w.shape), w, w.shape[1]
    arr, layer, col0, width = w
    blk = col0 // width
    spec = pl.BlockSpec((None, arr.shape[1], width), lambda *idx: (layer, 0, blk), pipeline_mode=pl.Buffered(1))
    return spec, arr, width


def _norm_matmul(x, g, ws, *, tm, tn, out_dtype, rope=None):
    M, K = x.shape
    wspecs, wops, widths = zip(*[_weight_spec(w) for w in ws])
    N = sum(widths)
    if rope is None:
        cos = sin = jnp.zeros((8, DK), F32)
        cs_spec = pl.BlockSpec((8, DK), lambda i: (0, 0))
    else:
        cos, sin = rope
        nblk = cos.shape[0] // tm
        cs_spec = pl.BlockSpec((tm, DK), lambda i: (i % nblk, 0))
    return pl.pallas_call(
        functools.partial(_norm_matmul_kernel, tn=tn, rope=rope is not None),
        grid=(M // tm,),
        in_specs=[pl.BlockSpec((tm, K), lambda i: (i, 0)), _resident((1, K)), cs_spec, cs_spec] + list(wspecs),
        out_specs=pl.BlockSpec((tm, N), lambda i: (i, 0)),
        out_shape=jax.ShapeDtypeStruct((M, N), out_dtype),
        compiler_params=_cp(1),
        name="norm_matmul",
    )(x, g, cos, sin, *wops)


def _norm_matmul_t_kernel(x_ref, g_ref, w_ref, cos_ref, sin_ref, o_ref):
    xn = _rms(x_ref[...], g_ref[...]).astype(BF16)
    acc = lax.dot_general(w_ref[...], xn, (((0,), (1,)), ((), ())), preferred_element_type=F32)
    c = cos_ref[...]
    s = sin_ref[...]
    even = lax.broadcasted_iota(jnp.int32, (DK, 1), 0) % 2 == 0
    for hh in range(2 * RH):
        scale = 1.0 if hh < RH else DK ** -0.5
        x = acc[hh * DK:(hh + 1) * DK, :]
        partner = jnp.where(even, pltpu.roll(x, DK - 1, 0), pltpu.roll(x, 1, 0))
        o_ref[hh * DK:(hh + 1) * DK, :] = (x * c + partner * s) * scale


def _norm_matmul_t(x, g, w, cos_t, sin_t, *, rows):
    K = x.shape[1]
    wspec, wop, _ = _weight_spec(w)
    return pl.pallas_call(
        _norm_matmul_t_kernel,
        grid=(1,),
        in_specs=[pl.BlockSpec((rows, K), lambda i: (0, 0)),
                  pl.BlockSpec((1, K), lambda i: (0, 0)),
                  wspec,
                  pl.BlockSpec((DK, rows), lambda i: (0, 0)),
                  pl.BlockSpec((DK, rows), lambda i: (0, 0))],
        out_specs=pl.BlockSpec((2 * HK, rows), lambda i: (0, 0)),
        out_shape=jax.ShapeDtypeStruct((2 * HK, rows), F32),
        compiler_params=_cp(1),
        name="norm_matmul_t",
    )(x, g, wop, cos_t, sin_t)


MLP_TF = 512


def _mlp_tail(h1, g_ref, w1_ref, w2_ref, gf_ref, o_ref):
    xn = _rms(h1, g_ref[...]).astype(BF16)
    acc = None
    for f in range(D_FF // MLP_TF):
        a = _dot(xn, w1_ref[:, f * MLP_TF:(f + 1) * MLP_TF])
        a = jnp.square(jnp.maximum(a, 0.0)).astype(BF16)
        part = _dot(a, w2_ref[f * MLP_TF:(f + 1) * MLP_TF, :])
        acc = part if acc is None else acc + part
    out = h1 + acc
    o_ref[...] = out if gf_ref is None else _rms(out, gf_ref[...])


def _layer_resident(arr, layer):
    return pl.BlockSpec((None,) + arr.shape[1:], lambda *idx: (layer, 0, 0), pipeline_mode=pl.Buffered(1))


def _tail_specs(mlp, layer, final):
    g, w1, w2, gf = mlp
    specs = [_layer_resident(g, layer), _layer_resident(w1, layer), _layer_resident(w2, layer)]
    ops = [g, w1, w2]
    if final:
        specs.append(_resident(gf.shape))
        ops.append(gf)
    return specs, ops


def _ret_tail_kernel(a_ref, wo_ref, h_ref, g_ref, w1_ref, w2_ref, *rest):
    gf_ref, o_ref = (rest[0], rest[1]) if len(rest) == 2 else (None, rest[0])
    h1 = h_ref[...] + _dot(a_ref[...].astype(BF16), wo_ref[...])
    _mlp_tail(h1, g_ref, w1_ref, w2_ref, gf_ref, o_ref)


def _ret_tail(a, wo, h, mlp, *, layer, final, tm):
    M, K = a.shape
    tspecs, tops = _tail_specs(mlp, layer, final)
    row = pl.BlockSpec((tm, D), lambda i: (i, 0))
    return pl.pallas_call(
        _ret_tail_kernel,
        grid=(M // tm,),
        in_specs=[pl.BlockSpec((tm, K), lambda i: (i, 0)), _resident((K, D)), row] + tspecs,
        out_specs=row,
        out_shape=jax.ShapeDtypeStruct((M, D), F32),
        compiler_params=_cp(1),
        name="ret_tail",
    )(a, wo, h, *tops)


def _log_gamma(h):
    return math.log(1.0 - 2.0 ** (-5.0 - h))


def _ret_chunk_kernel(p_ref, s0_ref, y_ref, so_ref, s_ref, *, L, nsub, nC):
    c = pl.program_id(1)

    @pl.when(c == 0)
    def _():
        for h in range(RH):
            s_ref[h] = s0_ref[0, h]

    ti = lax.broadcasted_iota(jnp.int32, (L, L), 0)
    si = lax.broadcasted_iota(jnp.int32, (L, L), 1)
    diff = (ti - si).astype(F32)
    ri = lax.broadcasted_iota(jnp.int32, (L, 1), 0).astype(F32)
    for j in range(nsub):
        rows = slice(j * L, (j + 1) * L)
        for h in range(RH):
            lg = _log_gamma(h)
            mask = jnp.where(diff >= 0, jnp.exp(jnp.maximum(diff, 0.0) * lg), 0.0)
            qd = jnp.exp((ri + 1.0) * lg)
            kd = jnp.exp((L - 1.0 - ri) * lg)
            q = p_ref[rows, h * DK:(h + 1) * DK]
            k = p_ref[rows, HK + h * DK:HK + (h + 1) * DK]
            v = p_ref[rows, 2 * HK + h * DV:2 * HK + (h + 1) * DV].astype(BF16)
            g = p_ref[rows, 2 * HK + HV + h * DV:2 * HK + HV + (h + 1) * DV].astype(F32)
            sc = lax.dot_general(q.astype(BF16), k.astype(BF16), NT, preferred_element_type=F32) * mask
            inner = _dot(sc.astype(BF16), v)
            s_old = s_ref[h]
            cross = _dot((q.astype(F32) * qd).astype(BF16), s_old.astype(BF16))
            o = inner + cross
            s_ref[h] = s_old * math.exp(L * lg) + lax.dot_general(
                (k.astype(F32) * kd).astype(BF16), v, TN, preferred_element_type=F32)
            o = o * lax.rsqrt(jnp.mean(o * o, axis=-1, keepdims=True) + EPS)
            y_ref[rows, h * DV:(h + 1) * DV] = (g * _sigmoid(g) * o).astype(y_ref.dtype)

    @pl.when(c == nC - 1)
    def _():
        so_ref[0] = s_ref[...]


def _ret_chunk(p, s0, *, nb, nC, L, nsub, row_block0, out_dtype):
    shared = s0.shape[0] == 1
    R = L * nsub
    return pl.pallas_call(
        functools.partial(_ret_chunk_kernel, L=L, nsub=nsub, nC=nC),
        grid=(nb, nC),
        in_specs=[pl.BlockSpec((R, 2 * HK + 2 * HV), lambda b, c: (row_block0 + b * nC + c, 0)),
                  pl.BlockSpec((1, RH, DK, DV), lambda b, c: (0 if shared else b, 0, 0, 0))],
        out_specs=[pl.BlockSpec((R, HV), lambda b, c: (b * nC + c, 0)),
                   pl.BlockSpec((1, RH, DK, DV), lambda b, c: (b, 0, 0, 0))],
        out_shape=[jax.ShapeDtypeStruct((nb * nC * R, HV), out_dtype),
                   jax.ShapeDtypeStruct((nb, RH, DK, DV), F32)],
        scratch_shapes=[pltpu.VMEM((RH, DK, DV), F32)],
        compiler_params=_cp(2),
        name="ret_chunk",
    )(p, s0)


def _ret_dec_kernel(qkt_ref, p_ref, s_ref, y_ref, so_ref, *, bb, nbatch):
    step = pl.program_id(0)
    lane = lax.broadcasted_iota(jnp.int32, (1, nbatch), 1)

    def body(jb, carry):
        b = step * bb + jb
        onehot = (lane == b).astype(F32)
        rows = []
        for h in range(RH):
            gam = math.exp(_log_gamma(h))
            vrow = p_ref[pl.ds(b, 1), 2 * HK + h * DV:2 * HK + (h + 1) * DV]
            qcol = jnp.sum(qkt_ref[h * DK:(h + 1) * DK, :] * onehot, axis=1, keepdims=True)
            kcol = jnp.sum(qkt_ref[HK + h * DK:HK + (h + 1) * DK, :] * onehot, axis=1, keepdims=True)
            sn = gam * s_ref[jb, h] + kcol * vrow
            so_ref[jb, h] = sn
            o = jnp.sum(qcol * sn, axis=0, keepdims=True)
            o = o * lax.rsqrt(jnp.mean(o * o, axis=-1, keepdims=True) + EPS)
            g = p_ref[pl.ds(b, 1), 2 * HK + HV + h * DV:2 * HK + HV + (h + 1) * DV]
            rows.append(g * _sigmoid(g) * o)
        y_ref[b] = jnp.concatenate(rows, axis=1)
        return carry

    lax.fori_loop(0, bb, body, 0)


def _ret_dec(qkt, p, s, *, bb):
    nbatch = s.shape[0]
    return pl.pallas_call(
        functools.partial(_ret_dec_kernel, bb=bb, nbatch=nbatch),
        grid=(nbatch // bb,),
        in_specs=[pl.BlockSpec(qkt.shape, lambda i: (0, 0)),
                  pl.BlockSpec(p.shape, lambda i: (0, 0)),
                  pl.BlockSpec((bb, RH, DK, DV), lambda i: (i, 0, 0, 0))],
        out_specs=[pl.BlockSpec((nbatch, 1, HV), lambda i: (0, 0, 0)),
                   pl.BlockSpec((bb, RH, DK, DV), lambda i: (i, 0, 0, 0))],
        out_shape=[jax.ShapeDtypeStruct((nbatch, 1, HV), F32),
                   jax.ShapeDtypeStruct(s.shape, F32)],
        compiler_params=_cp(1),
        name="ret_dec",
    )(qkt, p, s)


N_RWKV_OUT = 8


def _rwkv_core(xn, xprev, w_refs, out_refs):
    (mix_ref, wrkv_ref, w0_ref, w1_ref, w2_ref, a0_ref, a1_ref, a2_ref, g1_ref, g2_ref,
     kk_ref, ka_ref, rk_ref, eh_ref, eht_ref) = w_refs
    r_o, lw_o, k_o, v_o, kk_o, b_o, g_o, bonus_o = out_refs
    xx = xprev - xn

    def xm(j):
        return (xn + xx * mix_ref[j:j + 1, :]).astype(BF16)

    r = _dot(xm(0), wrkv_ref[0])
    k = _dot(xm(1), wrkv_ref[1])
    v = _dot(xm(2), wrkv_ref[2])
    wl = _dot(jnp.tanh(_dot(xm(3), w1_ref[...])).astype(BF16), w2_ref[...])
    lw_o[...] = -math.exp(-0.5) * _sigmoid(w0_ref[...] + wl)
    al = _dot(_dot(xm(4), a1_ref[...]).astype(BF16), a2_ref[...])
    a = _sigmoid(a0_ref[...] + al)
    g_o[...] = _dot(_sigmoid(_dot(xm(5), g1_ref[...])).astype(BF16), g2_ref[...]).astype(g_o.dtype)
    kk = k * kk_ref[...]
    ssq = _seg_sum(kk * kk, eh_ref[...], eht_ref[...])
    kkn = kk / jnp.maximum(jnp.sqrt(ssq), 1e-12)
    k2 = k * (1.0 + (a - 1.0) * ka_ref[...])
    rk = _seg_sum(r * k2 * rk_ref[...], eh_ref[...], eht_ref[...])
    r_o[...] = r.astype(r_o.dtype)
    k_o[...] = k2.astype(k_o.dtype)
    v_o[...] = v.astype(v_o.dtype)
    kk_o[...] = kkn.astype(kk_o.dtype)
    b_o[...] = (kkn * a).astype(b_o.dtype)
    bonus_o[...] = (rk * v).astype(bonus_o.dtype)


def _rwkv_proj_main_kernel(h_ref, hp_ref, sh0_ref, g_ref, *refs, tm, tps):
    w_refs = refs[:15]
    out_refs = refs[15:15 + N_RWKV_OUT]
    sho_ref = refs[15 + N_RWKV_OUT]
    i = pl.program_id(0)
    xn = _rms(h_ref[...], g_ref[...])
    prevn = _rms(hp_ref[...], g_ref[...])[7:8, :]
    prev = jnp.where(i % tps == 0, sh0_ref[...], prevn)
    row = lax.broadcasted_iota(jnp.int32, (tm, 1), 0)
    xprev = jnp.where(row == 0, prev, pltpu.roll(xn, 1, 0))
    _rwkv_core(xn, xprev, w_refs, out_refs)
    sho_ref[0] = xn[tm - 1:tm, :]


def _rwkv_proj_small_kernel(h_ref, shp_ref, g_ref, *refs, n_dec):
    w_refs = refs[:15]
    out_refs = refs[15:15 + N_RWKV_OUT]
    xn_ref = refs[15 + N_RWKV_OUT]
    t_refs = refs[16 + N_RWKV_OUT:]
    m = h_ref.shape[0]
    xn = _rms(h_ref[...], g_ref[...])
    row = lax.broadcasted_iota(jnp.int32, (m, 1), 0)
    xprev = jnp.where(row <= n_dec, shp_ref[...], pltpu.roll(xn, 1, 0))
    _rwkv_core(xn, xprev, w_refs, out_refs)
    xn_ref[...] = xn
    for src, dst in zip(out_refs[:6], t_refs):
        dst[...] = src[0:n_dec, :].T


def _rwkv_weight_specs(wts):
    return [_resident(w.shape) for w in wts]


def _rwkv_proj_main(h, sh0, g, wts, *, tm, seq):
    M = h.shape[0]
    tps = seq // tm
    nseq = M // seq
    odt = [BF16, F32, BF16, BF16, BF16, BF16, BF16, BF16]
    row_spec = pl.BlockSpec((tm, D), lambda i: (i, 0))
    return pl.pallas_call(
        functools.partial(_rwkv_proj_main_kernel, tm=tm, tps=tps),
        grid=(M // tm,),
        in_specs=[row_spec,
                  pl.BlockSpec((8, D), lambda i: (jnp.maximum(i * (tm // 8) - 1, 0), 0)),
                  pl.BlockSpec((1, D), lambda i: (0, 0)),
                  pl.BlockSpec((1, D), lambda i: (0, 0))] + _rwkv_weight_specs(wts),
        out_specs=[row_spec] * N_RWKV_OUT + [pl.BlockSpec((1, 1, D), lambda i: (i // tps, 0, 0))],
        out_shape=[jax.ShapeDtypeStruct((M, D), dt) for dt in odt] + [jax.ShapeDtypeStruct((nseq, 1, D), F32)],
        compiler_params=_cp(1),
        name="rwkv_proj_main",
    )(h, h, sh0, g, *wts)


def _rwkv_proj_small(h, shp, g, wts, *, n_dec):
    M = h.shape[0]
    full = pl.BlockSpec((M, D), lambda i: (0, 0))
    return pl.pallas_call(
        functools.partial(_rwkv_proj_small_kernel, n_dec=n_dec),
        grid=(1,),
        in_specs=[full, full, pl.BlockSpec((1, D), lambda i: (0, 0))] + _rwkv_weight_specs(wts),
        out_specs=[full] * (N_RWKV_OUT + 1) + [pl.BlockSpec((D, n_dec), lambda i: (0, 0))] * 6,
        out_shape=[jax.ShapeDtypeStruct((M, D), F32)] * (N_RWKV_OUT + 1) + [jax.ShapeDtypeStruct((D, n_dec), F32)] * 6,
        compiler_params=_cp(1),
        name="rwkv_proj_small",
    )(h, shp, g, *wts)


def _mm(a, b, passes):
    if passes == 1:
        return _dot(a.astype(BF16), b.astype(BF16))
    ah, al = _split(a)
    bh, bl = _split(b)
    m = a.shape[0]
    r = _dot(jnp.concatenate([ah, al], axis=0), bh)
    return r[0:m] + r[m:2 * m] + _dot(ah, bl)


def _wkv_sweep(rows, r_ref, lw_ref, k_ref, v_ref, kk_ref, b_ref, y_ref, s_ref, *, C, G):
    ti = lax.broadcasted_iota(jnp.int32, (C, C), 0)
    si = lax.broadcasted_iota(jnp.int32, (C, C), 1)
    incl = si <= ti
    strict = si < ti
    tri = jnp.where(incl, 1.0, 0.0).astype(BF16)
    t2 = lax.broadcasted_iota(jnp.int32, (C, 2 * C), 0)
    s2 = lax.broadcasted_iota(jnp.int32, (C, 2 * C), 1)
    incl2 = jnp.where(s2 >= C, s2 - C, s2) <= t2

    at, rt, kt, bt, ke, be, wc, vb = [], [], [], [], [], [], [], []
    for g in range(G):
        lw = lw_ref[g, rows, :].astype(F32)
        l0 = lw.astype(BF16)
        rem = lw - l0.astype(F32)
        l1 = rem.astype(BF16)
        l2 = (rem - l1.astype(F32)).astype(BF16)
        cum = _dot(tri, l0) + _dot(tri, l1) + _dot(tri, l2)
        tot = cum[C - 1:C, :]
        w_in = jnp.exp(-cum)
        w_end = jnp.exp(tot - cum)
        kf = k_ref[g, rows, :].astype(F32)
        bf = b_ref[g, rows, :].astype(F32)
        rt.append((r_ref[g, rows, :].astype(F32) * jnp.exp(cum)).astype(BF16))
        at.append((-kk_ref[g, rows, :].astype(F32) * jnp.exp(cum - lw)).astype(BF16))
        kt.append((kf * w_in).astype(BF16))
        bt.append((bf * w_in).astype(BF16))
        ke.append((kf * w_end).astype(BF16))
        be.append((bf * w_end).astype(BF16))
        wc.append(jnp.exp(tot))
        vb.append(v_ref[g, rows, :].astype(BF16))

    items = [(g, h) for g in range(G) for h in range(WH)]
    n = range(len(items))
    sl = lambda h: slice(h * WN, (h + 1) * WN)
    s_old = [s_ref[g, h] for g, h in items]
    lhs = [jnp.concatenate([at[g][:, sl(h)], rt[g][:, sl(h)]], axis=0) for g, h in items]
    rhs = [jnp.concatenate([bt[g][:, sl(h)], kt[g][:, sl(h)], s_old[i].astype(BF16)], axis=0)
           for i, (g, h) in enumerate(items)]
    gm = [lax.dot_general(lhs[i], rhs[i], NT, preferred_element_type=F32) for i in n]
    lh = [m[:, 2 * C:2 * C + WN] for m in gm]
    vh = [vb[g][:, sl(h)] for g, h in items]
    a_ak = [jnp.where(strict, m[0:C, C:2 * C], 0.0).astype(BF16) for m in gm]
    a_r = [jnp.where(incl2, m[C:2 * C, 0:2 * C], 0.0).astype(BF16) for m in gm]
    cat = [jnp.concatenate([jnp.where(strict, gm[i][0:C, 0:C], 0.0), lh[i][0:C] + _dot(a_ak[i], vh[i])], axis=1)
           for i in n]
    keep_x = lax.broadcasted_iota(jnp.int32, (C, C + WN), 1) >= C
    nlev = int(math.log2(C))
    for lv in range(nlev - 1):
        passes = 3 if lv < INV_HI_LEVELS else 1
        cat = [_mm(cat[i][:, 0:C], cat[i], passes) + jnp.where(keep_x, cat[i], 0.0) for i in n]
    u = [cat[i][:, C:C + WN] + _mm(cat[i][:, 0:C], cat[i][:, C:C + WN], 1) for i in n]
    uv = [jnp.concatenate([u[i].astype(BF16), vh[i]], axis=0) for i in n]
    y = [lh[i][C:2 * C] + _dot(a_r[i], uv[i]) for i in n]
    for g in range(G):
        y_ref[g, rows, :] = jnp.concatenate(y[g * WH:(g + 1) * WH], axis=1)
    for i, (g, h) in enumerate(items):
        s_ref[g, h] = s_old[i] * wc[g][:, sl(h)] + lax.dot_general(
            uv[i], jnp.concatenate([be[g][:, sl(h)], ke[g][:, sl(h)]], axis=0), TN, preferred_element_type=F32)


def _wkv_chunk_kernel(r_ref, lw_ref, k_ref, v_ref, kk_ref, b_ref, s0_ref, y_ref, so_ref, s_ref, *, C, nsub, nC, G, shared):
    c = pl.program_id(1)

    @pl.when(c == 0)
    def _():
        for g in range(G):
            s_ref[g] = s0_ref[0 if shared else g]

    for j in range(nsub):
        _wkv_sweep(slice(j * C, (j + 1) * C), r_ref, lw_ref, k_ref, v_ref, kk_ref, b_ref, y_ref, s_ref, C=C, G=G)

    @pl.when(c == nC - 1)
    def _():
        so_ref[...] = s_ref[...]


def _wkv_chunk(ins, s0, *, nC, C, nsub, G, row_block0):
    nseq = ins[0].shape[0]
    shared = s0.shape[0] == 1
    R = C * nsub
    row_spec = pl.BlockSpec((G, R, D), lambda b, c: (b, row_block0 + c, 0))
    st_spec = pl.BlockSpec((G, WH, WN, WN), lambda b, c: (b, 0, 0, 0))
    s0_spec = pl.BlockSpec((1, WH, WN, WN), lambda b, c: (0, 0, 0, 0)) if shared else st_spec
    return pl.pallas_call(
        functools.partial(_wkv_chunk_kernel, C=C, nsub=nsub, nC=nC, G=G, shared=shared),
        grid=(nseq // G, nC),
        in_specs=[row_spec] * 6 + [s0_spec],
        out_specs=[pl.BlockSpec((G, R, D), lambda b, c: (b, c, 0)), st_spec],
        out_shape=[jax.ShapeDtypeStruct((nseq, nC * R, D), F32),
                   jax.ShapeDtypeStruct((nseq, WH, WN, WN), F32)],
        scratch_shapes=[pltpu.VMEM((G, WH, WN, WN), F32)],
        compiler_params=_cp(2),
        name="wkv_chunk",
    )(*ins, s0)


def _wkv_dec_kernel(r_ref, lw_ref, k_ref, v_ref, kk_ref, b_ref, s_ref, y_ref, so_ref):
    nkk = -kk_ref[...]
    w = jnp.exp(lw_ref[...])
    bb = b_ref[...]
    k2 = k_ref[...]
    rr = r_ref[...]
    for vi in range(WN):
        s_old = s_ref[0, vi]
        sa = jnp.sum(s_old * nkk, axis=0, keepdims=True)
        sn = s_old * w + sa * bb + v_ref[vi:vi + 1, :] * k2
        so_ref[0, vi] = sn
        y_ref[vi:vi + 1, :] = jnp.sum(sn * rr, axis=0, keepdims=True)


def _wkv_dec(ins, s):
    nbatch = s.shape[-1]
    vec = pl.BlockSpec((WN, nbatch), lambda h: (h, 0))
    st = pl.BlockSpec((1, WN, WN, nbatch), lambda h: (h, 0, 0, 0))
    return pl.pallas_call(
        _wkv_dec_kernel,
        grid=(WH,),
        in_specs=[vec] * 6 + [st],
        out_specs=[vec, st],
        out_shape=[jax.ShapeDtypeStruct((D, nbatch), F32), jax.ShapeDtypeStruct(s.shape, F32)],
        compiler_params=_cp(1),
        name="wkv_dec",
    )(*ins, s)


def _rwkv_tail_kernel(y_ref, bonus_ref, gate_ref, lng_ref, lnb_ref, wo_ref, eh_ref, eht_ref, h_ref,
                      g_ref, w1_ref, w2_ref, o_ref):
    y = y_ref[...]
    eh = eh_ref[...]
    eht = eht_ref[...]
    mu = _seg_sum(y, eh, eht) * (1.0 / WN)
    d = y - mu
    var = _seg_sum(d * d, eh, eht) * (1.0 / WN)
    yn = d * lax.rsqrt(var + GN_EPS) * lng_ref[...] + lnb_ref[...]
    z = (yn + bonus_ref[...].astype(F32)) * gate_ref[...].astype(F32)
    h1 = h_ref[...] + _dot(z.astype(BF16), wo_ref[...])
    _mlp_tail(h1, g_ref, w1_ref, w2_ref, None, o_ref)


def _rwkv_tail(y, bonus, gate, h, lng, lnb, wo, eh, eht, mlp, *, layer, tm):
    M = h.shape[0]
    tspecs, tops = _tail_specs(mlp, layer, False)
    row = pl.BlockSpec((tm, D), lambda i: (i, 0))
    return pl.pallas_call(
        _rwkv_tail_kernel,
        grid=(M // tm,),
        in_specs=[row, row, row, _resident((1, D)), _resident((1, D)), _resident((D, D)),
                  _resident(eh.shape), _resident(eht.shape), row] + tspecs,
        out_specs=row,
        out_shape=jax.ShapeDtypeStruct((M, D), F32),
        compiler_params=_cp(1),
        name="rwkv_tail",
    )(y, bonus, gate, lng, lnb, wo, eh, eht, h, *tops)


def _conv_main_kernel(p_ref, pp_ref, buf_ref, cw_ref, wo_ref, h_ref, g_ref, w1_ref, w2_ref, o_ref, nb_ref, *, tm, tps):
    i = pl.program_id(0)
    bq = p_ref[:, 0:D].astype(F32)
    u = p_ref[:, D:2 * D].astype(F32) * p_ref[:, 2 * D:3 * D].astype(F32)
    up = pp_ref[:, D:2 * D].astype(F32) * pp_ref[:, 2 * D:3 * D].astype(F32)
    npr = pp_ref.shape[0]
    first = i % tps == 0
    prev1 = jnp.where(first, buf_ref[1:2, :], up[npr - 1:npr, :])
    prev2 = jnp.where(first, buf_ref[0:1, :], up[npr - 2:npr - 1, :])
    row = lax.broadcasted_iota(jnp.int32, (tm, 1), 0)
    m1 = jnp.where(row == 0, prev1, pltpu.roll(u, 1, 0))
    m2 = jnp.where(row == 0, prev2, jnp.where(row == 1, prev1, pltpu.roll(u, 2, 0)))
    y = cw_ref[0:1, :] * m2 + cw_ref[1:2, :] * m1 + cw_ref[2:3, :] * u
    nb_ref[0] = u[tm - 2:tm, :]
    h1 = h_ref[...] + _dot((bq * y).astype(BF16), wo_ref[...])
    _mlp_tail(h1, g_ref, w1_ref, w2_ref, None, o_ref)


def _conv_main(p, buf, cw, wo, h, mlp, *, layer, tm, seq):
    M = h.shape[0]
    tps = seq // tm
    npr = 16
    tspecs, tops = _tail_specs(mlp, layer, False)
    return pl.pallas_call(
        functools.partial(_conv_main_kernel, tm=tm, tps=tps),
        grid=(M // tm,),
        in_specs=[pl.BlockSpec((tm, 3 * D), lambda i: (i, 0)),
                  pl.BlockSpec((npr, 3 * D), lambda i: (jnp.maximum(i * (tm // npr) - 1, 0), 0)),
                  _resident((2, D)), _resident((3, D)), _resident((D, D)),
                  pl.BlockSpec((tm, D), lambda i: (i, 0))] + tspecs,
        out_specs=[pl.BlockSpec((tm, D), lambda i: (i, 0)),
                   pl.BlockSpec((1, 2, D), lambda i: (i // tps, 0, 0))],
        out_shape=[jax.ShapeDtypeStruct((M, D), F32), jax.ShapeDtypeStruct((M // seq, 2, D), F32)],
        compiler_params=_cp(1),
        name="conv_main",
    )(p, p, buf, cw, wo, h, *tops)


def _conv_small_kernel(p_ref, b0_ref, b1_ref, cw_ref, wo_ref, h_ref, g_ref, w1_ref, w2_ref, o_ref, u_ref, *, n_dec):
    m = h_ref.shape[0]
    bq = p_ref[:, 0:D]
    u = p_ref[:, D:2 * D] * p_ref[:, 2 * D:3 * D]
    row = lax.broadcasted_iota(jnp.int32, (m, 1), 0)
    m1 = jnp.where(row <= n_dec, b1_ref[...], pltpu.roll(u, 1, 0))
    m2 = jnp.where(row <= n_dec + 1, b0_ref[...], pltpu.roll(u, 2, 0))
    y = cw_ref[0:1, :] * m2 + cw_ref[1:2, :] * m1 + cw_ref[2:3, :] * u
    u_ref[...] = u
    h1 = h_ref[...] + _dot((bq * y).astype(BF16), wo_ref[...])
    _mlp_tail(h1, g_ref, w1_ref, w2_ref, None, o_ref)


def _conv_small(p, b0, b1, cw, wo, h, mlp, *, layer, n_dec):
    M = h.shape[0]
    full = pl.BlockSpec((M, D), lambda i: (0, 0))
    tspecs, tops = _tail_specs(mlp, layer, False)
    return pl.pallas_call(
        functools.partial(_conv_small_kernel, n_dec=n_dec),
        grid=(1,),
        in_specs=[pl.BlockSpec((M, 3 * D), lambda i: (0, 0)), full, full,
                  pl.BlockSpec((3, D), lambda i: (0, 0)),
                  pl.BlockSpec((D, D), lambda i: (0, 0)), full] + tspecs,
        out_specs=[full, full],
        out_shape=[jax.ShapeDtypeStruct((M, D), F32)] * 2,
        compiler_params=_cp(1),
        name="conv_small",
    )(p, b0, b1, cw, wo, h, *tops)


def _rope_tables(pos):
    inv = 1.0 / (ROPE_BASE ** jnp.linspace(0.0, 1.0, DK // 2, dtype=F32))
    ang = pos.astype(F32)[:, None] * inv[None, :]
    return jnp.cos(ang), jnp.sin(ang)


def kernel(x_prompt, x_sample, state_ret_l0, state_rwkv_shift_l1, state_rwkv_wkv_l1, state_conv_l2, state_ret_l3,
           meta_tokens, norm_mix, norm_mlp, norm_final, ret_w_in, ret_w_out, rwkv_mix, rwkv_w_rkv, rwkv_w0, rwkv_w1,
           rwkv_w2, rwkv_a0, rwkv_a1, rwkv_a2, rwkv_g1, rwkv_g2, rwkv_k_k, rwkv_k_a, rwkv_r_k, rwkv_ln_g, rwkv_ln_b,
           rwkv_w_o, conv_w_in, conv_w, conv_w_out, mlp_w1, mlp_w2):
    B, T, _ = x_prompt.shape
    NB = x_sample.shape[0]
    MS = NB + N_META
    past_len = 16384

    ret_wb = ret_w_in.astype(BF16)
    ret_wp = [((ret_wb, j, 0, 2 * HK), (ret_wb, j, 2 * HK, 2 * HK), (ret_wb, j, 4 * HK, 2 * HK)) for j in range(2)]
    ret_wo = [ret_w_out[0].astype(BF16), ret_w_out[1].astype(BF16)]
    mlp = (norm_mlp.reshape(4, 1, D).astype(F32), mlp_w1.astype(BF16), mlp_w2.astype(BF16),
           norm_final.reshape(1, D).astype(F32))
    head_of_lane = jnp.arange(D) // WN
    eh = (head_of_lane[:, None] == jnp.arange(LANES)[None, :]).astype(BF16)
    eht = eh.T
    vec = lambda a: a.reshape(1, D).astype(F32)
    rwkv_wts = (rwkv_mix.astype(F32), rwkv_w_rkv.astype(BF16), vec(rwkv_w0), rwkv_w1.astype(BF16), rwkv_w2.astype(BF16),
                vec(rwkv_a0), rwkv_a1.astype(BF16), rwkv_a2.astype(BF16), rwkv_g1.astype(BF16), rwkv_g2.astype(BF16),
                vec(rwkv_k_k), vec(rwkv_k_a), vec(rwkv_r_k), eh, eht)
    wo_rwkv = rwkv_w_o.astype(BF16)
    conv_wi = conv_w_in.astype(BF16)
    conv_wo = conv_w_out.astype(BF16)
    conv_wf = conv_w.astype(F32)

    pair_sign = jnp.where(jnp.arange(DK) % 2 == 0, -1.0, 1.0).astype(F32)

    def rope_rows(pos):
        c, sn = _rope_tables(pos)
        return jnp.repeat(c, 2, axis=1), jnp.repeat(sn, 2, axis=1) * pair_sign[None, :]

    cos_m, sin_m = rope_rows(N_META + jnp.arange(T))
    pos_s = jnp.concatenate([jnp.full((NB,), past_len, jnp.int32), jnp.arange(N_META, dtype=jnp.int32)])
    cos_s, sin_s = rope_rows(pos_s)
    cos_t, sin_t = cos_s[:NB].T, sin_s[:NB].T

    h = jnp.concatenate([x_sample.reshape(NB, D), meta_tokens.astype(F32)], axis=0)
    meta_blk = NB // N_META
    zero_ret = jnp.zeros((1, RH, DK, DV), F32)
    ret_meta, ret_dec_out = [], []
    for i in range(4):
        g_mix = norm_mix[i].reshape(1, D)
        kind = i % 3
        if kind == 0:
            j = i // 3
            p = _norm_matmul(h, g_mix, ret_wp[j], tm=MS, tn=1024, out_dtype=F32, rope=(cos_s, sin_s))
            qkt = _norm_matmul_t(h, g_mix, ret_wp[j][0], cos_t, sin_t, rows=NB)
            s_in = (state_ret_l0 if j == 0 else state_ret_l3).astype(F32)
            y_dec, s_dec = _ret_dec(qkt, p, s_in, bb=4)
            y_meta, s_meta = _ret_chunk(p, zero_ret, nb=1, nC=1, L=N_META, nsub=1, row_block0=meta_blk, out_dtype=F32)
            ret_meta.append(s_meta)
            ret_dec_out.append(s_dec)
            h = _ret_tail(jnp.concatenate([y_dec.reshape(NB, HV), y_meta], axis=0), ret_wo[j], h, mlp,
                          layer=i, final=i == 3, tm=MS)
        elif kind == 1:
            shp = jnp.concatenate([state_rwkv_shift_l1.astype(F32), jnp.zeros((N_META, D), F32)], axis=0)
            outs = _rwkv_proj_small(h, shp, g_mix, rwkv_wts, n_dec=NB)
            r_, lw_, k_, v_, kk_, b_, gate_, bonus_, xn_ = outs[:9]
            y_dec, wkv_dec = _wkv_dec(outs[9:], jnp.transpose(state_rwkv_wkv_l1.astype(F32), (1, 2, 3, 0)))
            wkv_dec = jnp.transpose(wkv_dec, (3, 0, 1, 2))
            y_meta, wkv_meta = _wkv_chunk([t[None] for t in (r_, lw_, k_, v_, kk_, b_)], jnp.zeros((1, WH, WN, WN), F32),
                                          nC=1, C=N_META, nsub=1, G=1, row_block0=meta_blk)
            y_meta = y_meta[0]
            shift_dec = xn_[:NB]
            shift_meta = xn_[MS - 1:MS]
            h = _rwkv_tail(jnp.concatenate([y_dec.T, y_meta], axis=0), bonus_, gate_, h, vec(rwkv_ln_g),
                           vec(rwkv_ln_b), wo_rwkv, eh, eht, mlp, layer=i, tm=MS)
        else:
            p = _norm_matmul(h, g_mix, (conv_wi,), tm=MS, tn=1024, out_dtype=F32)
            zpad = jnp.zeros((N_META, D), F32)
            b0 = jnp.concatenate([state_conv_l2[:, 0].astype(F32), zpad], axis=0)
            b1 = jnp.concatenate([state_conv_l2[:, 1].astype(F32), zpad], axis=0)
            h, u = _conv_small(p, b0, b1, conv_wf, conv_wo, h, mlp, layer=i, n_dec=NB)
            conv_dec = jnp.stack([state_conv_l2[:, 1].astype(F32), u[:NB]], axis=1)
            conv_meta = u[MS - 2:MS]
    y_sample = h[:NB].reshape(NB, 1, D)

    h = x_prompt.reshape(B * T, D)
    nC = T // RET_CHUNK
    ret_main = []
    for i in range(4):
        g_mix = norm_mix[i].reshape(1, D)
        kind = i % 3
        if kind == 0:
            j = i // 3
            p = _norm_matmul(h, g_mix, ret_wp[j], tm=512, tn=1024, out_dtype=BF16, rope=(cos_m, sin_m))
            y, s_fin = _ret_chunk(p, ret_meta[j], nb=B, nC=nC // RET_SUB, L=RET_CHUNK, nsub=RET_SUB, row_block0=0,
                                  out_dtype=BF16)
            ret_main.append(s_fin)
            h = _ret_tail(y, ret_wo[j], h, mlp, layer=i, final=i == 3, tm=512)
        elif kind == 1:
            outs = _rwkv_proj_main(h, shift_meta, g_mix, rwkv_wts, tm=256, seq=T)
            r_, lw_, k_, v_, kk_, b_, gate_, bonus_, shift_main = outs
            y, wkv_main = _wkv_chunk([t.reshape(B, T, D) for t in (r_, lw_, k_, v_, kk_, b_)], wkv_meta,
                                     nC=T // (WKV_CHUNK * WKV_SUB), C=WKV_CHUNK, nsub=WKV_SUB, G=WKV_GROUP, row_block0=0)
            y = y.reshape(B * T, D)
            h = _rwkv_tail(y, bonus_, gate_, h, vec(rwkv_ln_g), vec(rwkv_ln_b), wo_rwkv, eh, eht, mlp, layer=i, tm=512)
        else:
            p = _norm_matmul(h, g_mix, (conv_wi,), tm=512, tn=1024, out_dtype=BF16)
            h, conv_main = _conv_main(p, conv_meta, conv_wf, conv_wo, h, mlp, layer=i, tm=512, seq=T)
    y_prompt = h.reshape(B, T, D)

    return (y_prompt, y_sample, ret_main[0], ret_dec_out[0], shift_main.reshape(B, D), shift_dec,
            wkv_main, wkv_dec, conv_main, conv_dec, ret_main[1], ret_dec_out[1])
```

```python
import functools
import math

import jax
import jax.numpy as jnp
from jax import lax
from jax.experimental import pallas as pl
from jax.experimental.pallas import tpu as pltpu

F32 = jnp.float32
BF16 = jnp.bfloat16

D = 1024
N_META = 16
RH = 4
DK = D // RH
DV = 2 * D // RH
HK = RH * DK
HV = RH * DV
RET_CHUNK = 128
RET_SUB = 4
ROPE_BASE = 10000.0
WH = 16
WN = 64
WKV_CHUNK = 64
INV_HI_LEVELS = 4
WKV_SUB = 2
WKV_GROUP = 2
D_FF = 4 * D
EPS = 1e-6
GN_EPS = 64e-5
LANES = 128
VMEM_LIMIT_V7X = 56 * 1024 * 1024

NT = (((1,), (1,)), ((), ()))
TN = (((0,), (0,)), ((), ()))


def _cp(n_axes):
    return pltpu.CompilerParams(dimension_semantics=("arbitrary",) * n_axes,
                                vmem_limit_bytes=VMEM_LIMIT_V7X)


def _dot(a, b):
    return jnp.dot(a, b, preferred_element_type=F32)


def _rms(x, g):
    return x * lax.rsqrt(jnp.mean(x * x, axis=-1, keepdims=True) + EPS) * g


def _sigmoid(x):
    return 1.0 / (1.0 + jnp.exp(-x))


def _split(x):
    hi = x.astype(BF16)
    lo = (x - hi.astype(F32)).astype(BF16)
    return hi, lo


def _seg_sum(x, eh, eht):
    return _dot(_dot(x.astype(BF16), eh).astype(BF16), eht)


def _resident(shape):
    return pl.BlockSpec(shape, lambda *idx: (0,) * len(shape), pipeline_mode=pl.Buffered(1))


def _norm_matmul_kernel(x_ref, g_ref, cos_ref, sin_ref, *refs, tn, rope):
    w_refs, o_ref = refs[:-1], refs[-1]
    xn = _rms(x_ref[...], g_ref[...]).astype(BF16)
    even = lax.broadcasted_iota(jnp.int32, (1, DK), 1) % 2 == 0
    col = 0
    for gi, w_ref in enumerate(w_refs):
        for j in range(w_ref.shape[1] // tn):
            acc = _dot(xn, w_ref[:, j * tn:(j + 1) * tn])
            if not (rope and gi == 0):
                o_ref[:, col:col + tn] = acc.astype(o_ref.dtype)
            else:
                c = cos_ref[...]
                s = sin_ref[...]
                scale = DK ** -0.5 if col >= HK else 1.0
                for hh in range(tn // DK):
                    x = acc[:, hh * DK:(hh + 1) * DK]
                    partner = jnp.where(even, pltpu.roll(x, DK - 1, 1), pltpu.roll(x, 1, 1))
                    o_ref[:, col + hh * DK:col + (hh + 1) * DK] = ((x * c + partner * s) * scale).astype(o_ref.dtype)
            col += tn


def _weight_spec(w):
    if not isinstance(w, tuple):
        return _resident(w.shape), w, w.shape[1]
    arr, layer, col0, width = w
    blk = col0 // width
    spec = pl.BlockSpec((None, arr.shape[1], width), lambda *idx: (layer, 0, blk), pipeline_mode=pl.Buffered(1))
    return spec, arr, width


def _norm_matmul(x, g, ws, *, tm, tn, out_dtype, rope=None):
    M, K = x.shape
    wspecs, wops, widths = zip(*[_weight_spec(w) for w in ws])
    N = sum(widths)
    if rope is None:
        cos = sin = jnp.zeros((8, DK), F32)
        cs_spec = pl.BlockSpec((8, DK), lambda i: (0, 0))
    else:
        cos, sin = rope
        nblk = cos.shape[0] // tm
        cs_spec = pl.BlockSpec((tm, DK), lambda i: (i % nblk, 0))
    return pl.pallas_call(
        functools.partial(_norm_matmul_kernel, tn=tn, rope=rope is not None),
        grid=(M // tm,),
        in_specs=[pl.BlockSpec((tm, K), lambda i: (i, 0)), _resident((1, K)), cs_spec, cs_spec] + list(wspecs),
        out_specs=pl.BlockSpec((tm, N), lambda i: (i, 0)),
        out_shape=jax.ShapeDtypeStruct((M, N), out_dtype),
        compiler_params=_cp(1),
        name="norm_matmul",
    )(x, g, cos, sin, *wops)


def _norm_matmul_t_kernel(x_ref, g_ref, w_ref, cos_ref, sin_ref, o_ref):
    xn = _rms(x_ref[...], g_ref[...]).astype(BF16)
    acc = lax.dot_general(w_ref[...], xn, (((0,), (1,)), ((), ())), preferred_element_type=F32)
    c = cos_ref[...]
    s = sin_ref[...]
    even = lax.broadcasted_iota(jnp.int32, (DK, 1), 0) % 2 == 0
    for hh in range(2 * RH):
        scale = 1.0 if hh < RH else DK ** -0.5
        x = acc[hh * DK:(hh + 1) * DK, :]
        partner = jnp.where(even, pltpu.roll(x, DK - 1, 0), pltpu.roll(x, 1, 0))
        o_ref[hh * DK:(hh + 1) * DK, :] = (x * c + partner * s) * scale


def _norm_matmul_t(x, g, w, cos_t, sin_t, *, rows):
    K = x.shape[1]
    wspec, wop, _ = _weight_spec(w)
    return pl.pallas_call(
        _norm_matmul_t_kernel,
        grid=(1,),
        in_specs=[pl.BlockSpec((rows, K), lambda i: (0, 0)),
                  pl.BlockSpec((1, K), lambda i: (0, 0)),
                  wspec,
                  pl.BlockSpec((DK, rows), lambda i: (0, 0)),
                  pl.BlockSpec((DK, rows), lambda i: (0, 0))],
        out_specs=pl.BlockSpec((2 * HK, rows), lambda i: (0, 0)),
        out_shape=jax.ShapeDtypeStruct((2 * HK, rows), F32),
        compiler_params=_cp(1),
        name="norm_matmul_t",
    )(x, g, wop, cos_t, sin_t)


MLP_TF = 512


def _mlp_tail(h1, g_ref, w1_ref, w2_ref, gf_ref, o_ref):
    xn = _rms(h1, g_ref[...]).astype(BF16)
    acc = None
    for f in range(D_FF // MLP_TF):
        a = _dot(xn, w1_ref[:, f * MLP_TF:(f + 1) * MLP_TF])
        a = jnp.square(jnp.maximum(a, 0.0)).astype(BF16)
        part = _dot(a, w2_ref[f * MLP_TF:(f + 1) * MLP_TF, :])
        acc = part if acc is None else acc + part
    out = h1 + acc
    o_ref[...] = out if gf_ref is None else _rms(out, gf_ref[...])


def _layer_resident(arr, layer):
    return pl.BlockSpec((None,) + arr.shape[1:], lambda *idx: (layer, 0, 0), pipeline_mode=pl.Buffered(1))


def _tail_specs(mlp, layer, final):
    g, w1, w2, gf = mlp
    specs = [_layer_resident(g, layer), _layer_resident(w1, layer), _layer_resident(w2, layer)]
    ops = [g, w1, w2]
    if final:
        specs.append(_resident(gf.shape))
        ops.append(gf)
    return specs, ops


def _ret_tail_kernel(a_ref, wo_ref, h_ref, g_ref, w1_ref, w2_ref, *rest):
    gf_ref, o_ref = (rest[0], rest[1]) if len(rest) == 2 else (None, rest[0])
    h1 = h_ref[...] + _dot(a_ref[...].astype(BF16), wo_ref[...])
    _mlp_tail(h1, g_ref, w1_ref, w2_ref, gf_ref, o_ref)


def _ret_tail(a, wo, h, mlp, *, layer, final, tm):
    M, K = a.shape
    tspecs, tops = _tail_specs(mlp, layer, final)
    row = pl.BlockSpec((tm, D), lambda i: (i, 0))
    return pl.pallas_call(
        _ret_tail_kernel,
        grid=(M // tm,),
        in_specs=[pl.BlockSpec((tm, K), lambda i: (i, 0)), _resident((K, D)), row] + tspecs,
        out_specs=row,
        out_shape=jax.ShapeDtypeStruct((M, D), F32),
        compiler_params=_cp(1),
        name="ret_tail",
    )(a, wo, h, *tops)


def _log_gamma(h):
    return math.log(1.0 - 2.0 ** (-5.0 - h))


def _ret_chunk_kernel(p_ref, s0_ref, y_ref, so_ref, s_ref, *, L, nsub, nC):
    c = pl.program_id(1)

    @pl.when(c == 0)
    def _():
        for h in range(RH):
            s_ref[h] = s0_ref[0, h]

    ti = lax.broadcasted_iota(jnp.int32, (L, L), 0)
    si = lax.broadcasted_iota(jnp.int32, (L, L), 1)
    diff = (ti - si).astype(F32)
    ri = lax.broadcasted_iota(jnp.int32, (L, 1), 0).astype(F32)
    for j in range(nsub):
        rows = slice(j * L, (j + 1) * L)
        for h in range(RH):
            lg = _log_gamma(h)
            mask = jnp.where(diff >= 0, jnp.exp(jnp.maximum(diff, 0.0) * lg), 0.0)
            qd = jnp.exp((ri + 1.0) * lg)
            kd = jnp.exp((L - 1.0 - ri) * lg)
            q = p_ref[rows, h * DK:(h + 1) * DK]
            k = p_ref[rows, HK + h * DK:HK + (h + 1) * DK]
            v = p_ref[rows, 2 * HK + h * DV:2 * HK + (h + 1) * DV].astype(BF16)
            g = p_ref[rows, 2 * HK + HV + h * DV:2 * HK + HV + (h + 1) * DV].astype(F32)
            sc = lax.dot_general(q.astype(BF16), k.astype(BF16), NT, preferred_element_type=F32) * mask
            inner = _dot(sc.astype(BF16), v)
            s_old = s_ref[h]
            cross = _dot((q.astype(F32) * qd).astype(BF16), s_old.astype(BF16))
            o = inner + cross
            s_ref[h] = s_old * math.exp(L * lg) + lax.dot_general(
                (k.astype(F32) * kd).astype(BF16), v, TN, preferred_element_type=F32)
            o = o * lax.rsqrt(jnp.mean(o * o, axis=-1, keepdims=True) + EPS)
            y_ref[rows, h * DV:(h + 1) * DV] = (g * _sigmoid(g) * o).astype(y_ref.dtype)

    @pl.when(c == nC - 1)
    def _():
        so_ref[0] = s_ref[...]


def _ret_chunk(p, s0, *, nb, nC, L, nsub, row_block0, out_dtype):
    shared = s0.shape[0] == 1
    R = L * nsub
    return pl.pallas_call(
        functools.partial(_ret_chunk_kernel, L=L, nsub=nsub, nC=nC),
        grid=(nb, nC),
        in_specs=[pl.BlockSpec((R, 2 * HK + 2 * HV), lambda b, c: (row_block0 + b * nC + c, 0)),
                  pl.BlockSpec((1, RH, DK, DV), lambda b, c: (0 if shared else b, 0, 0, 0))],
        out_specs=[pl.BlockSpec((R, HV), lambda b, c: (b * nC + c, 0)),
                   pl.BlockSpec((1, RH, DK, DV), lambda b, c: (b, 0, 0, 0))],
        out_shape=[jax.ShapeDtypeStruct((nb * nC * R, HV), out_dtype),
                   jax.ShapeDtypeStruct((nb, RH, DK, DV), F32)],
        scratch_shapes=[pltpu.VMEM((RH, DK, DV), F32)],
        compiler_params=_cp(2),
        name="ret_chunk",
    )(p, s0)


def _ret_dec_kernel(qkt_ref, p_ref, s_ref, y_ref, so_ref, *, bb, nbatch):
    step = pl.program_id(0)
    lane = lax.broadcasted_iota(jnp.int32, (1, nbatch), 1)

    def body(jb, carry):
        b = step * bb + jb
        onehot = (lane == b).astype(F32)
        rows = []
        for h in range(RH):
            gam = math.exp(_log_gamma(h))
            vrow = p_ref[pl.ds(b, 1), 2 * HK + h * DV:2 * HK + (h + 1) * DV]
            qcol = jnp.sum(qkt_ref[h * DK:(h + 1) * DK, :] * onehot, axis=1, keepdims=True)
            kcol = jnp.sum(qkt_ref[HK + h * DK:HK + (h + 1) * DK, :] * onehot, axis=1, keepdims=True)
            sn = gam * s_ref[jb, h] + kcol * vrow
            so_ref[jb, h] = sn
            o = jnp.sum(qcol * sn, axis=0, keepdims=True)
            o = o * lax.rsqrt(jnp.mean(o * o, axis=-1, keepdims=True) + EPS)
            g = p_ref[pl.ds(b, 1), 2 * HK + HV + h * DV:2 * HK + HV + (h + 1) * DV]
            rows.append(g * _sigmoid(g) * o)
        y_ref[b] = jnp.concatenate(rows, axis=1)
        return carry

    lax.fori_loop(0, bb, body, 0)


def _ret_dec(qkt, p, s, *, bb):
    nbatch = s.shape[0]
    return pl.pallas_call(
        functools.partial(_ret_dec_kernel, bb=bb, nbatch=nbatch),
        grid=(nbatch // bb,),
        in_specs=[pl.BlockSpec(qkt.shape, lambda i: (0, 0)),
                  pl.BlockSpec(p.shape, lambda i: (0, 0)),
                  pl.BlockSpec((bb, RH, DK, DV), lambda i: (i, 0, 0, 0))],
        out_specs=[pl.BlockSpec((nbatch, 1, HV), lambda i: (0, 0, 0)),
                   pl.BlockSpec((bb, RH, DK, DV), lambda i: (i, 0, 0, 0))],
        out_shape=[jax.ShapeDtypeStruct((nbatch, 1, HV), F32),
                   jax.ShapeDtypeStruct(s.shape, F32)],
        compiler_params=_cp(1),
        name="ret_dec",
    )(qkt, p, s)


N_RWKV_OUT = 8


def _rwkv_core(xn, xprev, w_refs, out_refs):
    (mix_ref, wrkv_ref, w0_ref, w1_ref, w2_ref, a0_ref, a1_ref, a2_ref, g1_ref, g2_ref,
     kk_ref, ka_ref, rk_ref, eh_ref, eht_ref) = w_refs
    r_o, lw_o, k_o, v_o, kk_o, b_o, g_o, bonus_o = out_refs
    xx = xprev - xn

    def xm(j):
        return (xn + xx * mix_ref[j:j + 1, :]).astype(BF16)

    r = _dot(xm(0), wrkv_ref[0])
    k = _dot(xm(1), wrkv_ref[1])
    v = _dot(xm(2), wrkv_ref[2])
    wl = _dot(jnp.tanh(_dot(xm(3), w1_ref[...])).astype(BF16), w2_ref[...])
    lw_o[...] = -math.exp(-0.5) * _sigmoid(w0_ref[...] + wl)
    al = _dot(_dot(xm(4), a1_ref[...]).astype(BF16), a2_ref[...])
    a = _sigmoid(a0_ref[...] + al)
    g_o[...] = _dot(_sigmoid(_dot(xm(5), g1_ref[...])).astype(BF16), g2_ref[...]).astype(g_o.dtype)
    kk = k * kk_ref[...]
    ssq = _seg_sum(kk * kk, eh_ref[...], eht_ref[...])
    kkn = kk / jnp.maximum(jnp.sqrt(ssq), 1e-12)
    k2 = k * (1.0 + (a - 1.0) * ka_ref[...])
    rk = _seg_sum(r * k2 * rk_ref[...], eh_ref[...], eht_ref[...])
    r_o[...] = r.astype(r_o.dtype)
    k_o[...] = k2.astype(k_o.dtype)
    v_o[...] = v.astype(v_o.dtype)
    kk_o[...] = kkn.astype(kk_o.dtype)
    b_o[...] = (kkn * a).astype(b_o.dtype)
    bonus_o[...] = (rk * v).astype(bonus_o.dtype)


def _rwkv_proj_main_kernel(h_ref, hp_ref, sh0_ref, g_ref, *refs, tm, tps):
    w_refs = refs[:15]
    out_refs = refs[15:15 + N_RWKV_OUT]
    sho_ref = refs[15 + N_RWKV_OUT]
    i = pl.program_id(0)
    xn = _rms(h_ref[...], g_ref[...])
    prevn = _rms(hp_ref[...], g_ref[...])[7:8, :]
    prev = jnp.where(i % tps == 0, sh0_ref[...], prevn)
    row = lax.broadcasted_iota(jnp.int32, (tm, 1), 0)
    xprev = jnp.where(row == 0, prev, pltpu.roll(xn, 1, 0))
    _rwkv_core(xn, xprev, w_refs, out_refs)
    sho_ref[0] = xn[tm - 1:tm, :]


def _rwkv_proj_small_kernel(h_ref, shp_ref, g_ref, *refs, n_dec):
    w_refs = refs[:15]
    out_refs = refs[15:15 + N_RWKV_OUT]
    xn_ref = refs[15 + N_RWKV_OUT]
    t_refs = refs[16 + N_RWKV_OUT:]
    m = h_ref.shape[0]
    xn = _rms(h_ref[...], g_ref[...])
    row = lax.broadcasted_iota(jnp.int32, (m, 1), 0)
    xprev = jnp.where(row <= n_dec, shp_ref[...], pltpu.roll(xn, 1, 0))
    _rwkv_core(xn, xprev, w_refs, out_refs)
    xn_ref[...] = xn
    for src, dst in zip(out_refs[:6], t_refs):
        dst[...] = src[0:n_dec, :].T


def _rwkv_weight_specs(wts):
    return [_resident(w.shape) for w in wts]


def _rwkv_proj_main(h, sh0, g, wts, *, tm, seq):
    M = h.shape[0]
    tps = seq // tm
    nseq = M // seq
    odt = [BF16, F32, BF16, BF16, BF16, BF16, BF16, BF16]
    row_spec = pl.BlockSpec((tm, D), lambda i: (i, 0))
    return pl.pallas_call(
        functools.partial(_rwkv_proj_main_kernel, tm=tm, tps=tps),
        grid=(M // tm,),
        in_specs=[row_spec,
                  pl.BlockSpec((8, D), lambda i: (jnp.maximum(i * (tm // 8) - 1, 0), 0)),
                  pl.BlockSpec((1, D), lambda i: (0, 0)),
                  pl.BlockSpec((1, D), lambda i: (0, 0))] + _rwkv_weight_specs(wts),
        out_specs=[row_spec] * N_RWKV_OUT + [pl.BlockSpec((1, 1, D), lambda i: (i // tps, 0, 0))],
        out_shape=[jax.ShapeDtypeStruct((M, D), dt) for dt in odt] + [jax.ShapeDtypeStruct((nseq, 1, D), F32)],
        compiler_params=_cp(1),
        name="rwkv_proj_main",
    )(h, h, sh0, g, *wts)


def _rwkv_proj_small(h, shp, g, wts, *, n_dec):
    M = h.shape[0]
    full = pl.BlockSpec((M, D), lambda i: (0, 0))
    return pl.pallas_call(
        functools.partial(_rwkv_proj_small_kernel, n_dec=n_dec),
        grid=(1,),
        in_specs=[full, full, pl.BlockSpec((1, D), lambda i: (0, 0))] + _rwkv_weight_specs(wts),
        out_specs=[full] * (N_RWKV_OUT + 1) + [pl.BlockSpec((D, n_dec), lambda i: (0, 0))] * 6,
        out_shape=[jax.ShapeDtypeStruct((M, D), F32)] * (N_RWKV_OUT + 1) + [jax.ShapeDtypeStruct((D, n_dec), F32)] * 6,
        compiler_params=_cp(1),
        name="rwkv_proj_small",
    )(h, shp, g, *wts)


def _chain1(cat, C):
    cb = cat.astype(BF16)
    return _dot(cb[:, 0:C], cb)


def _chain3(cat, C):
    hi, lo = _split(cat)
    r = _dot(jnp.concatenate([hi[:, 0:C], lo[:, 0:C]], axis=0), hi)
    return r[0:C] + r[C:2 * C] + _dot(hi[:, 0:C], lo)


def _wkv_sweep(rows, r_ref, lw_ref, k_ref, v_ref, kk_ref, b_ref, y_ref, s_ref, *, C, G):
    ti = lax.broadcasted_iota(jnp.int32, (C, C), 0)
    si = lax.broadcasted_iota(jnp.int32, (C, C), 1)
    incl = si <= ti
    strict = si < ti
    tri = jnp.where(incl, 1.0, 0.0).astype(BF16)
    t2 = lax.broadcasted_iota(jnp.int32, (C, 2 * C), 0)
    s2 = lax.broadcasted_iota(jnp.int32, (C, 2 * C), 1)
    incl2 = jnp.where(s2 >= C, s2 - C, s2) <= t2

    at, rt, kt, bt, ke, be, wc, vb = [], [], [], [], [], [], [], []
    for g in range(G):
        lw = lw_ref[g, rows, :].astype(F32)
        l0, l1 = _split(lw)
        cum = _dot(tri, l0) + _dot(tri, l1)
        tot = cum[C - 1:C, :]
        w_in = jnp.exp(-cum)
        w_end = jnp.exp(tot - cum)
        kf = k_ref[g, rows, :].astype(F32)
        bf = b_ref[g, rows, :].astype(F32)
        rt.append((r_ref[g, rows, :].astype(F32) * jnp.exp(cum)).astype(BF16))
        at.append((-kk_ref[g, rows, :].astype(F32) * jnp.exp(cum - lw)).astype(BF16))
        kt.append((kf * w_in).astype(BF16))
        bt.append((bf * w_in).astype(BF16))
        ke.append((kf * w_end).astype(BF16))
        be.append((bf * w_end).astype(BF16))
        wc.append(jnp.exp(tot))
        vb.append(v_ref[g, rows, :].astype(BF16))

    items = [(g, h) for g in range(G) for h in range(WH)]
    n = range(len(items))
    sl = lambda h: slice(h * WN, (h + 1) * WN)
    s_old = [s_ref[g, h] for g, h in items]
    lhs = [jnp.concatenate([at[g][:, sl(h)], rt[g][:, sl(h)]], axis=0) for g, h in items]
    rhs = [jnp.concatenate([bt[g][:, sl(h)], kt[g][:, sl(h)], s_old[i].astype(BF16)], axis=0)
           for i, (g, h) in enumerate(items)]
    gm = [lax.dot_general(lhs[i], rhs[i], NT, preferred_element_type=F32) for i in n]
    lh = [m[:, 2 * C:2 * C + WN] for m in gm]
    vh = [vb[g][:, sl(h)] for g, h in items]
    a_ak = [jnp.where(strict, m[0:C, C:2 * C], 0.0).astype(BF16) for m in gm]
    a_r = [jnp.where(incl2, m[C:2 * C, 0:2 * C], 0.0).astype(BF16) for m in gm]
    cat = [jnp.concatenate([jnp.where(strict, gm[i][0:C, 0:C], 0.0), lh[i][0:C] + _dot(a_ak[i], vh[i])], axis=1)
           for i in n]
    keep_x = lax.broadcasted_iota(jnp.int32, (C, C + WN), 1) >= C
    nlev = int(math.log2(C))
    for lv in range(nlev - 1):
        step = _chain3 if lv < INV_HI_LEVELS else _chain1
        cat = [step(cat[i], C) + jnp.where(keep_x, cat[i], 0.0) for i in n]
    u = [cat[i][:, C:C + WN] + _chain1(cat[i], C)[:, C:C + WN] for i in n]
    uv = [jnp.concatenate([u[i].astype(BF16), vh[i]], axis=0) for i in n]
    y = [lh[i][C:2 * C] + _dot(a_r[i], uv[i]) for i in n]
    for g in range(G):
        y_ref[g, rows, :] = jnp.concatenate(y[g * WH:(g + 1) * WH], axis=1)
    for i, (g, h) in enumerate(items):
        s_ref[g, h] = s_old[i] * wc[g][:, sl(h)] + lax.dot_general(
            uv[i], jnp.concatenate([be[g][:, sl(h)], ke[g][:, sl(h)]], axis=0), TN, preferred_element_type=F32)


def _wkv_chunk_kernel(r_ref, lw_ref, k_ref, v_ref, kk_ref, b_ref, s0_ref, y_ref, so_ref, s_ref, *, C, nsub, nC, G, shared):
    c = pl.program_id(1)

    @pl.when(c == 0)
    def _():
        for g in range(G):
            s_ref[g] = s0_ref[0 if shared else g]

    for j in range(nsub):
        _wkv_sweep(slice(j * C, (j + 1) * C), r_ref, lw_ref, k_ref, v_ref, kk_ref, b_ref, y_ref, s_ref, C=C, G=G)

    @pl.when(c == nC - 1)
    def _():
        so_ref[...] = s_ref[...]


def _wkv_chunk(ins, s0, *, nC, C, nsub, G, row_block0):
    nseq = ins[0].shape[0]
    shared = s0.shape[0] == 1
    R = C * nsub
    row_spec = pl.BlockSpec((G, R, D), lambda b, c: (b, row_block0 + c, 0))
    st_spec = pl.BlockSpec((G, WH, WN, WN), lambda b, c: (b, 0, 0, 0))
    s0_spec = pl.BlockSpec((1, WH, WN, WN), lambda b, c: (0, 0, 0, 0)) if shared else st_spec
    return pl.pallas_call(
        functools.partial(_wkv_chunk_kernel, C=C, nsub=nsub, nC=nC, G=G, shared=shared),
        grid=(nseq // G, nC),
        in_specs=[row_spec] * 6 + [s0_spec],
        out_specs=[pl.BlockSpec((G, R, D), lambda b, c: (b, c, 0)), st_spec],
        out_shape=[jax.ShapeDtypeStruct((nseq, nC * R, D), F32),
                   jax.ShapeDtypeStruct((nseq, WH, WN, WN), F32)],
        scratch_shapes=[pltpu.VMEM((G, WH, WN, WN), F32)],
        compiler_params=_cp(2),
        name="wkv_chunk",
    )(*ins, s0)


def _wkv_dec_kernel(r_ref, lw_ref, k_ref, v_ref, kk_ref, b_ref, s_ref, y_ref, so_ref):
    nkk = -kk_ref[...]
    w = jnp.exp(lw_ref[...])
    bb = b_ref[...]
    k2 = k_ref[...]
    rr = r_ref[...]
    for vi in range(WN):
        s_old = s_ref[0, vi]
        sa = jnp.sum(s_old * nkk, axis=0, keepdims=True)
        sn = s_old * w + sa * bb + v_ref[vi:vi + 1, :] * k2
        so_ref[0, vi] = sn
        y_ref[vi:vi + 1, :] = jnp.sum(sn * rr, axis=0, keepdims=True)


def _wkv_dec(ins, s):
    nbatch = s.shape[-1]
    vec = pl.BlockSpec((WN, nbatch), lambda h: (h, 0))
    st = pl.BlockSpec((1, WN, WN, nbatch), lambda h: (h, 0, 0, 0))
    return pl.pallas_call(
        _wkv_dec_kernel,
        grid=(WH,),
        in_specs=[vec] * 6 + [st],
        out_specs=[vec, st],
        out_shape=[jax.ShapeDtypeStruct((D, nbatch), F32), jax.ShapeDtypeStruct(s.shape, F32)],
        compiler_params=_cp(1),
        name="wkv_dec",
    )(*ins, s)


def _rwkv_tail_kernel(y_ref, bonus_ref, gate_ref, lng_ref, lnb_ref, wo_ref, eh_ref, eht_ref, h_ref,
                      g_ref, w1_ref, w2_ref, o_ref):
    y = y_ref[...]
    eh = eh_ref[...]
    eht = eht_ref[...]
    mu = _seg_sum(y, eh, eht) * (1.0 / WN)
    d = y - mu
    var = _seg_sum(d * d, eh, eht) * (1.0 / WN)
    yn = d * lax.rsqrt(var + GN_EPS) * lng_ref[...] + lnb_ref[...]
    z = (yn + bonus_ref[...].astype(F32)) * gate_ref[...].astype(F32)
    h1 = h_ref[...] + _dot(z.astype(BF16), wo_ref[...])
    _mlp_tail(h1, g_ref, w1_ref, w2_ref, None, o_ref)


def _rwkv_tail(y, bonus, gate, h, lng, lnb, wo, eh, eht, mlp, *, layer, tm):
    M = h.shape[0]
    tspecs, tops = _tail_specs(mlp, layer, False)
    row = pl.BlockSpec((tm, D), lambda i: (i, 0))
    return pl.pallas_call(
        _rwkv_tail_kernel,
        grid=(M // tm,),
        in_specs=[row, row, row, _resident((1, D)), _resident((1, D)), _resident((D, D)),
                  _resident(eh.shape), _resident(eht.shape), row] + tspecs,
        out_specs=row,
        out_shape=jax.ShapeDtypeStruct((M, D), F32),
        compiler_params=_cp(1),
        name="rwkv_tail",
    )(y, bonus, gate, lng, lnb, wo, eh, eht, h, *tops)


def _conv_main_kernel(p_ref, pp_ref, buf_ref, cw_ref, wo_ref, h_ref, g_ref, w1_ref, w2_ref, o_ref, nb_ref, *, tm, tps):
    i = pl.program_id(0)
    bq = p_ref[:, 0:D].astype(F32)
    u = p_ref[:, D:2 * D].astype(F32) * p_ref[:, 2 * D:3 * D].astype(F32)
    up = pp_ref[:, D:2 * D].astype(F32) * pp_ref[:, 2 * D:3 * D].astype(F32)
    npr = pp_ref.shape[0]
    first = i % tps == 0
    prev1 = jnp.where(first, buf_ref[1:2, :], up[npr - 1:npr, :])
    prev2 = jnp.where(first, buf_ref[0:1, :], up[npr - 2:npr - 1, :])
    row = lax.broadcasted_iota(jnp.int32, (tm, 1), 0)
    m1 = jnp.where(row == 0, prev1, pltpu.roll(u, 1, 0))
    m2 = jnp.where(row == 0, prev2, jnp.where(row == 1, prev1, pltpu.roll(u, 2, 0)))
    y = cw_ref[0:1, :] * m2 + cw_ref[1:2, :] * m1 + cw_ref[2:3, :] * u
    nb_ref[0] = u[tm - 2:tm, :]
    h1 = h_ref[...] + _dot((bq * y).astype(BF16), wo_ref[...])
    _mlp_tail(h1, g_ref, w1_ref, w2_ref, None, o_ref)


def _conv_main(p, buf, cw, wo, h, mlp, *, layer, tm, seq):
    M = h.shape[0]
    tps = seq // tm
    npr = 16
    tspecs, tops = _tail_specs(mlp, layer, False)
    return pl.pallas_call(
        functools.partial(_conv_main_kernel, tm=tm, tps=tps),
        grid=(M // tm,),
        in_specs=[pl.BlockSpec((tm, 3 * D), lambda i: (i, 0)),
                  pl.BlockSpec((npr, 3 * D), lambda i: (jnp.maximum(i * (tm // npr) - 1, 0), 0)),
                  _resident((2, D)), _resident((3, D)), _resident((D, D)),
                  pl.BlockSpec((tm, D), lambda i: (i, 0))] + tspecs,
        out_specs=[pl.BlockSpec((tm, D), lambda i: (i, 0)),
                   pl.BlockSpec((1, 2, D), lambda i: (i // tps, 0, 0))],
        out_shape=[jax.ShapeDtypeStruct((M, D), F32), jax.ShapeDtypeStruct((M // seq, 2, D), F32)],
        compiler_params=_cp(1),
        name="conv_main",
    )(p, p, buf, cw, wo, h, *tops)


def _conv_small_kernel(p_ref, b0_ref, b1_ref, cw_ref, wo_ref, h_ref, g_ref, w1_ref, w2_ref, o_ref, u_ref, *, n_dec):
    m = h_ref.shape[0]
    bq = p_ref[:, 0:D]
    u = p_ref[:, D:2 * D] * p_ref[:, 2 * D:3 * D]
    row = lax.broadcasted_iota(jnp.int32, (m, 1), 0)
    m1 = jnp.where(row <= n_dec, b1_ref[...], pltpu.roll(u, 1, 0))
    m2 = jnp.where(row <= n_dec + 1, b0_ref[...], pltpu.roll(u, 2, 0))
    y = cw_ref[0:1, :] * m2 + cw_ref[1:2, :] * m1 + cw_ref[2:3, :] * u
    u_ref[...] = u
    h1 = h_ref[...] + _dot((bq * y).astype(BF16), wo_ref[...])
    _mlp_tail(h1, g_ref, w1_ref, w2_ref, None, o_ref)


def _conv_small(p, b0, b1, cw, wo, h, mlp, *, layer, n_dec):
    M = h.shape[0]
    full = pl.BlockSpec((M, D), lambda i: (0, 0))
    tspecs, tops = _tail_specs(mlp, layer, False)
    return pl.pallas_call(
        functools.partial(_conv_small_kernel, n_dec=n_dec),
        grid=(1,),
        in_specs=[pl.BlockSpec((M, 3 * D), lambda i: (0, 0)), full, full,
                  pl.BlockSpec((3, D), lambda i: (0, 0)),
                  pl.BlockSpec((D, D), lambda i: (0, 0)), full] + tspecs,
        out_specs=[full, full],
        out_shape=[jax.ShapeDtypeStruct((M, D), F32)] * 2,
        compiler_params=_cp(1),
        name="conv_small",
    )(p, b0, b1, cw, wo, h, *tops)


def _rope_tables(pos):
    inv = 1.0 / (ROPE_BASE ** jnp.linspace(0.0, 1.0, DK // 2, dtype=F32))
    ang = pos.astype(F32)[:, None] * inv[None, :]
    return jnp.cos(ang), jnp.sin(ang)


def kernel(x_prompt, x_sample, state_ret_l0, state_rwkv_shift_l1, state_rwkv_wkv_l1, state_conv_l2, state_ret_l3,
           meta_tokens, norm_mix, norm_mlp, norm_final, ret_w_in, ret_w_out, rwkv_mix, rwkv_w_rkv, rwkv_w0, rwkv_w1,
           rwkv_w2, rwkv_a0, rwkv_a1, rwkv_a2, rwkv_g1, rwkv_g2, rwkv_k_k, rwkv_k_a, rwkv_r_k, rwkv_ln_g, rwkv_ln_b,
           rwkv_w_o, conv_w_in, conv_w, conv_w_out, mlp_w1, mlp_w2):
    B, T, _ = x_prompt.shape
    NB = x_sample.shape[0]
    MS = NB + N_META
    past_len = 16384

    ret_wb = ret_w_in.astype(BF16)
    ret_wp = [((ret_wb, j, 0, 2 * HK), (ret_wb, j, 2 * HK, 2 * HK), (ret_wb, j, 4 * HK, 2 * HK)) for j in range(2)]
    ret_wo = [ret_w_out[0].astype(BF16), ret_w_out[1].astype(BF16)]
    mlp = (norm_mlp.reshape(4, 1, D).astype(F32), mlp_w1.astype(BF16), mlp_w2.astype(BF16),
           norm_final.reshape(1, D).astype(F32))
    head_of_lane = jnp.arange(D) // WN
    eh = (head_of_lane[:, None] == jnp.arange(LANES)[None, :]).astype(BF16)
    eht = eh.T
    vec = lambda a: a.reshape(1, D).astype(F32)
    rwkv_wts = (rwkv_mix.astype(F32), rwkv_w_rkv.astype(BF16), vec(rwkv_w0), rwkv_w1.astype(BF16), rwkv_w2.astype(BF16),
                vec(rwkv_a0), rwkv_a1.astype(BF16), rwkv_a2.astype(BF16), rwkv_g1.astype(BF16), rwkv_g2.astype(BF16),
                vec(rwkv_k_k), vec(rwkv_k_a), vec(rwkv_r_k), eh, eht)
    wo_rwkv = rwkv_w_o.astype(BF16)
    conv_wi = conv_w_in.astype(BF16)
    conv_wo = conv_w_out.astype(BF16)
    conv_wf = conv_w.astype(F32)

    pair_sign = jnp.where(jnp.arange(DK) % 2 == 0, -1.0, 1.0).astype(F32)

    def rope_rows(pos):
        c, sn = _rope_tables(pos)
        return jnp.repeat(c, 2, axis=1), jnp.repeat(sn, 2, axis=1) * pair_sign[None, :]

    cos_m, sin_m = rope_rows(N_META + jnp.arange(T))
    pos_s = jnp.concatenate([jnp.full((NB,), past_len, jnp.int32), jnp.arange(N_META, dtype=jnp.int32)])
    cos_s, sin_s = rope_rows(pos_s)
    cos_t, sin_t = cos_s[:NB].T, sin_s[:NB].T

    h = jnp.concatenate([x_sample.reshape(NB, D), meta_tokens.astype(F32)], axis=0)
    meta_blk = NB // N_META
    zero_ret = jnp.zeros((1, RH, DK, DV), F32)
    ret_meta, ret_dec_out = [], []
    for i in range(4):
        g_mix = norm_mix[i].reshape(1, D)
        kind = i % 3
        if kind == 0:
            j = i // 3
            p = _norm_matmul(h, g_mix, ret_wp[j], tm=MS, tn=1024, out_dtype=F32, rope=(cos_s, sin_s))
            qkt = _norm_matmul_t(h, g_mix, ret_wp[j][0], cos_t, sin_t, rows=NB)
            s_in = (state_ret_l0 if j == 0 else state_ret_l3).astype(F32)
            y_dec, s_dec = _ret_dec(qkt, p, s_in, bb=4)
            y_meta, s_meta = _ret_chunk(p, zero_ret, nb=1, nC=1, L=N_META, nsub=1, row_block0=meta_blk, out_dtype=F32)
            ret_meta.append(s_meta)
            ret_dec_out.append(s_dec)
            h = _ret_tail(jnp.concatenate([y_dec.reshape(NB, HV), y_meta], axis=0), ret_wo[j], h, mlp,
                          layer=i, final=i == 3, tm=MS)
        elif kind == 1:
            shp = jnp.concatenate([state_rwkv_shift_l1.astype(F32), jnp.zeros((N_META, D), F32)], axis=0)
            outs = _rwkv_proj_small(h, shp, g_mix, rwkv_wts, n_dec=NB)
            r_, lw_, k_, v_, kk_, b_, gate_, bonus_, xn_ = outs[:9]
            y_dec, wkv_dec = _wkv_dec(outs[9:], jnp.transpose(state_rwkv_wkv_l1.astype(F32), (1, 2, 3, 0)))
            wkv_dec = jnp.transpose(wkv_dec, (3, 0, 1, 2))
            y_meta, wkv_meta = _wkv_chunk([t[None] for t in (r_, lw_, k_, v_, kk_, b_)], jnp.zeros((1, WH, WN, WN), F32),
                                          nC=1, C=N_META, nsub=1, G=1, row_block0=meta_blk)
            y_meta = y_meta[0]
            shift_dec = xn_[:NB]
            shift_meta = xn_[MS - 1:MS]
            h = _rwkv_tail(jnp.concatenate([y_dec.T, y_meta], axis=0), bonus_, gate_, h, vec(rwkv_ln_g),
                           vec(rwkv_ln_b), wo_rwkv, eh, eht, mlp, layer=i, tm=MS)
        else:
            p = _norm_matmul(h, g_mix, (conv_wi,), tm=MS, tn=1024, out_dtype=F32)
            zpad = jnp.zeros((N_META, D), F32)
            b0 = jnp.concatenate([state_conv_l2[:, 0].astype(F32), zpad], axis=0)
            b1 = jnp.concatenate([state_conv_l2[:, 1].astype(F32), zpad], axis=0)
            h, u = _conv_small(p, b0, b1, conv_wf, conv_wo, h, mlp, layer=i, n_dec=NB)
            conv_dec = jnp.stack([state_conv_l2[:, 1].astype(F32), u[:NB]], axis=1)
            conv_meta = u[MS - 2:MS]
    y_sample = h[:NB].reshape(NB, 1, D)

    h = x_prompt.reshape(B * T, D)
    nC = T // RET_CHUNK
    ret_main = []
    for i in range(4):
        g_mix = norm_mix[i].reshape(1, D)
        kind = i % 3
        if kind == 0:
            j = i // 3
            p = _norm_matmul(h, g_mix, ret_wp[j], tm=512, tn=1024, out_dtype=BF16, rope=(cos_m, sin_m))
            y, s_fin = _ret_chunk(p, ret_meta[j], nb=B, nC=nC // RET_SUB, L=RET_CHUNK, nsub=RET_SUB, row_block0=0,
                                  out_dtype=BF16)
            ret_main.append(s_fin)
            h = _ret_tail(y, ret_wo[j], h, mlp, layer=i, final=i == 3, tm=512)
        elif kind == 1:
            outs = _rwkv_proj_main(h, shift_meta, g_mix, rwkv_wts, tm=512, seq=T)
            r_, lw_, k_, v_, kk_, b_, gate_, bonus_, shift_main = outs
            y, wkv_main = _wkv_chunk([t.reshape(B, T, D) for t in (r_, lw_, k_, v_, kk_, b_)], wkv_meta,
                                     nC=T // (WKV_CHUNK * WKV_SUB), C=WKV_CHUNK, nsub=WKV_SUB, G=WKV_GROUP, row_block0=0)
            y = y.reshape(B * T, D)
            h = _rwkv_tail(y, bonus_, gate_, h, vec(rwkv_ln_g), vec(rwkv_ln_b), wo_rwkv, eh, eht, mlp, layer=i, tm=512)
        else:
            p = _norm_matmul(h, g_mix, (conv_wi,), tm=512, tn=1024, out_dtype=BF16)
            h, conv_main = _conv_main(p, conv_meta, conv_wf, conv_wo, h, mlp, layer=i, tm=512, seq=T)
    y_prompt = h.reshape(B, T, D)

    return (y_prompt, y_sample, ret_main[0], ret_dec_out[0], shift_main.reshape(B, D), shift_dec,
            wkv_main, wkv_dec, conv_main, conv_dec, ret_main[1], ret_dec_out[1])
```

```python
import functools
import math

import jax
import jax.numpy as jnp
from jax import lax
from jax.experimental import pallas as pl
from jax.experimental.pallas import tpu as pltpu

F32 = jnp.float32
BF16 = jnp.bfloat16

D = 1024
N_META = 16
RH = 4
DK = D // RH
DV = 2 * D // RH
HK = RH * DK
HV = RH * DV
RET_CHUNK = 128
RET_SUB = 4
ROPE_BASE = 10000.0
WH = 16
WN = 64
WKV_CHUNK = 64
INV_HI_LEVELS = 4
WKV_SUB = 2
WKV_GROUP = 2
D_FF = 4 * D
EPS = 1e-6
GN_EPS = 64e-5
LANES = 128
VMEM_LIMIT_V7X = 56 * 1024 * 1024

NT = (((1,), (1,)), ((), ()))
TN = (((0,), (0,)), ((), ()))


def _cp(n_axes):
    return pltpu.CompilerParams(dimension_semantics=("arbitrary",) * n_axes,
                                vmem_limit_bytes=VMEM_LIMIT_V7X)


def _dot(a, b):
    return jnp.dot(a, b, preferred_element_type=F32)


def _rms(x, g):
    return x * lax.rsqrt(jnp.mean(x * x, axis=-1, keepdims=True) + EPS) * g


def _sigmoid(x):
    return 1.0 / (1.0 + jnp.exp(-x))


def _split(x):
    hi = x.astype(BF16)
    lo = (x - hi.astype(F32)).astype(BF16)
    return hi, lo


def _seg_sum(x, eh, eht):
    return _dot(_dot(x.astype(BF16), eh).astype(BF16), eht)


def _resident(shape):
    return pl.BlockSpec(shape, lambda *idx: (0,) * len(shape), pipeline_mode=pl.Buffered(1))


def _norm_matmul_kernel(x_ref, g_ref, cos_ref, sin_ref, *refs, tn, rope):
    w_refs, o_ref = refs[:-1], refs[-1]
    xn = _rms(x_ref[...], g_ref[...]).astype(BF16)
    even = lax.broadcasted_iota(jnp.int32, (1, DK), 1) % 2 == 0
    col = 0
    for gi, w_ref in enumerate(w_refs):
        for j in range(w_ref.shape[1] // tn):
            acc = _dot(xn, w_ref[:, j * tn:(j + 1) * tn])
            if not (rope and gi == 0):
                o_ref[:, col:col + tn] = acc.astype(o_ref.dtype)
            else:
                c = cos_ref[...]
                s = sin_ref[...]
                scale = DK ** -0.5 if col >= HK else 1.0
                for hh in range(tn // DK):
                    x = acc[:, hh * DK:(hh + 1) * DK]
                    partner = jnp.where(even, pltpu.roll(x, DK - 1, 1), pltpu.roll(x, 1, 1))
                    o_ref[:, col + hh * DK:col + (hh + 1) * DK] = ((x * c + partner * s) * scale).astype(o_ref.dtype)
            col += tn


def _weight_spec(w):
    if not isinstance(w, tuple):
        return _resident(w.shape), w, w.shape[1]
    arr, layer, col0, width = w
    blk = col0 // width
    spec = pl.BlockSpec((None, arr.shape[1], width), lambda *idx: (layer, 0, blk), pipeline_mode=pl.Buffered(1))
    return spec, arr, width


def _norm_matmul(x, g, ws, *, tm, tn, out_dtype, rope=None):
    M, K = x.shape
    wspecs, wops, widths = zip(*[_weight_spec(w) for w in ws])
    N = sum(widths)
    if rope is None:
        cos = sin = jnp.zeros((8, DK), F32)
        cs_spec = pl.BlockSpec((8, DK), lambda i: (0, 0))
    else:
        cos, sin = rope
        nblk = cos.shape[0] // tm
        cs_spec = pl.BlockSpec((tm, DK), lambda i: (i % nblk, 0))
    return pl.pallas_call(
        functools.partial(_norm_matmul_kernel, tn=tn, rope=rope is not None),
        grid=(M // tm,),
        in_specs=[pl.BlockSpec((tm, K), lambda i: (i, 0)), _resident((1, K)), cs_spec, cs_spec] + list(wspecs),
        out_specs=pl.BlockSpec((tm, N), lambda i: (i, 0)),
        out_shape=jax.ShapeDtypeStruct((M, N), out_dtype),
        compiler_params=_cp(1),
        name="norm_matmul",
    )(x, g, cos, sin, *wops)


def _norm_matmul_t_kernel(x_ref, g_ref, w_ref, cos_ref, sin_ref, o_ref):
    xn = _rms(x_ref[...], g_ref[...]).astype(BF16)
    acc = lax.dot_general(w_ref[...], xn, (((0,), (1,)), ((), ())), preferred_element_type=F32)
    c = cos_ref[...]
    s = sin_ref[...]
    even = lax.broadcasted_iota(jnp.int32, (DK, 1), 0) % 2 == 0
    for hh in range(2 * RH):
        scale = 1.0 if hh < RH else DK ** -0.5
        x = acc[hh * DK:(hh + 1) * DK, :]
        partner = jnp.where(even, pltpu.roll(x, DK - 1, 0), pltpu.roll(x, 1, 0))
        o_ref[hh * DK:(hh + 1) * DK, :] = (x * c + partner * s) * scale


def _norm_matmul_t(x, g, w, cos_t, sin_t, *, rows):
    K = x.shape[1]
    wspec, wop, _ = _weight_spec(w)
    return pl.pallas_call(
        _norm_matmul_t_kernel,
        grid=(1,),
        in_specs=[pl.BlockSpec((rows, K), lambda i: (0, 0)),
                  pl.BlockSpec((1, K), lambda i: (0, 0)),
                  wspec,
                  pl.BlockSpec((DK, rows), lambda i: (0, 0)),
                  pl.BlockSpec((DK, rows), lambda i: (0, 0))],
        out_specs=pl.BlockSpec((2 * HK, rows), lambda i: (0, 0)),
        out_shape=jax.ShapeDtypeStruct((2 * HK, rows), F32),
        compiler_params=_cp(1),
        name="norm_matmul_t",
    )(x, g, wop, cos_t, sin_t)


MLP_TF = 512


def _mlp_tail(h1, g_ref, w1_ref, w2_ref, gf_ref, o_ref):
    xn = _rms(h1, g_ref[...]).astype(BF16)
    acc = None
    for f in range(D_FF // MLP_TF):
        a = _dot(xn, w1_ref[:, f * MLP_TF:(f + 1) * MLP_TF])
        a = jnp.square(jnp.maximum(a, 0.0)).astype(BF16)
        part = _dot(a, w2_ref[f * MLP_TF:(f + 1) * MLP_TF, :])
        acc = part if acc is None else acc + part
    out = h1 + acc
    o_ref[...] = out if gf_ref is None else _rms(out, gf_ref[...])


def _layer_resident(arr, layer):
    return pl.BlockSpec((None,) + arr.shape[1:], lambda *idx: (layer, 0, 0), pipeline_mode=pl.Buffered(1))


def _tail_specs(mlp, layer, final):
    g, w1, w2, gf = mlp
    specs = [_layer_resident(g, layer), _layer_resident(w1, layer), _layer_resident(w2, layer)]
    ops = [g, w1, w2]
    if final:
        specs.append(_resident(gf.shape))
        ops.append(gf)
    return specs, ops


def _ret_tail_kernel(a_ref, wo_ref, h_ref, g_ref, w1_ref, w2_ref, *rest):
    gf_ref, o_ref = (rest[0], rest[1]) if len(rest) == 2 else (None, rest[0])
    h1 = h_ref[...] + _dot(a_ref[...].astype(BF16), wo_ref[...])
    _mlp_tail(h1, g_ref, w1_ref, w2_ref, gf_ref, o_ref)


def _ret_tail(a, wo, h, mlp, *, layer, final, tm):
    M, K = a.shape
    tspecs, tops = _tail_specs(mlp, layer, final)
    row = pl.BlockSpec((tm, D), lambda i: (i, 0))
    return pl.pallas_call(
        _ret_tail_kernel,
        grid=(M // tm,),
        in_specs=[pl.BlockSpec((tm, K), lambda i: (i, 0)), _resident((K, D)), row] + tspecs,
        out_specs=row,
        out_shape=jax.ShapeDtypeStruct((M, D), F32),
        compiler_params=_cp(1),
        name="ret_tail",
    )(a, wo, h, *tops)


def _log_gamma(h):
    return math.log(1.0 - 2.0 ** (-5.0 - h))


def _ret_chunk_kernel(p_ref, s0_ref, y_ref, so_ref, s_ref, *, L, nsub, nC):
    c = pl.program_id(1)

    @pl.when(c == 0)
    def _():
        for h in range(RH):
            s_ref[h] = s0_ref[0, h]

    ti = lax.broadcasted_iota(jnp.int32, (L, L), 0)
    si = lax.broadcasted_iota(jnp.int32, (L, L), 1)
    diff = (ti - si).astype(F32)
    ri = lax.broadcasted_iota(jnp.int32, (L, 1), 0).astype(F32)
    for j in range(nsub):
        rows = slice(j * L, (j + 1) * L)
        for h in range(RH):
            lg = _log_gamma(h)
            mask = jnp.where(diff >= 0, jnp.exp(jnp.maximum(diff, 0.0) * lg), 0.0)
            qd = jnp.exp((ri + 1.0) * lg)
            kd = jnp.exp((L - 1.0 - ri) * lg)
            q = p_ref[rows, h * DK:(h + 1) * DK]
            k = p_ref[rows, HK + h * DK:HK + (h + 1) * DK]
            v = p_ref[rows, 2 * HK + h * DV:2 * HK + (h + 1) * DV].astype(BF16)
            g = p_ref[rows, 2 * HK + HV + h * DV:2 * HK + HV + (h + 1) * DV].astype(F32)
            sc = lax.dot_general(q.astype(BF16), k.astype(BF16), NT, preferred_element_type=F32) * mask
            inner = _dot(sc.astype(BF16), v)
            s_old = s_ref[h]
            cross = _dot((q.astype(F32) * qd).astype(BF16), s_old.astype(BF16))
            o = inner + cross
            s_ref[h] = s_old * math.exp(L * lg) + lax.dot_general(
                (k.astype(F32) * kd).astype(BF16), v, TN, preferred_element_type=F32)
            o = o * lax.rsqrt(jnp.mean(o * o, axis=-1, keepdims=True) + EPS)
            y_ref[rows, h * DV:(h + 1) * DV] = (g * _sigmoid(g) * o).astype(y_ref.dtype)

    @pl.when(c == nC - 1)
    def _():
        so_ref[0] = s_ref[...]


def _ret_chunk(p, s0, *, nb, nC, L, nsub, row_block0, out_dtype):
    shared = s0.shape[0] == 1
    R = L * nsub
    return pl.pallas_call(
        functools.partial(_ret_chunk_kernel, L=L, nsub=nsub, nC=nC),
        grid=(nb, nC),
        in_specs=[pl.BlockSpec((R, 2 * HK + 2 * HV), lambda b, c: (row_block0 + b * nC + c, 0)),
                  pl.BlockSpec((1, RH, DK, DV), lambda b, c: (0 if shared else b, 0, 0, 0))],
        out_specs=[pl.BlockSpec((R, HV), lambda b, c: (b * nC + c, 0)),
                   pl.BlockSpec((1, RH, DK, DV), lambda b, c: (b, 0, 0, 0))],
        out_shape=[jax.ShapeDtypeStruct((nb * nC * R, HV), out_dtype),
                   jax.ShapeDtypeStruct((nb, RH, DK, DV), F32)],
        scratch_shapes=[pltpu.VMEM((RH, DK, DV), F32)],
        compiler_params=_cp(2),
        name="ret_chunk",
    )(p, s0)


def _ret_dec_kernel(qkt_ref, p_ref, *refs, bb, nbatch):
    s_refs, y_ref, so_ref = refs[:RH], refs[RH], refs[RH + 1]
    step = pl.program_id(0)
    lane = lax.broadcasted_iota(jnp.int32, (1, nbatch), 1)

    def body(jb, carry):
        b = step * bb + jb
        onehot = (lane == b).astype(F32)
        rows = []
        for h in range(RH):
            gam = math.exp(_log_gamma(h))
            vrow = p_ref[pl.ds(b, 1), 2 * HK + h * DV:2 * HK + (h + 1) * DV]
            qcol = jnp.sum(qkt_ref[h * DK:(h + 1) * DK, :] * onehot, axis=1, keepdims=True)
            kcol = jnp.sum(qkt_ref[HK + h * DK:HK + (h + 1) * DK, :] * onehot, axis=1, keepdims=True)
            sn = gam * s_refs[h][jb, 0] + kcol * vrow
            so_ref[jb, h] = sn
            o = jnp.sum(qcol * sn, axis=0, keepdims=True)
            o = o * lax.rsqrt(jnp.mean(o * o, axis=-1, keepdims=True) + EPS)
            g = p_ref[pl.ds(b, 1), 2 * HK + HV + h * DV:2 * HK + HV + (h + 1) * DV]
            rows.append(g * _sigmoid(g) * o)
        y_ref[b] = jnp.concatenate(rows, axis=1)
        return carry

    lax.fori_loop(0, bb, body, 0)


def _ret_dec(qkt, p, s, *, bb):
    nbatch = s.shape[0]
    return pl.pallas_call(
        functools.partial(_ret_dec_kernel, bb=bb, nbatch=nbatch),
        grid=(nbatch // bb,),
        in_specs=[pl.BlockSpec(qkt.shape, lambda i: (0, 0)),
                  pl.BlockSpec(p.shape, lambda i: (0, 0))]
        + [pl.BlockSpec((bb, 1, DK, DV), functools.partial(lambda i, h: (i, h, 0, 0), h=h)) for h in range(RH)],
        out_specs=[pl.BlockSpec((nbatch, 1, HV), lambda i: (0, 0, 0)),
                   pl.BlockSpec((bb, RH, DK, DV), lambda i: (i, 0, 0, 0))],
        out_shape=[jax.ShapeDtypeStruct((nbatch, 1, HV), F32),
                   jax.ShapeDtypeStruct(s.shape, F32)],
        compiler_params=_cp(1),
        name="ret_dec",
    )(qkt, p, *([s] * RH))


N_RWKV_OUT = 8


def _rwkv_core(xn, xprev, w_refs, out_refs):
    (mix_ref, wrkv_ref, w0_ref, w1_ref, w2_ref, a0_ref, a1_ref, a2_ref, g1_ref, g2_ref,
     kk_ref, ka_ref, rk_ref, eh_ref, eht_ref) = w_refs
    r_o, lw_o, k_o, v_o, kk_o, b_o, g_o, bonus_o = out_refs
    xx = xprev - xn

    def xm(j):
        return (xn + xx * mix_ref[j:j + 1, :]).astype(BF16)

    r = _dot(xm(0), wrkv_ref[0])
    k = _dot(xm(1), wrkv_ref[1])
    v = _dot(xm(2), wrkv_ref[2])
    wl = _dot(jnp.tanh(_dot(xm(3), w1_ref[...])).astype(BF16), w2_ref[...])
    lw_o[...] = -math.exp(-0.5) * _sigmoid(w0_ref[...] + wl)
    al = _dot(_dot(xm(4), a1_ref[...]).astype(BF16), a2_ref[...])
    a = _sigmoid(a0_ref[...] + al)
    g_o[...] = _dot(_sigmoid(_dot(xm(5), g1_ref[...])).astype(BF16), g2_ref[...]).astype(g_o.dtype)
    kk = k * kk_ref[...]
    ssq = _seg_sum(kk * kk, eh_ref[...], eht_ref[...])
    kkn = kk / jnp.maximum(jnp.sqrt(ssq), 1e-12)
    k2 = k * (1.0 + (a - 1.0) * ka_ref[...])
    rk = _seg_sum(r * k2 * rk_ref[...], eh_ref[...], eht_ref[...])
    r_o[...] = r.astype(r_o.dtype)
    k_o[...] = k2.astype(k_o.dtype)
    v_o[...] = v.astype(v_o.dtype)
    kk_o[...] = kkn.astype(kk_o.dtype)
    b_o[...] = (kkn * a).astype(b_o.dtype)
    bonus_o[...] = (rk * v).astype(bonus_o.dtype)


def _rwkv_proj_main_kernel(h_ref, hp_ref, sh0_ref, g_ref, *refs, tm, tps):
    w_refs = refs[:15]
    out_refs = refs[15:15 + N_RWKV_OUT]
    sho_ref = refs[15 + N_RWKV_OUT]
    i = pl.program_id(0)
    xn = _rms(h_ref[...], g_ref[...])
    prevn = _rms(hp_ref[...], g_ref[...])[7:8, :]
    prev = jnp.where(i % tps == 0, sh0_ref[...], prevn)
    row = lax.broadcasted_iota(jnp.int32, (tm, 1), 0)
    xprev = jnp.where(row == 0, prev, pltpu.roll(xn, 1, 0))
    _rwkv_core(xn, xprev, w_refs, out_refs)
    sho_ref[0] = xn[tm - 1:tm, :]


def _rwkv_proj_small_kernel(h_ref, shp_ref, g_ref, *refs, n_dec):
    w_refs = refs[:15]
    out_refs = refs[15:15 + N_RWKV_OUT]
    xn_ref = refs[15 + N_RWKV_OUT]
    t_refs = refs[16 + N_RWKV_OUT:]
    m = h_ref.shape[0]
    xn = _rms(h_ref[...], g_ref[...])
    row = lax.broadcasted_iota(jnp.int32, (m, 1), 0)
    xprev = jnp.where(row <= n_dec, shp_ref[...], pltpu.roll(xn, 1, 0))
    _rwkv_core(xn, xprev, w_refs, out_refs)
    xn_ref[...] = xn
    for src, dst in zip(out_refs[:6], t_refs):
        dst[...] = src[0:n_dec, :].T


def _rwkv_weight_specs(wts):
    return [_resident(w.shape) for w in wts]


def _rwkv_proj_main(h, sh0, g, wts, *, tm, seq):
    M = h.shape[0]
    tps = seq // tm
    nseq = M // seq
    odt = [BF16, F32, BF16, BF16, BF16, BF16, BF16, BF16]
    row_spec = pl.BlockSpec((tm, D), lambda i: (i, 0))
    return pl.pallas_call(
        functools.partial(_rwkv_proj_main_kernel, tm=tm, tps=tps),
        grid=(M // tm,),
        in_specs=[row_spec,
                  pl.BlockSpec((8, D), lambda i: (jnp.maximum(i * (tm // 8) - 1, 0), 0)),
                  pl.BlockSpec((1, D), lambda i: (0, 0)),
                  pl.BlockSpec((1, D), lambda i: (0, 0))] + _rwkv_weight_specs(wts),
        out_specs=[row_spec] * N_RWKV_OUT + [pl.BlockSpec((1, 1, D), lambda i: (i // tps, 0, 0))],
        out_shape=[jax.ShapeDtypeStruct((M, D), dt) for dt in odt] + [jax.ShapeDtypeStruct((nseq, 1, D), F32)],
        compiler_params=_cp(1),
        name="rwkv_proj_main",
    )(h, h, sh0, g, *wts)


def _rwkv_proj_small(h, shp, g, wts, *, n_dec):
    M = h.shape[0]
    full = pl.BlockSpec((M, D), lambda i: (0, 0))
    return pl.pallas_call(
        functools.partial(_rwkv_proj_small_kernel, n_dec=n_dec),
        grid=(1,),
        in_specs=[full, full, pl.BlockSpec((1, D), lambda i: (0, 0))] + _rwkv_weight_specs(wts),
        out_specs=[full] * (N_RWKV_OUT + 1) + [pl.BlockSpec((D, n_dec), lambda i: (0, 0))] * 6,
        out_shape=[jax.ShapeDtypeStruct((M, D), F32)] * (N_RWKV_OUT + 1) + [jax.ShapeDtypeStruct((D, n_dec), F32)] * 6,
        compiler_params=_cp(1),
        name="rwkv_proj_small",
    )(h, shp, g, *wts)


def _chain1(cat, C):
    cb = cat.astype(BF16)
    return _dot(cb[:, 0:C], cb)


def _chain3(cat, C):
    hi, lo = _split(cat)
    r = _dot(jnp.concatenate([hi[:, 0:C], lo[:, 0:C]], axis=0), hi)
    return r[0:C] + r[C:2 * C] + _dot(hi[:, 0:C], lo)


def _wkv_sweep(rows, r_ref, lw_ref, k_ref, v_ref, kk_ref, b_ref, y_ref, s_ref, *, C, G):
    ti = lax.broadcasted_iota(jnp.int32, (C, C), 0)
    si = lax.broadcasted_iota(jnp.int32, (C, C), 1)
    incl = si <= ti
    strict = si < ti
    tri = jnp.where(incl, 1.0, 0.0).astype(BF16)
    t2 = lax.broadcasted_iota(jnp.int32, (C, 2 * C), 0)
    s2 = lax.broadcasted_iota(jnp.int32, (C, 2 * C), 1)
    incl2 = jnp.where(s2 >= C, s2 - C, s2) <= t2

    at, rt, kt, bt, ke, be, wc, vb = [], [], [], [], [], [], [], []
    for g in range(G):
        lw = lw_ref[g, rows, :].astype(F32)
        l0, l1 = _split(lw)
        cum = _dot(tri, l0) + _dot(tri, l1)
        tot = cum[C - 1:C, :]
        w_in = jnp.exp(-cum)
        w_end = jnp.exp(tot - cum)
        kf = k_ref[g, rows, :].astype(F32)
        bf = b_ref[g, rows, :].astype(F32)
        rt.append((r_ref[g, rows, :].astype(F32) * jnp.exp(cum)).astype(BF16))
        at.append((-kk_ref[g, rows, :].astype(F32) * jnp.exp(cum - lw)).astype(BF16))
        kt.append((kf * w_in).astype(BF16))
        bt.append((bf * w_in).astype(BF16))
        ke.append((kf * w_end).astype(BF16))
        be.append((bf * w_end).astype(BF16))
        wc.append(jnp.exp(tot))
        vb.append(v_ref[g, rows, :].astype(BF16))

    items = [(g, h) for g in range(G) for h in range(WH)]
    n = range(len(items))
    sl = lambda h: slice(h * WN, (h + 1) * WN)
    s_old = [s_ref[g, h] for g, h in items]
    lhs = [jnp.concatenate([at[g][:, sl(h)], rt[g][:, sl(h)]], axis=0) for g, h in items]
    rhs = [jnp.concatenate([bt[g][:, sl(h)], kt[g][:, sl(h)], s_old[i].astype(BF16)], axis=0)
           for i, (g, h) in enumerate(items)]
    gm = [lax.dot_general(lhs[i], rhs[i], NT, preferred_element_type=F32) for i in n]
    lh = [m[:, 2 * C:2 * C + WN] for m in gm]
    vh = [vb[g][:, sl(h)] for g, h in items]
    a_ak = [jnp.where(strict, m[0:C, C:2 * C], 0.0).astype(BF16) for m in gm]
    a_r = [jnp.where(incl2, m[C:2 * C, 0:2 * C], 0.0).astype(BF16) for m in gm]
    cat = [jnp.concatenate([jnp.where(strict, gm[i][0:C, 0:C], 0.0), lh[i][0:C] + _dot(a_ak[i], vh[i])], axis=1)
           for i in n]
    keep_x = lax.broadcasted_iota(jnp.int32, (C, C + WN), 1) >= C
    nlev = int(math.log2(C))
    for lv in range(nlev - 1):
        step = _chain3 if lv < INV_HI_LEVELS else _chain1
        cat = [step(cat[i], C) + jnp.where(keep_x, cat[i], 0.0) for i in n]
    u = [cat[i][:, C:C + WN] + _chain1(cat[i], C)[:, C:C + WN] for i in n]
    uv = [jnp.concatenate([u[i].astype(BF16), vh[i]], axis=0) for i in n]
    y = [lh[i][C:2 * C] + _dot(a_r[i], uv[i]) for i in n]
    for g in range(G):
        y_ref[g, rows, :] = jnp.concatenate(y[g * WH:(g + 1) * WH], axis=1)
    for i, (g, h) in enumerate(items):
        s_ref[g, h] = s_old[i] * wc[g][:, sl(h)] + lax.dot_general(
            uv[i], jnp.concatenate([be[g][:, sl(h)], ke[g][:, sl(h)]], axis=0), TN, preferred_element_type=F32)


def _wkv_chunk_kernel(r_ref, lw_ref, k_ref, v_ref, kk_ref, b_ref, s0_ref, y_ref, so_ref, s_ref, *, C, nsub, nC, G, shared):
    c = pl.program_id(1)

    @pl.when(c == 0)
    def _():
        for g in range(G):
            s_ref[g] = s0_ref[0 if shared else g]

    for j in range(nsub):
        _wkv_sweep(slice(j * C, (j + 1) * C), r_ref, lw_ref, k_ref, v_ref, kk_ref, b_ref, y_ref, s_ref, C=C, G=G)

    @pl.when(c == nC - 1)
    def _():
        so_ref[...] = s_ref[...]


def _wkv_chunk(ins, s0, *, nC, C, nsub, G, row_block0):
    nseq = ins[0].shape[0]
    shared = s0.shape[0] == 1
    R = C * nsub
    row_spec = pl.BlockSpec((G, R, D), lambda b, c: (b, row_block0 + c, 0))
    st_spec = pl.BlockSpec((G, WH, WN, WN), lambda b, c: (b, 0, 0, 0))
    s0_spec = pl.BlockSpec((1, WH, WN, WN), lambda b, c: (0, 0, 0, 0)) if shared else st_spec
    return pl.pallas_call(
        functools.partial(_wkv_chunk_kernel, C=C, nsub=nsub, nC=nC, G=G, shared=shared),
        grid=(nseq // G, nC),
        in_specs=[row_spec] * 6 + [s0_spec],
        out_specs=[pl.BlockSpec((G, R, D), lambda b, c: (b, c, 0)), st_spec],
        out_shape=[jax.ShapeDtypeStruct((nseq, nC * R, D), F32),
                   jax.ShapeDtypeStruct((nseq, WH, WN, WN), F32)],
        scratch_shapes=[pltpu.VMEM((G, WH, WN, WN), F32)],
        compiler_params=_cp(2),
        name="wkv_chunk",
    )(*ins, s0)


def _wkv_dec_kernel(r_ref, lw_ref, k_ref, v_ref, kk_ref, b_ref, s_ref, y_ref, so_ref):
    nkk = -kk_ref[...]
    w = jnp.exp(lw_ref[...])
    bb = b_ref[...]
    k2 = k_ref[...]
    rr = r_ref[...]
    for vi in range(WN):
        s_old = s_ref[0, vi]
        sa = jnp.sum(s_old * nkk, axis=0, keepdims=True)
        sn = s_old * w + sa * bb + v_ref[vi:vi + 1, :] * k2
        so_ref[0, vi] = sn
        y_ref[vi:vi + 1, :] = jnp.sum(sn * rr, axis=0, keepdims=True)


def _wkv_dec(ins, s):
    nbatch = s.shape[-1]
    vec = pl.BlockSpec((WN, nbatch), lambda h: (h, 0))
    st = pl.BlockSpec((1, WN, WN, nbatch), lambda h: (h, 0, 0, 0))
    return pl.pallas_call(
        _wkv_dec_kernel,
        grid=(WH,),
        in_specs=[vec] * 6 + [st],
        out_specs=[vec, st],
        out_shape=[jax.ShapeDtypeStruct((D, nbatch), F32), jax.ShapeDtypeStruct(s.shape, F32)],
        compiler_params=_cp(1),
        name="wkv_dec",
    )(*ins, s)


def _rwkv_tail_kernel(y_ref, bonus_ref, gate_ref, lng_ref, lnb_ref, wo_ref, eh_ref, eht_ref, h_ref,
                      g_ref, w1_ref, w2_ref, o_ref):
    y = y_ref[...]
    eh = eh_ref[...]
    eht = eht_ref[...]
    mu = _seg_sum(y, eh, eht) * (1.0 / WN)
    d = y - mu
    var = _seg_sum(d * d, eh, eht) * (1.0 / WN)
    yn = d * lax.rsqrt(var + GN_EPS) * lng_ref[...] + lnb_ref[...]
    z = (yn + bonus_ref[...].astype(F32)) * gate_ref[...].astype(F32)
    h1 = h_ref[...] + _dot(z.astype(BF16), wo_ref[...])
    _mlp_tail(h1, g_ref, w1_ref, w2_ref, None, o_ref)


def _rwkv_tail(y, bonus, gate, h, lng, lnb, wo, eh, eht, mlp, *, layer, tm):
    M = h.shape[0]
    tspecs, tops = _tail_specs(mlp, layer, False)
    row = pl.BlockSpec((tm, D), lambda i: (i, 0))
    return pl.pallas_call(
        _rwkv_tail_kernel,
        grid=(M // tm,),
        in_specs=[row, row, row, _resident((1, D)), _resident((1, D)), _resident((D, D)),
                  _resident(eh.shape), _resident(eht.shape), row] + tspecs,
        out_specs=row,
        out_shape=jax.ShapeDtypeStruct((M, D), F32),
        compiler_params=_cp(1),
        name="rwkv_tail",
    )(y, bonus, gate, lng, lnb, wo, eh, eht, h, *tops)


def _conv_main_kernel(p_ref, pp_ref, buf_ref, cw_ref, wo_ref, h_ref, g_ref, w1_ref, w2_ref, o_ref, nb_ref, *, tm, tps):
    i = pl.program_id(0)
    bq = p_ref[:, 0:D].astype(F32)
    u = p_ref[:, D:2 * D].astype(F32) * p_ref[:, 2 * D:3 * D].astype(F32)
    up = pp_ref[:, D:2 * D].astype(F32) * pp_ref[:, 2 * D:3 * D].astype(F32)
    npr = pp_ref.shape[0]
    first = i % tps == 0
    prev1 = jnp.where(first, buf_ref[1:2, :], up[npr - 1:npr, :])
    prev2 = jnp.where(first, buf_ref[0:1, :], up[npr - 2:npr - 1, :])
    row = lax.broadcasted_iota(jnp.int32, (tm, 1), 0)
    m1 = jnp.where(row == 0, prev1, pltpu.roll(u, 1, 0))
    m2 = jnp.where(row == 0, prev2, jnp.where(row == 1, prev1, pltpu.roll(u, 2, 0)))
    y = cw_ref[0:1, :] * m2 + cw_ref[1:2, :] * m1 + cw_ref[2:3, :] * u
    nb_ref[0] = u[tm - 2:tm, :]
    h1 = h_ref[...] + _dot((bq * y).astype(BF16), wo_ref[...])
    _mlp_tail(h1, g_ref, w1_ref, w2_ref, None, o_ref)


def _conv_main(p, buf, cw, wo, h, mlp, *, layer, tm, seq):
    M = h.shape[0]
    tps = seq // tm
    npr = 16
    tspecs, tops = _tail_specs(mlp, layer, False)
    return pl.pallas_call(
        functools.partial(_conv_main_kernel, tm=tm, tps=tps),
        grid=(M // tm,),
        in_specs=[pl.BlockSpec((tm, 3 * D), lambda i: (i, 0)),
                  pl.BlockSpec((npr, 3 * D), lambda i: (jnp.maximum(i * (tm // npr) - 1, 0), 0)),
                  _resident((2, D)), _resident((3, D)), _resident((D, D)),
                  pl.BlockSpec((tm, D), lambda i: (i, 0))] + tspecs,
        out_specs=[pl.BlockSpec((tm, D), lambda i: (i, 0)),
                   pl.BlockSpec((1, 2, D), lambda i: (i // tps, 0, 0))],
        out_shape=[jax.ShapeDtypeStruct((M, D), F32), jax.ShapeDtypeStruct((M // seq, 2, D), F32)],
        compiler_params=_cp(1),
        name="conv_main",
    )(p, p, buf, cw, wo, h, *tops)


def _conv_small_kernel(p_ref, b0_ref, b1_ref, cw_ref, wo_ref, h_ref, g_ref, w1_ref, w2_ref, o_ref, u_ref, *, n_dec):
    m = h_ref.shape[0]
    bq = p_ref[:, 0:D]
    u = p_ref[:, D:2 * D] * p_ref[:, 2 * D:3 * D]
    row = lax.broadcasted_iota(jnp.int32, (m, 1), 0)
    m1 = jnp.where(row <= n_dec, b1_ref[...], pltpu.roll(u, 1, 0))
    m2 = jnp.where(row <= n_dec + 1, b0_ref[...], pltpu.roll(u, 2, 0))
    y = cw_ref[0:1, :] * m2 + cw_ref[1:2, :] * m1 + cw_ref[2:3, :] * u
    u_ref[...] = u
    h1 = h_ref[...] + _dot((bq * y).astype(BF16), wo_ref[...])
    _mlp_tail(h1, g_ref, w1_ref, w2_ref, None, o_ref)


def _conv_small(p, b0, b1, cw, wo, h, mlp, *, layer, n_dec):
    M = h.shape[0]
    full = pl.BlockSpec((M, D), lambda i: (0, 0))
    tspecs, tops = _tail_specs(mlp, layer, False)
    return pl.pallas_call(
        functools.partial(_conv_small_kernel, n_dec=n_dec),
        grid=(1,),
        in_specs=[pl.BlockSpec((M, 3 * D), lambda i: (0, 0)), full, full,
                  pl.BlockSpec((3, D), lambda i: (0, 0)),
                  pl.BlockSpec((D, D), lambda i: (0, 0)), full] + tspecs,
        out_specs=[full, full],
        out_shape=[jax.ShapeDtypeStruct((M, D), F32)] * 2,
        compiler_params=_cp(1),
        name="conv_small",
    )(p, b0, b1, cw, wo, h, *tops)


def _rope_tables(pos):
    inv = 1.0 / (ROPE_BASE ** jnp.linspace(0.0, 1.0, DK // 2, dtype=F32))
    ang = pos.astype(F32)[:, None] * inv[None, :]
    return jnp.cos(ang), jnp.sin(ang)


def kernel(x_prompt, x_sample, state_ret_l0, state_rwkv_shift_l1, state_rwkv_wkv_l1, state_conv_l2, state_ret_l3,
           meta_tokens, norm_mix, norm_mlp, norm_final, ret_w_in, ret_w_out, rwkv_mix, rwkv_w_rkv, rwkv_w0, rwkv_w1,
           rwkv_w2, rwkv_a0, rwkv_a1, rwkv_a2, rwkv_g1, rwkv_g2, rwkv_k_k, rwkv_k_a, rwkv_r_k, rwkv_ln_g, rwkv_ln_b,
           rwkv_w_o, conv_w_in, conv_w, conv_w_out, mlp_w1, mlp_w2):
    B, T, _ = x_prompt.shape
    NB = x_sample.shape[0]
    MS = NB + N_META
    past_len = 16384

    ret_wb = ret_w_in.astype(BF16)
    ret_wp = [((ret_wb, j, 0, 2 * HK), (ret_wb, j, 2 * HK, 2 * HK), (ret_wb, j, 4 * HK, 2 * HK)) for j in range(2)]
    ret_wo = [ret_w_out[0].astype(BF16), ret_w_out[1].astype(BF16)]
    mlp = (norm_mlp.reshape(4, 1, D).astype(F32), mlp_w1.astype(BF16), mlp_w2.astype(BF16),
           norm_final.reshape(1, D).astype(F32))
    head_of_lane = jnp.arange(D) // WN
    eh = (head_of_lane[:, None] == jnp.arange(LANES)[None, :]).astype(BF16)
    eht = eh.T
    vec = lambda a: a.reshape(1, D).astype(F32)
    rwkv_wts = (rwkv_mix.astype(F32), rwkv_w_rkv.astype(BF16), vec(rwkv_w0), rwkv_w1.astype(BF16), rwkv_w2.astype(BF16),
                vec(rwkv_a0), rwkv_a1.astype(BF16), rwkv_a2.astype(BF16), rwkv_g1.astype(BF16), rwkv_g2.astype(BF16),
                vec(rwkv_k_k), vec(rwkv_k_a), vec(rwkv_r_k), eh, eht)
    wo_rwkv = rwkv_w_o.astype(BF16)
    conv_wi = conv_w_in.astype(BF16)
    conv_wo = conv_w_out.astype(BF16)
    conv_wf = conv_w.astype(F32)

    pair_sign = jnp.where(jnp.arange(DK) % 2 == 0, -1.0, 1.0).astype(F32)

    def rope_rows(pos):
        c, sn = _rope_tables(pos)
        return jnp.repeat(c, 2, axis=1), jnp.repeat(sn, 2, axis=1) * pair_sign[None, :]

    cos_m, sin_m = rope_rows(N_META + jnp.arange(T))
    pos_s = jnp.concatenate([jnp.full((NB,), past_len, jnp.int32), jnp.arange(N_META, dtype=jnp.int32)])
    cos_s, sin_s = rope_rows(pos_s)
    cos_t, sin_t = cos_s[:NB].T, sin_s[:NB].T

    h = jnp.concatenate([x_sample.reshape(NB, D), meta_tokens.astype(F32)], axis=0)
    meta_blk = NB // N_META
    zero_ret = jnp.zeros((1, RH, DK, DV), F32)
    ret_meta, ret_dec_out = [], []
    for i in range(4):
        g_mix = norm_mix[i].reshape(1, D)
        kind = i % 3
        if kind == 0:
            j = i // 3
            p = _norm_matmul(h, g_mix, ret_wp[j], tm=MS, tn=1024, out_dtype=F32, rope=(cos_s, sin_s))
            qkt = _norm_matmul_t(h, g_mix, ret_wp[j][0], cos_t, sin_t, rows=NB)
            s_in = (state_ret_l0 if j == 0 else state_ret_l3).astype(F32)
            y_dec, s_dec = _ret_dec(qkt, p, s_in, bb=4)
            y_meta, s_meta = _ret_chunk(p, zero_ret, nb=1, nC=1, L=N_META, nsub=1, row_block0=meta_blk, out_dtype=F32)
            ret_meta.append(s_meta)
            ret_dec_out.append(s_dec)
            h = _ret_tail(jnp.concatenate([y_dec.reshape(NB, HV), y_meta], axis=0), ret_wo[j], h, mlp,
                          layer=i, final=i == 3, tm=MS)
        elif kind == 1:
            shp = jnp.concatenate([state_rwkv_shift_l1.astype(F32), jnp.zeros((N_META, D), F32)], axis=0)
            outs = _rwkv_proj_small(h, shp, g_mix, rwkv_wts, n_dec=NB)
            r_, lw_, k_, v_, kk_, b_, gate_, bonus_, xn_ = outs[:9]
            y_dec, wkv_dec = _wkv_dec(outs[9:], jnp.transpose(state_rwkv_wkv_l1.astype(F32), (1, 2, 3, 0)))
            wkv_dec = jnp.transpose(wkv_dec, (3, 0, 1, 2))
            y_meta, wkv_meta = _wkv_chunk([t[None] for t in (r_, lw_, k_, v_, kk_, b_)], jnp.zeros((1, WH, WN, WN), F32),
                                          nC=1, C=N_META, nsub=1, G=1, row_block0=meta_blk)
            y_meta = y_meta[0]
            shift_dec = xn_[:NB]
            shift_meta = xn_[MS - 1:MS]
            h = _rwkv_tail(jnp.concatenate([y_dec.T, y_meta], axis=0), bonus_, gate_, h, vec(rwkv_ln_g),
                           vec(rwkv_ln_b), wo_rwkv, eh, eht, mlp, layer=i, tm=MS)
        else:
            p = _norm_matmul(h, g_mix, (conv_wi,), tm=MS, tn=1024, out_dtype=F32)
            zpad = jnp.zeros((N_META, D), F32)
            b0 = jnp.concatenate([state_conv_l2[:, 0].astype(F32), zpad], axis=0)
            b1 = jnp.concatenate([state_conv_l2[:, 1].astype(F32), zpad], axis=0)
            h, u = _conv_small(p, b0, b1, conv_wf, conv_wo, h, mlp, layer=i, n_dec=NB)
            conv_dec = jnp.stack([state_conv_l2[:, 1].astype(F32), u[:NB]], axis=1)
            conv_meta = u[MS - 2:MS]
    y_sample = h[:NB].reshape(NB, 1, D)

    h = x_prompt.reshape(B * T, D)
    nC = T // RET_CHUNK
    ret_main = []
    for i in range(4):
        g_mix = norm_mix[i].reshape(1, D)
        kind = i % 3
        if kind == 0:
            j = i // 3
            p = _norm_matmul(h, g_mix, ret_wp[j], tm=512, tn=1024, out_dtype=BF16, rope=(cos_m, sin_m))
            y, s_fin = _ret_chunk(p, ret_meta[j], nb=B, nC=nC // RET_SUB, L=RET_CHUNK, nsub=RET_SUB, row_block0=0,
                                  out_dtype=BF16)
            ret_main.append(s_fin)
            h = _ret_tail(y, ret_wo[j], h, mlp, layer=i, final=i == 3, tm=512)
        elif kind == 1:
            outs = _rwkv_proj_main(h, shift_meta, g_mix, rwkv_wts, tm=512, seq=T)
            r_, lw_, k_, v_, kk_, b_, gate_, bonus_, shift_main = outs
            y, wkv_main = _wkv_chunk([t.reshape(B, T, D) for t in (r_, lw_, k_, v_, kk_, b_)], wkv_meta,
                                     nC=T // (WKV_CHUNK * WKV_SUB), C=WKV_CHUNK, nsub=WKV_SUB, G=WKV_GROUP, row_block0=0)
            y = y.reshape(B * T, D)
            h = _rwkv_tail(y, bonus_, gate_, h, vec(rwkv_ln_g), vec(rwkv_ln_b), wo_rwkv, eh, eht, mlp, layer=i, tm=512)
        else:
            p = _norm_matmul(h, g_mix, (conv_wi,), tm=512, tn=1024, out_dtype=BF16)
            h, conv_main = _conv_main(p, conv_meta, conv_wf, conv_wo, h, mlp, layer=i, tm=512, seq=T)
    y_prompt = h.reshape(B, T, D)

    return (y_prompt, y_sample, ret_main[0], ret_dec_out[0], shift_main.reshape(B, D), shift_dec,
            wkv_main, wkv_dec, conv_main, conv_dec, ret_main[1], ret_dec_out[1])
```

```python
import functools
import math

import jax
import jax.numpy as jnp
from jax import lax
from jax.experimental import pallas as pl
from jax.experimental.pallas import tpu as pltpu

F32 = jnp.float32
BF16 = jnp.bfloat16

D = 1024
N_META = 16
RH = 4
DK = D // RH
DV = 2 * D // RH
HK = RH * DK
HV = RH * DV
RET_CHUNK = 256
RET_SUB = 4
ROPE_BASE = 10000.0
WH = 16
WN = 64
WKV_CHUNK = 64
INV_HI_LEVELS = 4
WKV_SUB = 2
WKV_GROUP = 2
D_FF = 4 * D
EPS = 1e-6
GN_EPS = 64e-5
LANES = 128
VMEM_LIMIT_V7X = 56 * 1024 * 1024

NT = (((1,), (1,)), ((), ()))
TN = (((0,), (0,)), ((), ()))


def _cp(n_axes):
    return pltpu.CompilerParams(dimension_semantics=("arbitrary",) * n_axes,
                                vmem_limit_bytes=VMEM_LIMIT_V7X)


def _dot(a, b):
    return jnp.dot(a, b, preferred_element_type=F32)


def _rms(x, g):
    return x * lax.rsqrt(jnp.mean(x * x, axis=-1, keepdims=True) + EPS) * g


def _sigmoid(x):
    return 1.0 / (1.0 + jnp.exp(-x))


def _split(x):
    hi = x.astype(BF16)
    lo = (x - hi.astype(F32)).astype(BF16)
    return hi, lo


def _seg_sum(x, eh, eht):
    return _dot(_dot(x.astype(BF16), eh).astype(BF16), eht)


def _resident(shape):
    return pl.BlockSpec(shape, lambda *idx: (0,) * len(shape), pipeline_mode=pl.Buffered(1))


def _norm_matmul_kernel(x_ref, g_ref, cos_ref, sin_ref, *refs, tn, rope):
    w_refs, o_ref = refs[:-1], refs[-1]
    xn = _rms(x_ref[...], g_ref[...]).astype(BF16)
    even = lax.broadcasted_iota(jnp.int32, (1, DK), 1) % 2 == 0
    col = 0
    for gi, w_ref in enumerate(w_refs):
        for j in range(w_ref.shape[1] // tn):
            acc = _dot(xn, w_ref[:, j * tn:(j + 1) * tn])
            if not (rope and gi == 0):
                o_ref[:, col:col + tn] = acc.astype(o_ref.dtype)
            else:
                c = cos_ref[...]
                s = sin_ref[...]
                scale = DK ** -0.5 if col >= HK else 1.0
                for hh in range(tn // DK):
                    x = acc[:, hh * DK:(hh + 1) * DK]
                    partner = jnp.where(even, pltpu.roll(x, DK - 1, 1), pltpu.roll(x, 1, 1))
                    o_ref[:, col + hh * DK:col + (hh + 1) * DK] = ((x * c + partner * s) * scale).astype(o_ref.dtype)
            col += tn


def _weight_spec(w):
    if not isinstance(w, tuple):
        return _resident(w.shape), w, w.shape[1]
    arr, layer, col0, width = w
    blk = col0 // width
    spec = pl.BlockSpec((None, arr.shape[1], width), lambda *idx: (layer, 0, blk), pipeline_mode=pl.Buffered(1))
    return spec, arr, width


def _norm_matmul(x, g, ws, *, tm, tn, out_dtype, rope=None):
    M, K = x.shape
    wspecs, wops, widths = zip(*[_weight_spec(w) for w in ws])
    N = sum(widths)
    if rope is None:
        cos = sin = jnp.zeros((8, DK), F32)
        cs_spec = pl.BlockSpec((8, DK), lambda i: (0, 0))
    else:
        cos, sin = rope
        nblk = cos.shape[0] // tm
        cs_spec = pl.BlockSpec((tm, DK), lambda i: (i % nblk, 0))
    return pl.pallas_call(
        functools.partial(_norm_matmul_kernel, tn=tn, rope=rope is not None),
        grid=(M // tm,),
        in_specs=[pl.BlockSpec((tm, K), lambda i: (i, 0)), _resident((1, K)), cs_spec, cs_spec] + list(wspecs),
        out_specs=pl.BlockSpec((tm, N), lambda i: (i, 0)),
        out_shape=jax.ShapeDtypeStruct((M, N), out_dtype),
        compiler_params=_cp(1),
        name="norm_matmul",
    )(x, g, cos, sin, *wops)


def _norm_matmul_t_kernel(x_ref, g_ref, w_ref, cos_ref, sin_ref, o_ref):
    xn = _rms(x_ref[...], g_ref[...]).astype(BF16)
    acc = lax.dot_general(w_ref[...], xn, (((0,), (1,)), ((), ())), preferred_element_type=F32)
    c = cos_ref[...]
    s = sin_ref[...]
    even = lax.broadcasted_iota(jnp.int32, (DK, 1), 0) % 2 == 0
    for hh in range(2 * RH):
        scale = 1.0 if hh < RH else DK ** -0.5
        x = acc[hh * DK:(hh + 1) * DK, :]
        partner = jnp.where(even, pltpu.roll(x, DK - 1, 0), pltpu.roll(x, 1, 0))
        o_ref[hh * DK:(hh + 1) * DK, :] = (x * c + partner * s) * scale


def _norm_matmul_t(x, g, w, cos_t, sin_t, *, rows):
    K = x.shape[1]
    wspec, wop, _ = _weight_spec(w)
    return pl.pallas_call(
        _norm_matmul_t_kernel,
        grid=(1,),
        in_specs=[pl.BlockSpec((rows, K), lambda i: (0, 0)),
                  pl.BlockSpec((1, K), lambda i: (0, 0)),
                  wspec,
                  pl.BlockSpec((DK, rows), lambda i: (0, 0)),
                  pl.BlockSpec((DK, rows), lambda i: (0, 0))],
        out_specs=pl.BlockSpec((2 * HK, rows), lambda i: (0, 0)),
        out_shape=jax.ShapeDtypeStruct((2 * HK, rows), F32),
        compiler_params=_cp(1),
        name="norm_matmul_t",
    )(x, g, wop, cos_t, sin_t)


MLP_TF = 512


def _mlp_tail(h1, g_ref, w1_ref, w2_ref, gf_ref, o_ref):
    xn = _rms(h1, g_ref[...]).astype(BF16)
    acc = None
    for f in range(D_FF // MLP_TF):
        a = _dot(xn, w1_ref[:, f * MLP_TF:(f + 1) * MLP_TF])
        a = jnp.square(jnp.maximum(a, 0.0)).astype(BF16)
        part = _dot(a, w2_ref[f * MLP_TF:(f + 1) * MLP_TF, :])
        acc = part if acc is None else acc + part
    out = h1 + acc
    o_ref[...] = out if gf_ref is None else _rms(out, gf_ref[...])


def _layer_resident(arr, layer):
    return pl.BlockSpec((None,) + arr.shape[1:], lambda *idx: (layer, 0, 0), pipeline_mode=pl.Buffered(1))


def _tail_specs(mlp, layer, final):
    g, w1, w2, gf = mlp
    specs = [_layer_resident(g, layer), _layer_resident(w1, layer), _layer_resident(w2, layer)]
    ops = [g, w1, w2]
    if final:
        specs.append(_resident(gf.shape))
        ops.append(gf)
    return specs, ops


def _ret_tail_kernel(a_ref, wo_ref, h_ref, g_ref, w1_ref, w2_ref, *rest):
    gf_ref, o_ref = (rest[0], rest[1]) if len(rest) == 2 else (None, rest[0])
    h1 = h_ref[...] + _dot(a_ref[...].astype(BF16), wo_ref[...])
    _mlp_tail(h1, g_ref, w1_ref, w2_ref, gf_ref, o_ref)


def _ret_tail(a, wo, h, mlp, *, layer, final, tm):
    M, K = a.shape
    tspecs, tops = _tail_specs(mlp, layer, final)
    row = pl.BlockSpec((tm, D), lambda i: (i, 0))
    return pl.pallas_call(
        _ret_tail_kernel,
        grid=(M // tm,),
        in_specs=[pl.BlockSpec((tm, K), lambda i: (i, 0)), _resident((K, D)), row] + tspecs,
        out_specs=row,
        out_shape=jax.ShapeDtypeStruct((M, D), F32),
        compiler_params=_cp(1),
        name="ret_tail",
    )(a, wo, h, *tops)


def _log_gamma(h):
    return math.log(1.0 - 2.0 ** (-5.0 - h))


def _ret_chunk_kernel(p_ref, s0_ref, y_ref, so_ref, s_ref, *, L, nsub, nC):
    c = pl.program_id(1)

    @pl.when(c == 0)
    def _():
        for h in range(RH):
            s_ref[h] = s0_ref[0, h]

    ti = lax.broadcasted_iota(jnp.int32, (L, L), 0)
    si = lax.broadcasted_iota(jnp.int32, (L, L), 1)
    diff = (ti - si).astype(F32)
    ri = lax.broadcasted_iota(jnp.int32, (L, 1), 0).astype(F32)
    for j in range(nsub):
        rows = slice(j * L, (j + 1) * L)
        for h in range(RH):
            lg = _log_gamma(h)
            mask = jnp.where(diff >= 0, jnp.exp(jnp.maximum(diff, 0.0) * lg), 0.0)
            qd = jnp.exp((ri + 1.0) * lg)
            kd = jnp.exp((L - 1.0 - ri) * lg)
            q = p_ref[rows, h * DK:(h + 1) * DK]
            k = p_ref[rows, HK + h * DK:HK + (h + 1) * DK]
            v = p_ref[rows, 2 * HK + h * DV:2 * HK + (h + 1) * DV].astype(BF16)
            g = p_ref[rows, 2 * HK + HV + h * DV:2 * HK + HV + (h + 1) * DV].astype(F32)
            sc = lax.dot_general(q.astype(BF16), k.astype(BF16), NT, preferred_element_type=F32) * mask
            inner = _dot(sc.astype(BF16), v)
            s_old = s_ref[h]
            cross = _dot((q.astype(F32) * qd).astype(BF16), s_old.astype(BF16))
            o = inner + cross
            s_ref[h] = s_old * math.exp(L * lg) + lax.dot_general(
                (k.astype(F32) * kd).astype(BF16), v, TN, preferred_element_type=F32)
            o = o * lax.rsqrt(jnp.mean(o * o, axis=-1, keepdims=True) + EPS)
            y_ref[rows, h * DV:(h + 1) * DV] = (g * _sigmoid(g) * o).astype(y_ref.dtype)

    @pl.when(c == nC - 1)
    def _():
        so_ref[0] = s_ref[...]


def _ret_chunk(p, s0, *, nb, nC, L, nsub, row_block0, out_dtype):
    shared = s0.shape[0] == 1
    R = L * nsub
    return pl.pallas_call(
        functools.partial(_ret_chunk_kernel, L=L, nsub=nsub, nC=nC),
        grid=(nb, nC),
        in_specs=[pl.BlockSpec((R, 2 * HK + 2 * HV), lambda b, c: (row_block0 + b * nC + c, 0)),
                  pl.BlockSpec((1, RH, DK, DV), lambda b, c: (0 if shared else b, 0, 0, 0))],
        out_specs=[pl.BlockSpec((R, HV), lambda b, c: (b * nC + c, 0)),
                   pl.BlockSpec((1, RH, DK, DV), lambda b, c: (b, 0, 0, 0))],
        out_shape=[jax.ShapeDtypeStruct((nb * nC * R, HV), out_dtype),
                   jax.ShapeDtypeStruct((nb, RH, DK, DV), F32)],
        scratch_shapes=[pltpu.VMEM((RH, DK, DV), F32)],
        compiler_params=_cp(2),
        name="ret_chunk",
    )(p, s0)


def _ret_dec_kernel(qkt_ref, p_ref, s_ref, y_ref, so_ref, *, bb, nbatch):
    step = pl.program_id(0)
    lane = lax.broadcasted_iota(jnp.int32, (1, nbatch), 1)

    def body(jb, carry):
        b = step * bb + jb
        onehot = (lane == b).astype(F32)
        rows = []
        for h in range(RH):
            gam = math.exp(_log_gamma(h))
            vrow = p_ref[pl.ds(b, 1), 2 * HK + h * DV:2 * HK + (h + 1) * DV]
            qcol = jnp.sum(qkt_ref[h * DK:(h + 1) * DK, :] * onehot, axis=1, keepdims=True)
            kcol = jnp.sum(qkt_ref[HK + h * DK:HK + (h + 1) * DK, :] * onehot, axis=1, keepdims=True)
            sn = gam * s_ref[jb, h] + kcol * vrow
            so_ref[jb, h] = sn
            o = jnp.sum(qcol * sn, axis=0, keepdims=True)
            o = o * lax.rsqrt(jnp.mean(o * o, axis=-1, keepdims=True) + EPS)
            g = p_ref[pl.ds(b, 1), 2 * HK + HV + h * DV:2 * HK + HV + (h + 1) * DV]
            rows.append(g * _sigmoid(g) * o)
        y_ref[b] = jnp.concatenate(rows, axis=1)
        return carry

    lax.fori_loop(0, bb, body, 0)


def _ret_dec(qkt, p, s, *, bb):
    nbatch = s.shape[0]
    return pl.pallas_call(
        functools.partial(_ret_dec_kernel, bb=bb, nbatch=nbatch),
        grid=(nbatch // bb,),
        in_specs=[pl.BlockSpec(qkt.shape, lambda i: (0, 0)),
                  pl.BlockSpec(p.shape, lambda i: (0, 0)),
                  pl.BlockSpec((bb, RH, DK, DV), lambda i: (i, 0, 0, 0))],
        out_specs=[pl.BlockSpec((nbatch, 1, HV), lambda i: (0, 0, 0)),
                   pl.BlockSpec((bb, RH, DK, DV), lambda i: (i, 0, 0, 0))],
        out_shape=[jax.ShapeDtypeStruct((nbatch, 1, HV), F32),
                   jax.ShapeDtypeStruct(s.shape, F32)],
        compiler_params=_cp(1),
        name="ret_dec",
    )(qkt, p, s)


N_RWKV_OUT = 8


def _rwkv_core(xn, xprev, w_refs, out_refs):
    (mix_ref, wrkv_ref, w0_ref, w1_ref, w2_ref, a0_ref, a1_ref, a2_ref, g1_ref, g2_ref,
     kk_ref, ka_ref, rk_ref, eh_ref, eht_ref) = w_refs
    r_o, lw_o, k_o, v_o, kk_o, b_o, g_o, bonus_o = out_refs
    xx = xprev - xn

    def xm(j):
        return (xn + xx * mix_ref[j:j + 1, :]).astype(BF16)

    r = _dot(xm(0), wrkv_ref[0])
    k = _dot(xm(1), wrkv_ref[1])
    v = _dot(xm(2), wrkv_ref[2])
    wl = _dot(jnp.tanh(_dot(xm(3), w1_ref[...])).astype(BF16), w2_ref[...])
    lw_o[...] = -math.exp(-0.5) * _sigmoid(w0_ref[...] + wl)
    al = _dot(_dot(xm(4), a1_ref[...]).astype(BF16), a2_ref[...])
    a = _sigmoid(a0_ref[...] + al)
    g_o[...] = _dot(_sigmoid(_dot(xm(5), g1_ref[...])).astype(BF16), g2_ref[...]).astype(g_o.dtype)
    kk = k * kk_ref[...]
    ssq = _seg_sum(kk * kk, eh_ref[...], eht_ref[...])
    kkn = kk / jnp.maximum(jnp.sqrt(ssq), 1e-12)
    k2 = k * (1.0 + (a - 1.0) * ka_ref[...])
    rk = _seg_sum(r * k2 * rk_ref[...], eh_ref[...], eht_ref[...])
    r_o[...] = r.astype(r_o.dtype)
    k_o[...] = k2.astype(k_o.dtype)
    v_o[...] = v.astype(v_o.dtype)
    kk_o[...] = kkn.astype(kk_o.dtype)
    b_o[...] = (kkn * a).astype(b_o.dtype)
    bonus_o[...] = (rk * v).astype(bonus_o.dtype)


def _rwkv_proj_main_kernel(h_ref, hp_ref, sh0_ref, g_ref, *refs, tm, tps):
    w_refs = refs[:15]
    out_refs = refs[15:15 + N_RWKV_OUT]
    sho_ref = refs[15 + N_RWKV_OUT]
    i = pl.program_id(0)
    xn = _rms(h_ref[...], g_ref[...])
    prevn = _rms(hp_ref[...], g_ref[...])[7:8, :]
    prev = jnp.where(i % tps == 0, sh0_ref[...], prevn)
    row = lax.broadcasted_iota(jnp.int32, (tm, 1), 0)
    xprev = jnp.where(row == 0, prev, pltpu.roll(xn, 1, 0))
    _rwkv_core(xn, xprev, w_refs, out_refs)
    sho_ref[0] = xn[tm - 1:tm, :]


def _rwkv_proj_small_kernel(h_ref, shp_ref, g_ref, *refs, n_dec):
    w_refs = refs[:15]
    out_refs = refs[15:15 + N_RWKV_OUT]
    xn_ref = refs[15 + N_RWKV_OUT]
    t_refs = refs[16 + N_RWKV_OUT:]
    m = h_ref.shape[0]
    xn = _rms(h_ref[...], g_ref[...])
    row = lax.broadcasted_iota(jnp.int32, (m, 1), 0)
    xprev = jnp.where(row <= n_dec, shp_ref[...], pltpu.roll(xn, 1, 0))
    _rwkv_core(xn, xprev, w_refs, out_refs)
    xn_ref[...] = xn
    for src, dst in zip(out_refs[:6], t_refs):
        dst[...] = src[0:n_dec, :].T


def _rwkv_weight_specs(wts):
    return [_resident(w.shape) for w in wts]


def _rwkv_proj_main(h, sh0, g, wts, *, tm, seq):
    M = h.shape[0]
    tps = seq // tm
    nseq = M // seq
    odt = [BF16, F32, BF16, BF16, BF16, BF16, BF16, BF16]
    row_spec = pl.BlockSpec((tm, D), lambda i: (i, 0))
    return pl.pallas_call(
        functools.partial(_rwkv_proj_main_kernel, tm=tm, tps=tps),
        grid=(M // tm,),
        in_specs=[row_spec,
                  pl.BlockSpec((8, D), lambda i: (jnp.maximum(i * (tm // 8) - 1, 0), 0)),
                  pl.BlockSpec((1, D), lambda i: (0, 0)),
                  pl.BlockSpec((1, D), lambda i: (0, 0))] + _rwkv_weight_specs(wts),
        out_specs=[row_spec] * N_RWKV_OUT + [pl.BlockSpec((1, 1, D), lambda i: (i // tps, 0, 0))],
        out_shape=[jax.ShapeDtypeStruct((M, D), dt) for dt in odt] + [jax.ShapeDtypeStruct((nseq, 1, D), F32)],
        compiler_params=_cp(1),
        name="rwkv_proj_main",
    )(h, h, sh0, g, *wts)


def _rwkv_proj_small(h, shp, g, wts, *, n_dec):
    M = h.shape[0]
    full = pl.BlockSpec((M, D), lambda i: (0, 0))
    return pl.pallas_call(
        functools.partial(_rwkv_proj_small_kernel, n_dec=n_dec),
        grid=(1,),
        in_specs=[full, full, pl.BlockSpec((1, D), lambda i: (0, 0))] + _rwkv_weight_specs(wts),
        out_specs=[full] * (N_RWKV_OUT + 1) + [pl.BlockSpec((D, n_dec), lambda i: (0, 0))] * 6,
        out_shape=[jax.ShapeDtypeStruct((M, D), F32)] * (N_RWKV_OUT + 1) + [jax.ShapeDtypeStruct((D, n_dec), F32)] * 6,
        compiler_params=_cp(1),
        name="rwkv_proj_small",
    )(h, shp, g, *wts)


def _chain1(cat, C):
    cb = cat.astype(BF16)
    return _dot(cb[:, 0:C], cb)


def _chain3(cat, C):
    hi, lo = _split(cat)
    r = _dot(jnp.concatenate([hi[:, 0:C], lo[:, 0:C]], axis=0), hi)
    return r[0:C] + r[C:2 * C] + _dot(hi[:, 0:C], lo)


def _wkv_sweep(rows, r_ref, lw_ref, k_ref, v_ref, kk_ref, b_ref, y_ref, s_ref, *, C, G):
    ti = lax.broadcasted_iota(jnp.int32, (C, C), 0)
    si = lax.broadcasted_iota(jnp.int32, (C, C), 1)
    incl = si <= ti
    strict = si < ti
    tri = jnp.where(incl, 1.0, 0.0).astype(BF16)
    t2 = lax.broadcasted_iota(jnp.int32, (C, 2 * C), 0)
    s2 = lax.broadcasted_iota(jnp.int32, (C, 2 * C), 1)
    incl2 = jnp.where(s2 >= C, s2 - C, s2) <= t2

    at, rt, kt, bt, ke, be, wc, vb = [], [], [], [], [], [], [], []
    for g in range(G):
        lw = lw_ref[g, rows, :].astype(F32)
        l0, l1 = _split(lw)
        cum = _dot(tri, l0) + _dot(tri, l1)
        tot = cum[C - 1:C, :]
        w_in = jnp.exp(-cum)
        w_end = jnp.exp(tot - cum)
        kf = k_ref[g, rows, :].astype(F32)
        bf = b_ref[g, rows, :].astype(F32)
        rt.append((r_ref[g, rows, :].astype(F32) * jnp.exp(cum)).astype(BF16))
        at.append((-kk_ref[g, rows, :].astype(F32) * jnp.exp(cum - lw)).astype(BF16))
        kt.append((kf * w_in).astype(BF16))
        bt.append((bf * w_in).astype(BF16))
        ke.append((kf * w_end).astype(BF16))
        be.append((bf * w_end).astype(BF16))
        wc.append(jnp.exp(tot))
        vb.append(v_ref[g, rows, :].astype(BF16))

    items = [(g, h) for g in range(G) for h in range(WH)]
    n = range(len(items))
    sl = lambda h: slice(h * WN, (h + 1) * WN)
    s_old = [s_ref[g, h] for g, h in items]
    lhs = [jnp.concatenate([at[g][:, sl(h)], rt[g][:, sl(h)]], axis=0) for g, h in items]
    rhs = [jnp.concatenate([bt[g][:, sl(h)], kt[g][:, sl(h)], s_old[i].astype(BF16)], axis=0)
           for i, (g, h) in enumerate(items)]
    gm = [lax.dot_general(lhs[i], rhs[i], NT, preferred_element_type=F32) for i in n]
    lh = [m[:, 2 * C:2 * C + WN] for m in gm]
    vh = [vb[g][:, sl(h)] for g, h in items]
    a_ak = [jnp.where(strict, m[0:C, C:2 * C], 0.0).astype(BF16) for m in gm]
    a_r = [jnp.where(incl2, m[C:2 * C, 0:2 * C], 0.0).astype(BF16) for m in gm]
    cat = [jnp.concatenate([jnp.where(strict, gm[i][0:C, 0:C], 0.0), lh[i][0:C] + _dot(a_ak[i], vh[i])], axis=1)
           for i in n]
    keep_x = lax.broadcasted_iota(jnp.int32, (C, C + WN), 1) >= C
    nlev = int(math.log2(C))
    for lv in range(nlev - 1):
        step = _chain3 if lv < INV_HI_LEVELS else _chain1
        cat = [step(cat[i], C) + jnp.where(keep_x, cat[i], 0.0) for i in n]
    u = [cat[i][:, C:C + WN] + _chain1(cat[i], C)[:, C:C + WN] for i in n]
    uv = [jnp.concatenate([u[i].astype(BF16), vh[i]], axis=0) for i in n]
    y = [lh[i][C:2 * C] + _dot(a_r[i], uv[i]) for i in n]
    for g in range(G):
        y_ref[g, rows, :] = jnp.concatenate(y[g * WH:(g + 1) * WH], axis=1)
    for i, (g, h) in enumerate(items):
        s_ref[g, h] = s_old[i] * wc[g][:, sl(h)] + lax.dot_general(
            uv[i], jnp.concatenate([be[g][:, sl(h)], ke[g][:, sl(h)]], axis=0), TN, preferred_element_type=F32)


def _wkv_chunk_kernel(r_ref, lw_ref, k_ref, v_ref, kk_ref, b_ref, s0_ref, y_ref, so_ref, s_ref, *, C, nsub, nC, G, shared):
    c = pl.program_id(1)

    @pl.when(c == 0)
    def _():
        for g in range(G):
            s_ref[g] = s0_ref[0 if shared else g]

    for j in range(nsub):
        _wkv_sweep(slice(j * C, (j + 1) * C), r_ref, lw_ref, k_ref, v_ref, kk_ref, b_ref, y_ref, s_ref, C=C, G=G)

    @pl.when(c == nC - 1)
    def _():
        so_ref[...] = s_ref[...]


def _wkv_chunk(ins, s0, *, nC, C, nsub, G, row_block0):
    nseq = ins[0].shape[0]
    shared = s0.shape[0] == 1
    R = C * nsub
    row_spec = pl.BlockSpec((G, R, D), lambda b, c: (b, row_block0 + c, 0))
    st_spec = pl.BlockSpec((G, WH, WN, WN), lambda b, c: (b, 0, 0, 0))
    s0_spec = pl.BlockSpec((1, WH, WN, WN), lambda b, c: (0, 0, 0, 0)) if shared else st_spec
    return pl.pallas_call(
        functools.partial(_wkv_chunk_kernel, C=C, nsub=nsub, nC=nC, G=G, shared=shared),
        grid=(nseq // G, nC),
        in_specs=[row_spec] * 6 + [s0_spec],
        out_specs=[pl.BlockSpec((G, R, D), lambda b, c: (b, c, 0)), st_spec],
        out_shape=[jax.ShapeDtypeStruct((nseq, nC * R, D), F32),
                   jax.ShapeDtypeStruct((nseq, WH, WN, WN), F32)],
        scratch_shapes=[pltpu.VMEM((G, WH, WN, WN), F32)],
        compiler_params=_cp(2),
        name="wkv_chunk",
    )(*ins, s0)


def _wkv_dec_kernel(r_ref, lw_ref, k_ref, v_ref, kk_ref, b_ref, s_ref, y_ref, so_ref):
    nkk = -kk_ref[...]
    w = jnp.exp(lw_ref[...])
    bb = b_ref[...]
    k2 = k_ref[...]
    rr = r_ref[...]
    for vi in range(WN):
        s_old = s_ref[0, vi]
        sa = jnp.sum(s_old * nkk, axis=0, keepdims=True)
        sn = s_old * w + sa * bb + v_ref[vi:vi + 1, :] * k2
        so_ref[0, vi] = sn
        y_ref[vi:vi + 1, :] = jnp.sum(sn * rr, axis=0, keepdims=True)


def _wkv_dec(ins, s):
    nbatch = s.shape[-1]
    vec = pl.BlockSpec((WN, nbatch), lambda h: (h, 0))
    st = pl.BlockSpec((1, WN, WN, nbatch), lambda h: (h, 0, 0, 0))
    return pl.pallas_call(
        _wkv_dec_kernel,
        grid=(WH,),
        in_specs=[vec] * 6 + [st],
        out_specs=[vec, st],
        out_shape=[jax.ShapeDtypeStruct((D, nbatch), F32), jax.ShapeDtypeStruct(s.shape, F32)],
        compiler_params=_cp(1),
        name="wkv_dec",
    )(*ins, s)


def _rwkv_tail_kernel(y_ref, bonus_ref, gate_ref, lng_ref, lnb_ref, wo_ref, eh_ref, eht_ref, h_ref,
                      g_ref, w1_ref, w2_ref, o_ref):
    y = y_ref[...]
    eh = eh_ref[...]
    eht = eht_ref[...]
    mu = _seg_sum(y, eh, eht) * (1.0 / WN)
    d = y - mu
    var = _seg_sum(d * d, eh, eht) * (1.0 / WN)
    yn = d * lax.rsqrt(var + GN_EPS) * lng_ref[...] + lnb_ref[...]
    z = (yn + bonus_ref[...].astype(F32)) * gate_ref[...].astype(F32)
    h1 = h_ref[...] + _dot(z.astype(BF16), wo_ref[...])
    _mlp_tail(h1, g_ref, w1_ref, w2_ref, None, o_ref)


def _rwkv_tail(y, bonus, gate, h, lng, lnb, wo, eh, eht, mlp, *, layer, tm):
    M = h.shape[0]
    tspecs, tops = _tail_specs(mlp, layer, False)
    row = pl.BlockSpec((tm, D), lambda i: (i, 0))
    return pl.pallas_call(
        _rwkv_tail_kernel,
        grid=(M // tm,),
        in_specs=[row, row, row, _resident((1, D)), _resident((1, D)), _resident((D, D)),
                  _resident(eh.shape), _resident(eht.shape), row] + tspecs,
        out_specs=row,
        out_shape=jax.ShapeDtypeStruct((M, D), F32),
        compiler_params=_cp(1),
        name="rwkv_tail",
    )(y, bonus, gate, lng, lnb, wo, eh, eht, h, *tops)


def _conv_main_kernel(p_ref, pp_ref, buf_ref, cw_ref, wo_ref, h_ref, g_ref, w1_ref, w2_ref, o_ref, nb_ref, *, tm, tps):
    i = pl.program_id(0)
    bq = p_ref[:, 0:D].astype(F32)
    u = p_ref[:, D:2 * D].astype(F32) * p_ref[:, 2 * D:3 * D].astype(F32)
    up = pp_ref[:, D:2 * D].astype(F32) * pp_ref[:, 2 * D:3 * D].astype(F32)
    npr = pp_ref.shape[0]
    first = i % tps == 0
    prev1 = jnp.where(first, buf_ref[1:2, :], up[npr - 1:npr, :])
    prev2 = jnp.where(first, buf_ref[0:1, :], up[npr - 2:npr - 1, :])
    row = lax.broadcasted_iota(jnp.int32, (tm, 1), 0)
    m1 = jnp.where(row == 0, prev1, pltpu.roll(u, 1, 0))
    m2 = jnp.where(row == 0, prev2, jnp.where(row == 1, prev1, pltpu.roll(u, 2, 0)))
    y = cw_ref[0:1, :] * m2 + cw_ref[1:2, :] * m1 + cw_ref[2:3, :] * u
    nb_ref[0] = u[tm - 2:tm, :]
    h1 = h_ref[...] + _dot((bq * y).astype(BF16), wo_ref[...])
    _mlp_tail(h1, g_ref, w1_ref, w2_ref, None, o_ref)


def _conv_main(p, buf, cw, wo, h, mlp, *, layer, tm, seq):
    M = h.shape[0]
    tps = seq // tm
    npr = 16
    tspecs, tops = _tail_specs(mlp, layer, False)
    return pl.pallas_call(
        functools.partial(_conv_main_kernel, tm=tm, tps=tps),
        grid=(M // tm,),
        in_specs=[pl.BlockSpec((tm, 3 * D), lambda i: (i, 0)),
                  pl.BlockSpec((npr, 3 * D), lambda i: (jnp.maximum(i * (tm // npr) - 1, 0), 0)),
                  _resident((2, D)), _resident((3, D)), _resident((D, D)),
                  pl.BlockSpec((tm, D), lambda i: (i, 0))] + tspecs,
        out_specs=[pl.BlockSpec((tm, D), lambda i: (i, 0)),
                   pl.BlockSpec((1, 2, D), lambda i: (i // tps, 0, 0))],
        out_shape=[jax.ShapeDtypeStruct((M, D), F32), jax.ShapeDtypeStruct((M // seq, 2, D), F32)],
        compiler_params=_cp(1),
        name="conv_main",
    )(p, p, buf, cw, wo, h, *tops)


def _conv_small_kernel(p_ref, b0_ref, b1_ref, cw_ref, wo_ref, h_ref, g_ref, w1_ref, w2_ref, o_ref, u_ref, *, n_dec):
    m = h_ref.shape[0]
    bq = p_ref[:, 0:D]
    u = p_ref[:, D:2 * D] * p_ref[:, 2 * D:3 * D]
    row = lax.broadcasted_iota(jnp.int32, (m, 1), 0)
    m1 = jnp.where(row <= n_dec, b1_ref[...], pltpu.roll(u, 1, 0))
    m2 = jnp.where(row <= n_dec + 1, b0_ref[...], pltpu.roll(u, 2, 0))
    y = cw_ref[0:1, :] * m2 + cw_ref[1:2, :] * m1 + cw_ref[2:3, :] * u
    u_ref[...] = u
    h1 = h_ref[...] + _dot((bq * y).astype(BF16), wo_ref[...])
    _mlp_tail(h1, g_ref, w1_ref, w2_ref, None, o_ref)


def _conv_small(p, b0, b1, cw, wo, h, mlp, *, layer, n_dec):
    M = h.shape[0]
    full = pl.BlockSpec((M, D), lambda i: (0, 0))
    tspecs, tops = _tail_specs(mlp, layer, False)
    return pl.pallas_call(
        functools.partial(_conv_small_kernel, n_dec=n_dec),
        grid=(1,),
        in_specs=[pl.BlockSpec((M, 3 * D), lambda i: (0, 0)), full, full,
                  pl.BlockSpec((3, D), lambda i: (0, 0)),
                  pl.BlockSpec((D, D), lambda i: (0, 0)), full] + tspecs,
        out_specs=[full, full],
        out_shape=[jax.ShapeDtypeStruct((M, D), F32)] * 2,
        compiler_params=_cp(1),
        name="conv_small",
    )(p, b0, b1, cw, wo, h, *tops)


def _rope_tables(pos):
    inv = 1.0 / (ROPE_BASE ** jnp.linspace(0.0, 1.0, DK // 2, dtype=F32))
    ang = pos.astype(F32)[:, None] * inv[None, :]
    return jnp.cos(ang), jnp.sin(ang)


def kernel(x_prompt, x_sample, state_ret_l0, state_rwkv_shift_l1, state_rwkv_wkv_l1, state_conv_l2, state_ret_l3,
           meta_tokens, norm_mix, norm_mlp, norm_final, ret_w_in, ret_w_out, rwkv_mix, rwkv_w_rkv, rwkv_w0, rwkv_w1,
           rwkv_w2, rwkv_a0, rwkv_a1, rwkv_a2, rwkv_g1, rwkv_g2, rwkv_k_k, rwkv_k_a, rwkv_r_k, rwkv_ln_g, rwkv_ln_b,
           rwkv_w_o, conv_w_in, conv_w, conv_w_out, mlp_w1, mlp_w2):
    B, T, _ = x_prompt.shape
    NB = x_sample.shape[0]
    MS = NB + N_META
    past_len = 16384

    ret_wb = ret_w_in.astype(BF16)
    ret_wp = [((ret_wb, j, 0, 2 * HK), (ret_wb, j, 2 * HK, 2 * HK), (ret_wb, j, 4 * HK, 2 * HK)) for j in range(2)]
    ret_wo = [ret_w_out[0].astype(BF16), ret_w_out[1].astype(BF16)]
    mlp = (norm_mlp.reshape(4, 1, D).astype(F32), mlp_w1.astype(BF16), mlp_w2.astype(BF16),
           norm_final.reshape(1, D).astype(F32))
    head_of_lane = jnp.arange(D) // WN
    eh = (head_of_lane[:, None] == jnp.arange(LANES)[None, :]).astype(BF16)
    eht = eh.T
    vec = lambda a: a.reshape(1, D).astype(F32)
    rwkv_wts = (rwkv_mix.astype(F32), rwkv_w_rkv.astype(BF16), vec(rwkv_w0), rwkv_w1.astype(BF16), rwkv_w2.astype(BF16),
                vec(rwkv_a0), rwkv_a1.astype(BF16), rwkv_a2.astype(BF16), rwkv_g1.astype(BF16), rwkv_g2.astype(BF16),
                vec(rwkv_k_k), vec(rwkv_k_a), vec(rwkv_r_k), eh, eht)
    wo_rwkv = rwkv_w_o.astype(BF16)
    conv_wi = conv_w_in.astype(BF16)
    conv_wo = conv_w_out.astype(BF16)
    conv_wf = conv_w.astype(F32)

    pair_sign = jnp.where(jnp.arange(DK) % 2 == 0, -1.0, 1.0).astype(F32)

    def rope_rows(pos):
        c, sn = _rope_tables(pos)
        return jnp.repeat(c, 2, axis=1), jnp.repeat(sn, 2, axis=1) * pair_sign[None, :]

    cos_m, sin_m = rope_rows(N_META + jnp.arange(T))
    pos_s = jnp.concatenate([jnp.full((NB,), past_len, jnp.int32), jnp.arange(N_META, dtype=jnp.int32)])
    cos_s, sin_s = rope_rows(pos_s)
    cos_t, sin_t = cos_s[:NB].T, sin_s[:NB].T

    h = jnp.concatenate([x_sample.reshape(NB, D), meta_tokens.astype(F32)], axis=0)
    meta_blk = NB // N_META
    zero_ret = jnp.zeros((1, RH, DK, DV), F32)
    ret_meta, ret_dec_out = [], []
    for i in range(4):
        g_mix = norm_mix[i].reshape(1, D)
        kind = i % 3
        if kind == 0:
            j = i // 3
            p = _norm_matmul(h, g_mix, ret_wp[j], tm=MS, tn=1024, out_dtype=F32, rope=(cos_s, sin_s))
            qkt = _norm_matmul_t(h, g_mix, ret_wp[j][0], cos_t, sin_t, rows=NB)
            s_in = (state_ret_l0 if j == 0 else state_ret_l3).astype(F32)
            y_dec, s_dec = _ret_dec(qkt, p, s_in, bb=4)
            y_meta, s_meta = _ret_chunk(p, zero_ret, nb=1, nC=1, L=N_META, nsub=1, row_block0=meta_blk, out_dtype=F32)
            ret_meta.append(s_meta)
            ret_dec_out.append(s_dec)
            h = _ret_tail(jnp.concatenate([y_dec.reshape(NB, HV), y_meta], axis=0), ret_wo[j], h, mlp,
                          layer=i, final=i == 3, tm=MS)
        elif kind == 1:
            shp = jnp.concatenate([state_rwkv_shift_l1.astype(F32), jnp.zeros((N_META, D), F32)], axis=0)
            outs = _rwkv_proj_small(h, shp, g_mix, rwkv_wts, n_dec=NB)
            r_, lw_, k_, v_, kk_, b_, gate_, bonus_, xn_ = outs[:9]
            y_dec, wkv_dec = _wkv_dec(outs[9:], jnp.transpose(state_rwkv_wkv_l1.astype(F32), (1, 2, 3, 0)))
            wkv_dec = jnp.transpose(wkv_dec, (3, 0, 1, 2))
            y_meta, wkv_meta = _wkv_chunk([t[None] for t in (r_, lw_, k_, v_, kk_, b_)], jnp.zeros((1, WH, WN, WN), F32),
                                          nC=1, C=N_META, nsub=1, G=1, row_block0=meta_blk)
            y_meta = y_meta[0]
            shift_dec = xn_[:NB]
            shift_meta = xn_[MS - 1:MS]
            h = _rwkv_tail(jnp.concatenate([y_dec.T, y_meta], axis=0), bonus_, gate_, h, vec(rwkv_ln_g),
                           vec(rwkv_ln_b), wo_rwkv, eh, eht, mlp, layer=i, tm=MS)
        else:
            p = _norm_matmul(h, g_mix, (conv_wi,), tm=MS, tn=1024, out_dtype=F32)
            zpad = jnp.zeros((N_META, D), F32)
            b0 = jnp.concatenate([state_conv_l2[:, 0].astype(F32), zpad], axis=0)
            b1 = jnp.concatenate([state_conv_l2[:, 1].astype(F32), zpad], axis=0)
            h, u = _conv_small(p, b0, b1, conv_wf, conv_wo, h, mlp, layer=i, n_dec=NB)
            conv_dec = jnp.stack([state_conv_l2[:, 1].astype(F32), u[:NB]], axis=1)
            conv_meta = u[MS - 2:MS]
    y_sample = h[:NB].reshape(NB, 1, D)

    h = x_prompt.reshape(B * T, D)
    nC = T // RET_CHUNK
    ret_main = []
    for i in range(4):
        g_mix = norm_mix[i].reshape(1, D)
        kind = i % 3
        if kind == 0:
            j = i // 3
            p = _norm_matmul(h, g_mix, ret_wp[j], tm=512, tn=1024, out_dtype=BF16, rope=(cos_m, sin_m))
            y, s_fin = _ret_chunk(p, ret_meta[j], nb=B, nC=nC // RET_SUB, L=RET_CHUNK, nsub=RET_SUB, row_block0=0,
                                  out_dtype=BF16)
            ret_main.append(s_fin)
            h = _ret_tail(y, ret_wo[j], h, mlp, layer=i, final=i == 3, tm=512)
        elif kind == 1:
            outs = _rwkv_proj_main(h, shift_meta, g_mix, rwkv_wts, tm=512, seq=T)
            r_, lw_, k_, v_, kk_, b_, gate_, bonus_, shift_main = outs
            y, wkv_main = _wkv_chunk([t.reshape(B, T, D) for t in (r_, lw_, k_, v_, kk_, b_)], wkv_meta,
                                     nC=T // (WKV_CHUNK * WKV_SUB), C=WKV_CHUNK, nsub=WKV_SUB, G=WKV_GROUP, row_block0=0)
            y = y.reshape(B * T, D)
            h = _rwkv_tail(y, bonus_, gate_, h, vec(rwkv_ln_g), vec(rwkv_ln_b), wo_rwkv, eh, eht, mlp, layer=i, tm=512)
        else:
            p = _norm_matmul(h, g_mix, (conv_wi,), tm=512, tn=1024, out_dtype=BF16)
            h, conv_main = _conv_main(p, conv_meta, conv_wf, conv_wo, h, mlp, layer=i, tm=512, seq=T)
    y_prompt = h.reshape(B, T, D)

    return (y_prompt, y_sample, ret_main[0], ret_dec_out[0], shift_main.reshape(B, D), shift_dec,
            wkv_main, wkv_dec, conv_main, conv_dec, ret_main[1], ret_dec_out[1])
```

```python
import functools
import math

import jax
import jax.numpy as jnp
from jax import lax
from jax.experimental import pallas as pl
from jax.experimental.pallas import tpu as pltpu

F32 = jnp.float32
BF16 = jnp.bfloat16

D = 1024
N_META = 16
RH = 4
DK = D // RH
DV = 2 * D // RH
HK = RH * DK
HV = RH * DV
RET_CHUNK = 256
RET_SUB = 4
ROPE_BASE = 10000.0
WH = 16
WN = 64
WKV_CHUNK = 64
INV_HI_LEVELS = 4
WKV_SUB = 4
WKV_GROUP = 2
D_FF = 4 * D
EPS = 1e-6
GN_EPS = 64e-5
LANES = 128
VMEM_LIMIT_V7X = 56 * 1024 * 1024

NT = (((1,), (1,)), ((), ()))
TN = (((0,), (0,)), ((), ()))


def _cp(n_axes):
    return pltpu.CompilerParams(dimension_semantics=("arbitrary",) * n_axes,
                                vmem_limit_bytes=VMEM_LIMIT_V7X)


def _dot(a, b):
    return jnp.dot(a, b, preferred_element_type=F32)


def _rms(x, g):
    return x * lax.rsqrt(jnp.mean(x * x, axis=-1, keepdims=True) + EPS) * g


def _sigmoid(x):
    return 1.0 / (1.0 + jnp.exp(-x))


def _split(x):
    hi = x.astype(BF16)
    lo = (x - hi.astype(F32)).astype(BF16)
    return hi, lo


def _seg_sum(x, eh, eht):
    return _dot(_dot(x.astype(BF16), eh).astype(BF16), eht)


def _resident(shape):
    return pl.BlockSpec(shape, lambda *idx: (0,) * len(shape), pipeline_mode=pl.Buffered(1))


def _norm_matmul_kernel(x_ref, g_ref, cos_ref, sin_ref, *refs, tn, rope):
    w_refs, o_ref = refs[:-1], refs[-1]
    xn = _rms(x_ref[...], g_ref[...]).astype(BF16)
    even = lax.broadcasted_iota(jnp.int32, (1, DK), 1) % 2 == 0
    col = 0
    for gi, w_ref in enumerate(w_refs):
        for j in range(w_ref.shape[1] // tn):
            acc = _dot(xn, w_ref[:, j * tn:(j + 1) * tn])
            if not (rope and gi == 0):
                o_ref[:, col:col + tn] = acc.astype(o_ref.dtype)
            else:
                c = cos_ref[...]
                s = sin_ref[...]
                scale = DK ** -0.5 if col >= HK else 1.0
                for hh in range(tn // DK):
                    x = acc[:, hh * DK:(hh + 1) * DK]
                    partner = jnp.where(even, pltpu.roll(x, DK - 1, 1), pltpu.roll(x, 1, 1))
                    o_ref[:, col + hh * DK:col + (hh + 1) * DK] = ((x * c + partner * s) * scale).astype(o_ref.dtype)
            col += tn


def _weight_spec(w):
    if not isinstance(w, tuple):
        return _resident(w.shape), w, w.shape[1]
    arr, layer, col0, width = w
    blk = col0 // width
    spec = pl.BlockSpec((None, arr.shape[1], width), lambda *idx: (layer, 0, blk), pipeline_mode=pl.Buffered(1))
    return spec, arr, width


def _norm_matmul(x, g, ws, *, tm, tn, out_dtype, rope=None):
    M, K = x.shape
    wspecs, wops, widths = zip(*[_weight_spec(w) for w in ws])
    N = sum(widths)
    if rope is None:
        cos = sin = jnp.zeros((8, DK), F32)
        cs_spec = pl.BlockSpec((8, DK), lambda i: (0, 0))
    else:
        cos, sin = rope
        nblk = cos.shape[0] // tm
        cs_spec = pl.BlockSpec((tm, DK), lambda i: (i % nblk, 0))
    return pl.pallas_call(
        functools.partial(_norm_matmul_kernel, tn=tn, rope=rope is not None),
        grid=(M // tm,),
        in_specs=[pl.BlockSpec((tm, K), lambda i: (i, 0)), _resident((1, K)), cs_spec, cs_spec] + list(wspecs),
        out_specs=pl.BlockSpec((tm, N), lambda i: (i, 0)),
        out_shape=jax.ShapeDtypeStruct((M, N), out_dtype),
        compiler_params=_cp(1),
        name="norm_matmul",
    )(x, g, cos, sin, *wops)


def _norm_matmul_t_kernel(x_ref, g_ref, w_ref, cos_ref, sin_ref, o_ref):
    xn = _rms(x_ref[...], g_ref[...]).astype(BF16)
    acc = lax.dot_general(w_ref[...], xn, (((0,), (1,)), ((), ())), preferred_element_type=F32)
    c = cos_ref[...]
    s = sin_ref[...]
    even = lax.broadcasted_iota(jnp.int32, (DK, 1), 0) % 2 == 0
    for hh in range(2 * RH):
        scale = 1.0 if hh < RH else DK ** -0.5
        x = acc[hh * DK:(hh + 1) * DK, :]
        partner = jnp.where(even, pltpu.roll(x, DK - 1, 0), pltpu.roll(x, 1, 0))
        o_ref[hh * DK:(hh + 1) * DK, :] = (x * c + partner * s) * scale


def _norm_matmul_t(x, g, w, cos_t, sin_t, *, rows):
    K = x.shape[1]
    wspec, wop, _ = _weight_spec(w)
    return pl.pallas_call(
        _norm_matmul_t_kernel,
        grid=(1,),
        in_specs=[pl.BlockSpec((rows, K), lambda i: (0, 0)),
                  pl.BlockSpec((1, K), lambda i: (0, 0)),
                  wspec,
                  pl.BlockSpec((DK, rows), lambda i: (0, 0)),
                  pl.BlockSpec((DK, rows), lambda i: (0, 0))],
        out_specs=pl.BlockSpec((2 * HK, rows), lambda i: (0, 0)),
        out_shape=jax.ShapeDtypeStruct((2 * HK, rows), F32),
        compiler_params=_cp(1),
        name="norm_matmul_t",
    )(x, g, wop, cos_t, sin_t)


MLP_TF = 512


def _mlp_tail(h1, g_ref, w1_ref, w2_ref, gf_ref, o_ref):
    xn = _rms(h1, g_ref[...]).astype(BF16)
    acc = None
    for f in range(D_FF // MLP_TF):
        a = _dot(xn, w1_ref[:, f * MLP_TF:(f + 1) * MLP_TF])
        a = jnp.square(jnp.maximum(a, 0.0)).astype(BF16)
        part = _dot(a, w2_ref[f * MLP_TF:(f + 1) * MLP_TF, :])
        acc = part if acc is None else acc + part
    out = h1 + acc
    o_ref[...] = out if gf_ref is None else _rms(out, gf_ref[...])


def _layer_resident(arr, layer):
    return pl.BlockSpec((None,) + arr.shape[1:], lambda *idx: (layer, 0, 0), pipeline_mode=pl.Buffered(1))


def _tail_specs(mlp, layer, final):
    g, w1, w2, gf = mlp
    specs = [_layer_resident(g, layer), _layer_resident(w1, layer), _layer_resident(w2, layer)]
    ops = [g, w1, w2]
    if final:
        specs.append(_resident(gf.shape))
        ops.append(gf)
    return specs, ops


def _ret_tail_kernel(a_ref, wo_ref, h_ref, g_ref, w1_ref, w2_ref, *rest):
    gf_ref, o_ref = (rest[0], rest[1]) if len(rest) == 2 else (None, rest[0])
    h1 = h_ref[...] + _dot(a_ref[...].astype(BF16), wo_ref[...])
    _mlp_tail(h1, g_ref, w1_ref, w2_ref, gf_ref, o_ref)


def _ret_tail(a, wo, h, mlp, *, layer, final, tm):
    M, K = a.shape
    tspecs, tops = _tail_specs(mlp, layer, final)
    row = pl.BlockSpec((tm, D), lambda i: (i, 0))
    return pl.pallas_call(
        _ret_tail_kernel,
        grid=(M // tm,),
        in_specs=[pl.BlockSpec((tm, K), lambda i: (i, 0)), _resident((K, D)), row] + tspecs,
        out_specs=row,
        out_shape=jax.ShapeDtypeStruct((M, D), F32),
        compiler_params=_cp(1),
        name="ret_tail",
    )(a, wo, h, *tops)


def _log_gamma(h):
    return math.log(1.0 - 2.0 ** (-5.0 - h))


def _ret_chunk_kernel(p_ref, s0_ref, y_ref, so_ref, s_ref, *, L, nsub, nC):
    c = pl.program_id(1)

    @pl.when(c == 0)
    def _():
        for h in range(RH):
            s_ref[h] = s0_ref[0, h]

    ti = lax.broadcasted_iota(jnp.int32, (L, L), 0)
    si = lax.broadcasted_iota(jnp.int32, (L, L), 1)
    diff = (ti - si).astype(F32)
    ri = lax.broadcasted_iota(jnp.int32, (L, 1), 0).astype(F32)
    for j in range(nsub):
        rows = slice(j * L, (j + 1) * L)
        for h in range(RH):
            lg = _log_gamma(h)
            mask = jnp.where(diff >= 0, jnp.exp(jnp.maximum(diff, 0.0) * lg), 0.0)
            qd = jnp.exp((ri + 1.0) * lg)
            kd = jnp.exp((L - 1.0 - ri) * lg)
            q = p_ref[rows, h * DK:(h + 1) * DK]
            k = p_ref[rows, HK + h * DK:HK + (h + 1) * DK]
            v = p_ref[rows, 2 * HK + h * DV:2 * HK + (h + 1) * DV].astype(BF16)
            g = p_ref[rows, 2 * HK + HV + h * DV:2 * HK + HV + (h + 1) * DV].astype(F32)
            sc = lax.dot_general(q.astype(BF16), k.astype(BF16), NT, preferred_element_type=F32) * mask
            inner = _dot(sc.astype(BF16), v)
            s_old = s_ref[h]
            cross = _dot((q.astype(F32) * qd).astype(BF16), s_old.astype(BF16))
            o = inner + cross
            s_ref[h] = s_old * math.exp(L * lg) + lax.dot_general(
                (k.astype(F32) * kd).astype(BF16), v, TN, preferred_element_type=F32)
            o = o * lax.rsqrt(jnp.mean(o * o, axis=-1, keepdims=True) + EPS)
            y_ref[rows, h * DV:(h + 1) * DV] = (g * _sigmoid(g) * o).astype(y_ref.dtype)

    @pl.when(c == nC - 1)
    def _():
        so_ref[0] = s_ref[...]


def _ret_chunk(p, s0, *, nb, nC, L, nsub, row_block0, out_dtype):
    shared = s0.shape[0] == 1
    R = L * nsub
    return pl.pallas_call(
        functools.partial(_ret_chunk_kernel, L=L, nsub=nsub, nC=nC),
        grid=(nb, nC),
        in_specs=[pl.BlockSpec((R, 2 * HK + 2 * HV), lambda b, c: (row_block0 + b * nC + c, 0)),
                  pl.BlockSpec((1, RH, DK, DV), lambda b, c: (0 if shared else b, 0, 0, 0))],
        out_specs=[pl.BlockSpec((R, HV), lambda b, c: (b * nC + c, 0)),
                   pl.BlockSpec((1, RH, DK, DV), lambda b, c: (b, 0, 0, 0))],
        out_shape=[jax.ShapeDtypeStruct((nb * nC * R, HV), out_dtype),
                   jax.ShapeDtypeStruct((nb, RH, DK, DV), F32)],
        scratch_shapes=[pltpu.VMEM((RH, DK, DV), F32)],
        compiler_params=_cp(2),
        name="ret_chunk",
    )(p, s0)


def _ret_dec_kernel(qkt_ref, p_ref, s_ref, y_ref, so_ref, *, bb, nbatch):
    step = pl.program_id(0)
    lane = lax.broadcasted_iota(jnp.int32, (1, nbatch), 1)

    def body(jb, carry):
        b = step * bb + jb
        onehot = (lane == b).astype(F32)
        rows = []
        for h in range(RH):
            gam = math.exp(_log_gamma(h))
            vrow = p_ref[pl.ds(b, 1), 2 * HK + h * DV:2 * HK + (h + 1) * DV]
            qcol = jnp.sum(qkt_ref[h * DK:(h + 1) * DK, :] * onehot, axis=1, keepdims=True)
            kcol = jnp.sum(qkt_ref[HK + h * DK:HK + (h + 1) * DK, :] * onehot, axis=1, keepdims=True)
            sn = gam * s_ref[jb, h] + kcol * vrow
            so_ref[jb, h] = sn
            o = jnp.sum(qcol * sn, axis=0, keepdims=True)
            o = o * lax.rsqrt(jnp.mean(o * o, axis=-1, keepdims=True) + EPS)
            g = p_ref[pl.ds(b, 1), 2 * HK + HV + h * DV:2 * HK + HV + (h + 1) * DV]
            rows.append(g * _sigmoid(g) * o)
        y_ref[b] = jnp.concatenate(rows, axis=1)
        return carry

    lax.fori_loop(0, bb, body, 0)


def _ret_dec(qkt, p, s, *, bb):
    nbatch = s.shape[0]
    return pl.pallas_call(
        functools.partial(_ret_dec_kernel, bb=bb, nbatch=nbatch),
        grid=(nbatch // bb,),
        in_specs=[pl.BlockSpec(qkt.shape, lambda i: (0, 0)),
                  pl.BlockSpec(p.shape, lambda i: (0, 0)),
                  pl.BlockSpec((bb, RH, DK, DV), lambda i: (i, 0, 0, 0))],
        out_specs=[pl.BlockSpec((nbatch, 1, HV), lambda i: (0, 0, 0)),
                   pl.BlockSpec((bb, RH, DK, DV), lambda i: (i, 0, 0, 0))],
        out_shape=[jax.ShapeDtypeStruct((nbatch, 1, HV), F32),
                   jax.ShapeDtypeStruct(s.shape, F32)],
        compiler_params=_cp(1),
        name="ret_dec",
    )(qkt, p, s)


N_RWKV_OUT = 8


def _rwkv_core(xn, xprev, w_refs, out_refs):
    (mix_ref, wrkv_ref, w0_ref, w1_ref, w2_ref, a0_ref, a1_ref, a2_ref, g1_ref, g2_ref,
     kk_ref, ka_ref, rk_ref, eh_ref, eht_ref) = w_refs
    r_o, lw_o, k_o, v_o, kk_o, b_o, g_o, bonus_o = out_refs
    xx = xprev - xn

    def xm(j):
        return (xn + xx * mix_ref[j:j + 1, :]).astype(BF16)

    r = _dot(xm(0), wrkv_ref[0])
    k = _dot(xm(1), wrkv_ref[1])
    v = _dot(xm(2), wrkv_ref[2])
    wl = _dot(jnp.tanh(_dot(xm(3), w1_ref[...])).astype(BF16), w2_ref[...])
    lw_o[...] = -math.exp(-0.5) * _sigmoid(w0_ref[...] + wl)
    al = _dot(_dot(xm(4), a1_ref[...]).astype(BF16), a2_ref[...])
    a = _sigmoid(a0_ref[...] + al)
    g_o[...] = _dot(_sigmoid(_dot(xm(5), g1_ref[...])).astype(BF16), g2_ref[...]).astype(g_o.dtype)
    kk = k * kk_ref[...]
    ssq = _seg_sum(kk * kk, eh_ref[...], eht_ref[...])
    kkn = kk / jnp.maximum(jnp.sqrt(ssq), 1e-12)
    k2 = k * (1.0 + (a - 1.0) * ka_ref[...])
    rk = _seg_sum(r * k2 * rk_ref[...], eh_ref[...], eht_ref[...])
    r_o[...] = r.astype(r_o.dtype)
    k_o[...] = k2.astype(k_o.dtype)
    v_o[...] = v.astype(v_o.dtype)
    kk_o[...] = kkn.astype(kk_o.dtype)
    b_o[...] = (kkn * a).astype(b_o.dtype)
    bonus_o[...] = (rk * v).astype(bonus_o.dtype)


def _rwkv_proj_main_kernel(h_ref, hp_ref, sh0_ref, g_ref, *refs, tm, tps):
    w_refs = refs[:15]
    out_refs = refs[15:15 + N_RWKV_OUT]
    sho_ref = refs[15 + N_RWKV_OUT]
    i = pl.program_id(0)
    xn = _rms(h_ref[...], g_ref[...])
    prevn = _rms(hp_ref[...], g_ref[...])[7:8, :]
    prev = jnp.where(i % tps == 0, sh0_ref[...], prevn)
    row = lax.broadcasted_iota(jnp.int32, (tm, 1), 0)
    xprev = jnp.where(row == 0, prev, pltpu.roll(xn, 1, 0))
    _rwkv_core(xn, xprev, w_refs, out_refs)
    sho_ref[0] = xn[tm - 1:tm, :]


def _rwkv_proj_small_kernel(h_ref, shp_ref, g_ref, *refs, n_dec):
    w_refs = refs[:15]
    out_refs = refs[15:15 + N_RWKV_OUT]
    xn_ref = refs[15 + N_RWKV_OUT]
    t_refs = refs[16 + N_RWKV_OUT:]
    m = h_ref.shape[0]
    xn = _rms(h_ref[...], g_ref[...])
    row = lax.broadcasted_iota(jnp.int32, (m, 1), 0)
    xprev = jnp.where(row <= n_dec, shp_ref[...], pltpu.roll(xn, 1, 0))
    _rwkv_core(xn, xprev, w_refs, out_refs)
    xn_ref[...] = xn
    for src, dst in zip(out_refs[:6], t_refs):
        dst[...] = src[0:n_dec, :].T


def _rwkv_weight_specs(wts):
    return [_resident(w.shape) for w in wts]


def _rwkv_proj_main(h, sh0, g, wts, *, tm, seq):
    M = h.shape[0]
    tps = seq // tm
    nseq = M // seq
    odt = [BF16, F32, BF16, BF16, BF16, BF16, BF16, BF16]
    row_spec = pl.BlockSpec((tm, D), lambda i: (i, 0))
    return pl.pallas_call(
        functools.partial(_rwkv_proj_main_kernel, tm=tm, tps=tps),
        grid=(M // tm,),
        in_specs=[row_spec,
                  pl.BlockSpec((8, D), lambda i: (jnp.maximum(i * (tm // 8) - 1, 0), 0)),
                  pl.BlockSpec((1, D), lambda i: (0, 0)),
                  pl.BlockSpec((1, D), lambda i: (0, 0))] + _rwkv_weight_specs(wts),
        out_specs=[row_spec] * N_RWKV_OUT + [pl.BlockSpec((1, 1, D), lambda i: (i // tps, 0, 0))],
        out_shape=[jax.ShapeDtypeStruct((M, D), dt) for dt in odt] + [jax.ShapeDtypeStruct((nseq, 1, D), F32)],
        compiler_params=_cp(1),
        name="rwkv_proj_main",
    )(h, h, sh0, g, *wts)


def _rwkv_proj_small(h, shp, g, wts, *, n_dec):
    M = h.shape[0]
    full = pl.BlockSpec((M, D), lambda i: (0, 0))
    return pl.pallas_call(
        functools.partial(_rwkv_proj_small_kernel, n_dec=n_dec),
        grid=(1,),
        in_specs=[full, full, pl.BlockSpec((1, D), lambda i: (0, 0))] + _rwkv_weight_specs(wts),
        out_specs=[full] * (N_RWKV_OUT + 1) + [pl.BlockSpec((D, n_dec), lambda i: (0, 0))] * 6,
        out_shape=[jax.ShapeDtypeStruct((M, D), F32)] * (N_RWKV_OUT + 1) + [jax.ShapeDtypeStruct((D, n_dec), F32)] * 6,
        compiler_params=_cp(1),
        name="rwkv_proj_small",
    )(h, shp, g, *wts)


def _chain1(cat, C):
    cb = cat.astype(BF16)
    return _dot(cb[:, 0:C], cb)


def _chain3(cat, C):
    hi, lo = _split(cat)
    r = _dot(jnp.concatenate([hi[:, 0:C], lo[:, 0:C]], axis=0), hi)
    return r[0:C] + r[C:2 * C] + _dot(hi[:, 0:C], lo)


def _wkv_sweep(rows, r_ref, lw_ref, k_ref, v_ref, kk_ref, b_ref, y_ref, s_ref, *, C, G):
    ti = lax.broadcasted_iota(jnp.int32, (C, C), 0)
    si = lax.broadcasted_iota(jnp.int32, (C, C), 1)
    incl = si <= ti
    strict = si < ti
    tri = jnp.where(incl, 1.0, 0.0).astype(BF16)
    t2 = lax.broadcasted_iota(jnp.int32, (C, 2 * C), 0)
    s2 = lax.broadcasted_iota(jnp.int32, (C, 2 * C), 1)
    incl2 = jnp.where(s2 >= C, s2 - C, s2) <= t2

    at, rt, kt, bt, ke, be, wc, vb = [], [], [], [], [], [], [], []
    for g in range(G):
        lw = lw_ref[g, rows, :].astype(F32)
        l0, l1 = _split(lw)
        cum = _dot(tri, l0) + _dot(tri, l1)
        tot = cum[C - 1:C, :]
        w_in = jnp.exp(-cum)
        w_end = jnp.exp(tot - cum)
        kf = k_ref[g, rows, :].astype(F32)
        bf = b_ref[g, rows, :].astype(F32)
        rt.append((r_ref[g, rows, :].astype(F32) * jnp.exp(cum)).astype(BF16))
        at.append((-kk_ref[g, rows, :].astype(F32) * jnp.exp(cum - lw)).astype(BF16))
        kt.append((kf * w_in).astype(BF16))
        bt.append((bf * w_in).astype(BF16))
        ke.append((kf * w_end).astype(BF16))
        be.append((bf * w_end).astype(BF16))
        wc.append(jnp.exp(tot))
        vb.append(v_ref[g, rows, :].astype(BF16))

    items = [(g, h) for g in range(G) for h in range(WH)]
    n = range(len(items))
    sl = lambda h: slice(h * WN, (h + 1) * WN)
    s_old = [s_ref[g, h] for g, h in items]
    lhs = [jnp.concatenate([at[g][:, sl(h)], rt[g][:, sl(h)]], axis=0) for g, h in items]
    rhs = [jnp.concatenate([bt[g][:, sl(h)], kt[g][:, sl(h)], s_old[i].astype(BF16)], axis=0)
           for i, (g, h) in enumerate(items)]
    gm = [lax.dot_general(lhs[i], rhs[i], NT, preferred_element_type=F32) for i in n]
    lh = [m[:, 2 * C:2 * C + WN] for m in gm]
    vh = [vb[g][:, sl(h)] for g, h in items]
    a_ak = [jnp.where(strict, m[0:C, C:2 * C], 0.0).astype(BF16) for m in gm]
    a_r = [jnp.where(incl2, m[C:2 * C, 0:2 * C], 0.0).astype(BF16) for m in gm]
    cat = [jnp.concatenate([jnp.where(strict, gm[i][0:C, 0:C], 0.0), lh[i][0:C] + _dot(a_ak[i], vh[i])], axis=1)
           for i in n]
    keep_x = lax.broadcasted_iota(jnp.int32, (C, C + WN), 1) >= C
    nlev = int(math.log2(C))
    for lv in range(nlev - 1):
        step = _chain3 if lv < INV_HI_LEVELS else _chain1
        cat = [step(cat[i], C) + jnp.where(keep_x, cat[i], 0.0) for i in n]
    u = [cat[i][:, C:C + WN] + _chain1(cat[i], C)[:, C:C + WN] for i in n]
    uv = [jnp.concatenate([u[i].astype(BF16), vh[i]], axis=0) for i in n]
    y = [lh[i][C:2 * C] + _dot(a_r[i], uv[i]) for i in n]
    for g in range(G):
        y_ref[g, rows, :] = jnp.concatenate(y[g * WH:(g + 1) * WH], axis=1)
    for i, (g, h) in enumerate(items):
        s_ref[g, h] = s_old[i] * wc[g][:, sl(h)] + lax.dot_general(
            uv[i], jnp.concatenate([be[g][:, sl(h)], ke[g][:, sl(h)]], axis=0), TN, preferred_element_type=F32)


def _wkv_chunk_kernel(r_ref, lw_ref, k_ref, v_ref, kk_ref, b_ref, s0_ref, y_ref, so_ref, s_ref, *, C, nsub, nC, G, shared):
    c = pl.program_id(1)

    @pl.when(c == 0)
    def _():
        for g in range(G):
            s_ref[g] = s0_ref[0 if shared else g]

    for j in range(nsub):
        _wkv_sweep(slice(j * C, (j + 1) * C), r_ref, lw_ref, k_ref, v_ref, kk_ref, b_ref, y_ref, s_ref, C=C, G=G)

    @pl.when(c == nC - 1)
    def _():
        so_ref[...] = s_ref[...]


def _wkv_chunk(ins, s0, *, nC, C, nsub, G, row_block0):
    nseq = ins[0].shape[0]
    shared = s0.shape[0] == 1
    R = C * nsub
    row_spec = pl.BlockSpec((G, R, D), lambda b, c: (b, row_block0 + c, 0))
    st_spec = pl.BlockSpec((G, WH, WN, WN), lambda b, c: (b, 0, 0, 0))
    s0_spec = pl.BlockSpec((1, WH, WN, WN), lambda b, c: (0, 0, 0, 0)) if shared else st_spec
    return pl.pallas_call(
        functools.partial(_wkv_chunk_kernel, C=C, nsub=nsub, nC=nC, G=G, shared=shared),
        grid=(nseq // G, nC),
        in_specs=[row_spec] * 6 + [s0_spec],
        out_specs=[pl.BlockSpec((G, R, D), lambda b, c: (b, c, 0)), st_spec],
        out_shape=[jax.ShapeDtypeStruct((nseq, nC * R, D), F32),
                   jax.ShapeDtypeStruct((nseq, WH, WN, WN), F32)],
        scratch_shapes=[pltpu.VMEM((G, WH, WN, WN), F32)],
        compiler_params=_cp(2),
        name="wkv_chunk",
    )(*ins, s0)


def _wkv_dec_kernel(r_ref, lw_ref, k_ref, v_ref, kk_ref, b_ref, s_ref, y_ref, so_ref):
    nkk = -kk_ref[...]
    w = jnp.exp(lw_ref[...])
    bb = b_ref[...]
    k2 = k_ref[...]
    rr = r_ref[...]
    for vi in range(WN):
        s_old = s_ref[0, vi]
        sa = jnp.sum(s_old * nkk, axis=0, keepdims=True)
        sn = s_old * w + sa * bb + v_ref[vi:vi + 1, :] * k2
        so_ref[0, vi] = sn
        y_ref[vi:vi + 1, :] = jnp.sum(sn * rr, axis=0, keepdims=True)


def _wkv_dec(ins, s):
    nbatch = s.shape[-1]
    vec = pl.BlockSpec((WN, nbatch), lambda h: (h, 0))
    st = pl.BlockSpec((1, WN, WN, nbatch), lambda h: (h, 0, 0, 0))
    return pl.pallas_call(
        _wkv_dec_kernel,
        grid=(WH,),
        in_specs=[vec] * 6 + [st],
        out_specs=[vec, st],
        out_shape=[jax.ShapeDtypeStruct((D, nbatch), F32), jax.ShapeDtypeStruct(s.shape, F32)],
        compiler_params=_cp(1),
        name="wkv_dec",
    )(*ins, s)


def _rwkv_tail_kernel(y_ref, bonus_ref, gate_ref, lng_ref, lnb_ref, wo_ref, eh_ref, eht_ref, h_ref,
                      g_ref, w1_ref, w2_ref, o_ref):
    y = y_ref[...]
    eh = eh_ref[...]
    eht = eht_ref[...]
    mu = _seg_sum(y, eh, eht) * (1.0 / WN)
    d = y - mu
    var = _seg_sum(d * d, eh, eht) * (1.0 / WN)
    yn = d * lax.rsqrt(var + GN_EPS) * lng_ref[...] + lnb_ref[...]
    z = (yn + bonus_ref[...].astype(F32)) * gate_ref[...].astype(F32)
    h1 = h_ref[...] + _dot(z.astype(BF16), wo_ref[...])
    _mlp_tail(h1, g_ref, w1_ref, w2_ref, None, o_ref)


def _rwkv_tail(y, bonus, gate, h, lng, lnb, wo, eh, eht, mlp, *, layer, tm):
    M = h.shape[0]
    tspecs, tops = _tail_specs(mlp, layer, False)
    row = pl.BlockSpec((tm, D), lambda i: (i, 0))
    return pl.pallas_call(
        _rwkv_tail_kernel,
        grid=(M // tm,),
        in_specs=[row, row, row, _resident((1, D)), _resident((1, D)), _resident((D, D)),
                  _resident(eh.shape), _resident(eht.shape), row] + tspecs,
        out_specs=row,
        out_shape=jax.ShapeDtypeStruct((M, D), F32),
        compiler_params=_cp(1),
        name="rwkv_tail",
    )(y, bonus, gate, lng, lnb, wo, eh, eht, h, *tops)


def _conv_main_kernel(p_ref, pp_ref, buf_ref, cw_ref, wo_ref, h_ref, g_ref, w1_ref, w2_ref, o_ref, nb_ref, *, tm, tps):
    i = pl.program_id(0)
    bq = p_ref[:, 0:D].astype(F32)
    u = p_ref[:, D:2 * D].astype(F32) * p_ref[:, 2 * D:3 * D].astype(F32)
    up = pp_ref[:, D:2 * D].astype(F32) * pp_ref[:, 2 * D:3 * D].astype(F32)
    npr = pp_ref.shape[0]
    first = i % tps == 0
    prev1 = jnp.where(first, buf_ref[1:2, :], up[npr - 1:npr, :])
    prev2 = jnp.where(first, buf_ref[0:1, :], up[npr - 2:npr - 1, :])
    row = lax.broadcasted_iota(jnp.int32, (tm, 1), 0)
    m1 = jnp.where(row == 0, prev1, pltpu.roll(u, 1, 0))
    m2 = jnp.where(row == 0, prev2, jnp.where(row == 1, prev1, pltpu.roll(u, 2, 0)))
    y = cw_ref[0:1, :] * m2 + cw_ref[1:2, :] * m1 + cw_ref[2:3, :] * u
    nb_ref[0] = u[tm - 2:tm, :]
    h1 = h_ref[...] + _dot((bq * y).astype(BF16), wo_ref[...])
    _mlp_tail(h1, g_ref, w1_ref, w2_ref, None, o_ref)


def _conv_main(p, buf, cw, wo, h, mlp, *, layer, tm, seq):
    M = h.shape[0]
    tps = seq // tm
    npr = 16
    tspecs, tops = _tail_specs(mlp, layer, False)
    return pl.pallas_call(
        functools.partial(_conv_main_kernel, tm=tm, tps=tps),
        grid=(M // tm,),
        in_specs=[pl.BlockSpec((tm, 3 * D), lambda i: (i, 0)),
                  pl.BlockSpec((npr, 3 * D), lambda i: (jnp.maximum(i * (tm // npr) - 1, 0), 0)),
                  _resident((2, D)), _resident((3, D)), _resident((D, D)),
                  pl.BlockSpec((tm, D), lambda i: (i, 0))] + tspecs,
        out_specs=[pl.BlockSpec((tm, D), lambda i: (i, 0)),
                   pl.BlockSpec((1, 2, D), lambda i: (i // tps, 0, 0))],
        out_shape=[jax.ShapeDtypeStruct((M, D), F32), jax.ShapeDtypeStruct((M // seq, 2, D), F32)],
        compiler_params=_cp(1),
        name="conv_main",
    )(p, p, buf, cw, wo, h, *tops)


def _conv_small_kernel(p_ref, b0_ref, b1_ref, cw_ref, wo_ref, h_ref, g_ref, w1_ref, w2_ref, o_ref, u_ref, *, n_dec):
    m = h_ref.shape[0]
    bq = p_ref[:, 0:D]
    u = p_ref[:, D:2 * D] * p_ref[:, 2 * D:3 * D]
    row = lax.broadcasted_iota(jnp.int32, (m, 1), 0)
    m1 = jnp.where(row <= n_dec, b1_ref[...], pltpu.roll(u, 1, 0))
    m2 = jnp.where(row <= n_dec + 1, b0_ref[...], pltpu.roll(u, 2, 0))
    y = cw_ref[0:1, :] * m2 + cw_ref[1:2, :] * m1 + cw_ref[2:3, :] * u
    u_ref[...] = u
    h1 = h_ref[...] + _dot((bq * y).astype(BF16), wo_ref[...])
    _mlp_tail(h1, g_ref, w1_ref, w2_ref, None, o_ref)


def _conv_small(p, b0, b1, cw, wo, h, mlp, *, layer, n_dec):
    M = h.shape[0]
    full = pl.BlockSpec((M, D), lambda i: (0, 0))
    tspecs, tops = _tail_specs(mlp, layer, False)
    return pl.pallas_call(
        functools.partial(_conv_small_kernel, n_dec=n_dec),
        grid=(1,),
        in_specs=[pl.BlockSpec((M, 3 * D), lambda i: (0, 0)), full, full,
                  pl.BlockSpec((3, D), lambda i: (0, 0)),
                  pl.BlockSpec((D, D), lambda i: (0, 0)), full] + tspecs,
        out_specs=[full, full],
        out_shape=[jax.ShapeDtypeStruct((M, D), F32)] * 2,
        compiler_params=_cp(1),
        name="conv_small",
    )(p, b0, b1, cw, wo, h, *tops)


def _rope_tables(pos):
    inv = 1.0 / (ROPE_BASE ** jnp.linspace(0.0, 1.0, DK // 2, dtype=F32))
    ang = pos.astype(F32)[:, None] * inv[None, :]
    return jnp.cos(ang), jnp.sin(ang)


def kernel(x_prompt, x_sample, state_ret_l0, state_rwkv_shift_l1, state_rwkv_wkv_l1, state_conv_l2, state_ret_l3,
           meta_tokens, norm_mix, norm_mlp, norm_final, ret_w_in, ret_w_out, rwkv_mix, rwkv_w_rkv, rwkv_w0, rwkv_w1,
           rwkv_w2, rwkv_a0, rwkv_a1, rwkv_a2, rwkv_g1, rwkv_g2, rwkv_k_k, rwkv_k_a, rwkv_r_k, rwkv_ln_g, rwkv_ln_b,
           rwkv_w_o, conv_w_in, conv_w, conv_w_out, mlp_w1, mlp_w2):
    B, T, _ = x_prompt.shape
    NB = x_sample.shape[0]
    MS = NB + N_META
    past_len = 16384

    ret_wb = ret_w_in.astype(BF16)
    ret_wp = [((ret_wb, j, 0, 2 * HK), (ret_wb, j, 2 * HK, 2 * HK), (ret_wb, j, 4 * HK, 2 * HK)) for j in range(2)]
    ret_wo = [ret_w_out[0].astype(BF16), ret_w_out[1].astype(BF16)]
    mlp = (norm_mlp.reshape(4, 1, D).astype(F32), mlp_w1.astype(BF16), mlp_w2.astype(BF16),
           norm_final.reshape(1, D).astype(F32))
    head_of_lane = jnp.arange(D) // WN
    eh = (head_of_lane[:, None] == jnp.arange(LANES)[None, :]).astype(BF16)
    eht = eh.T
    vec = lambda a: a.reshape(1, D).astype(F32)
    rwkv_wts = (rwkv_mix.astype(F32), rwkv_w_rkv.astype(BF16), vec(rwkv_w0), rwkv_w1.astype(BF16), rwkv_w2.astype(BF16),
                vec(rwkv_a0), rwkv_a1.astype(BF16), rwkv_a2.astype(BF16), rwkv_g1.astype(BF16), rwkv_g2.astype(BF16),
                vec(rwkv_k_k), vec(rwkv_k_a), vec(rwkv_r_k), eh, eht)
    wo_rwkv = rwkv_w_o.astype(BF16)
    conv_wi = conv_w_in.astype(BF16)
    conv_wo = conv_w_out.astype(BF16)
    conv_wf = conv_w.astype(F32)

    pair_sign = jnp.where(jnp.arange(DK) % 2 == 0, -1.0, 1.0).astype(F32)

    def rope_rows(pos):
        c, sn = _rope_tables(pos)
        return jnp.repeat(c, 2, axis=1), jnp.repeat(sn, 2, axis=1) * pair_sign[None, :]

    cos_m, sin_m = rope_rows(N_META + jnp.arange(T))
    pos_s = jnp.concatenate([jnp.full((NB,), past_len, jnp.int32), jnp.arange(N_META, dtype=jnp.int32)])
    cos_s, sin_s = rope_rows(pos_s)
    cos_t, sin_t = cos_s[:NB].T, sin_s[:NB].T

    h = jnp.concatenate([x_sample.reshape(NB, D), meta_tokens.astype(F32)], axis=0)
    meta_blk = NB // N_META
    zero_ret = jnp.zeros((1, RH, DK, DV), F32)
    ret_meta, ret_dec_out = [], []
    for i in range(4):
        g_mix = norm_mix[i].reshape(1, D)
        kind = i % 3
        if kind == 0:
            j = i // 3
            p = _norm_matmul(h, g_mix, ret_wp[j], tm=MS, tn=1024, out_dtype=F32, rope=(cos_s, sin_s))
            qkt = _norm_matmul_t(h, g_mix, ret_wp[j][0], cos_t, sin_t, rows=NB)
            s_in = (state_ret_l0 if j == 0 else state_ret_l3).astype(F32)
            y_dec, s_dec = _ret_dec(qkt, p, s_in, bb=4)
            y_meta, s_meta = _ret_chunk(p, zero_ret, nb=1, nC=1, L=N_META, nsub=1, row_block0=meta_blk, out_dtype=F32)
            ret_meta.append(s_meta)
            ret_dec_out.append(s_dec)
            h = _ret_tail(jnp.concatenate([y_dec.reshape(NB, HV), y_meta], axis=0), ret_wo[j], h, mlp,
                          layer=i, final=i == 3, tm=MS)
        elif kind == 1:
            shp = jnp.concatenate([state_rwkv_shift_l1.astype(F32), jnp.zeros((N_META, D), F32)], axis=0)
            outs = _rwkv_proj_small(h, shp, g_mix, rwkv_wts, n_dec=NB)
            r_, lw_, k_, v_, kk_, b_, gate_, bonus_, xn_ = outs[:9]
            y_dec, wkv_dec = _wkv_dec(outs[9:], jnp.transpose(state_rwkv_wkv_l1.astype(F32), (1, 2, 3, 0)))
            wkv_dec = jnp.transpose(wkv_dec, (3, 0, 1, 2))
            y_meta, wkv_meta = _wkv_chunk([t[None] for t in (r_, lw_, k_, v_, kk_, b_)], jnp.zeros((1, WH, WN, WN), F32),
                                          nC=1, C=N_META, nsub=1, G=1, row_block0=meta_blk)
            y_meta = y_meta[0]
            shift_dec = xn_[:NB]
            shift_meta = xn_[MS - 1:MS]
            h = _rwkv_tail(jnp.concatenate([y_dec.T, y_meta], axis=0), bonus_, gate_, h, vec(rwkv_ln_g),
                           vec(rwkv_ln_b), wo_rwkv, eh, eht, mlp, layer=i, tm=MS)
        else:
            p = _norm_matmul(h, g_mix, (conv_wi,), tm=MS, tn=1024, out_dtype=F32)
            zpad = jnp.zeros((N_META, D), F32)
            b0 = jnp.concatenate([state_conv_l2[:, 0].astype(F32), zpad], axis=0)
            b1 = jnp.concatenate([state_conv_l2[:, 1].astype(F32), zpad], axis=0)
            h, u = _conv_small(p, b0, b1, conv_wf, conv_wo, h, mlp, layer=i, n_dec=NB)
            conv_dec = jnp.stack([state_conv_l2[:, 1].astype(F32), u[:NB]], axis=1)
            conv_meta = u[MS - 2:MS]
    y_sample = h[:NB].reshape(NB, 1, D)

    h = x_prompt.reshape(B * T, D)
    nC = T // RET_CHUNK
    ret_main = []
    for i in range(4):
        g_mix = norm_mix[i].reshape(1, D)
        kind = i % 3
        if kind == 0:
            j = i // 3
            p = _norm_matmul(h, g_mix, ret_wp[j], tm=512, tn=1024, out_dtype=BF16, rope=(cos_m, sin_m))
            y, s_fin = _ret_chunk(p, ret_meta[j], nb=B, nC=nC // RET_SUB, L=RET_CHUNK, nsub=RET_SUB, row_block0=0,
                                  out_dtype=BF16)
            ret_main.append(s_fin)
            h = _ret_tail(y, ret_wo[j], h, mlp, layer=i, final=i == 3, tm=512)
        elif kind == 1:
            outs = _rwkv_proj_main(h, shift_meta, g_mix, rwkv_wts, tm=512, seq=T)
            r_, lw_, k_, v_, kk_, b_, gate_, bonus_, shift_main = outs
            y, wkv_main = _wkv_chunk([t.reshape(B, T, D) for t in (r_, lw_, k_, v_, kk_, b_)], wkv_meta,
                                     nC=T // (WKV_CHUNK * WKV_SUB), C=WKV_CHUNK, nsub=WKV_SUB, G=WKV_GROUP, row_block0=0)
            y = y.reshape(B * T, D)
            h = _rwkv_tail(y, bonus_, gate_, h, vec(rwkv_ln_g), vec(rwkv_ln_b), wo_rwkv, eh, eht, mlp, layer=i, tm=512)
        else:
            p = _norm_matmul(h, g_mix, (conv_wi,), tm=512, tn=1024, out_dtype=BF16)
            h, conv_main = _conv_main(p, conv_meta, conv_wf, conv_wo, h, mlp, layer=i, tm=512, seq=T)
    y_prompt = h.reshape(B, T, D)

    return (y_prompt, y_sample, ret_main[0], ret_dec_out[0], shift_main.reshape(B, D), shift_dec,
            wkv_main, wkv_dec, conv_main, conv_dec, ret_main[1], ret_dec_out[1])
```
